```python
import math
import jax, jax.numpy as jnp
from jax import lax
import numpy as np

D_MODEL = 1024
BATCH = 32
SEQ = 256
DEPTH = 2
DEC_BATCH = 8
DEC_SEQ = 2048
PAST_LEN = 512

GRID_W = 64
EPS = 1e-6
POOL_WIDTH = 256
POOL_GROUPS = 4
POOL_WINDOWS = (2, 4, 8, 16)
HYENA_WIDTH = 256
HYENA_BANDS = 16
HYENA_EMB = 1 + 2 * HYENA_BANDS
HYENA_FFN = 64
SCONV_WIDTH = 256
NA_HEADS = 8
NA_HEAD_DIM = 64
NA_WIDTH = NA_HEADS * NA_HEAD_DIM
NA_ROWS = 8
NA_COLS = 16
QBLK_COLS = 16
KBLK_COLS = 32
CTX_QBLK = 128
N_BRANCH = 4
IN_WIDTH = POOL_WIDTH + 3 * HYENA_WIDTH + 3 * SCONV_WIDTH + 3 * NA_WIDTH
MOE_GROUPS = 4
MOE_EXPERTS_PER_GROUP = 8
MOE_TOP_K = 2
MOE_D_FF = 256

kernel_name = 'hybrid_dit_step'


def _rmsnorm(x, g):
    xf = x.astype(jnp.float32)
    y = xf * lax.rsqrt(jnp.mean(xf * xf, axis=-1, keepdims=True) + EPS)
    return (y * g.astype(jnp.float32)).astype(x.dtype)


def _conv3(u, w):
    up = jnp.pad(u, ((0, 0), (1, 1), (0, 0)))
    return w[0] * up[:, :-2] + w[1] * up[:, 1:-1] + w[2] * up[:, 2:]


def _pool_mixer(u, pool_w, pool_scale):
    L = u.shape[1]
    uf = u.astype(jnp.float32)
    cs = jnp.pad(jnp.cumsum(uf, axis=1), ((0, 0), (1, 0), (0, 0)))
    t = np.arange(L)
    gw = POOL_WIDTH // POOL_GROUPS
    outs = []
    for g, win in enumerate(POOL_WINDOWS):
        lo = np.clip(t - win // 2, 0, L)
        hi = np.clip(t - win // 2 + win, 0, L)
        seg = cs[:, :, g * gw:(g + 1) * gw]
        cnt = jnp.asarray((hi - lo)[None, :, None], jnp.float32)
        pooled = (seg[:, hi] - seg[:, lo]) / cnt - uf[:, :, g * gw:(g + 1) * gw]
        outs.append(jnp.einsum('blc,ce->ble', pooled, pool_w[g].astype(jnp.float32)))
    y = jnp.concatenate(outs, axis=-1) * pool_scale.astype(jnp.float32)
    return y.astype(u.dtype)


def _hyena_filter(L, f1, fb1, f2, fb2, f3, freq, decay):
    f32 = jnp.float32
    t = jnp.arange(L, dtype=f32)
    w = (2.0 * math.pi / L) * t
    bands = jnp.linspace(1e-4, HYENA_BANDS - 1, HYENA_BANDS, dtype=f32)
    z = jnp.concatenate([(t / (L - 1))[:, None], jnp.cos(w[:, None] * bands),
                         -jnp.sin(w[:, None] * bands)], axis=-1)
    fr = freq.astype(f32)
    hdn = jnp.sin(fr * (z @ f1.astype(f32) + fb1.astype(f32)))
    hdn = jnp.sin(fr * (hdn @ f2.astype(f32) + fb2.astype(f32)))
    filt = hdn @ f3.astype(f32)
    dist = jnp.abs(t - (L // 2)) / (L / 2)
    filt = filt * jnp.exp(-dist[:, None] * decay.astype(f32)[None, :])
    return filt / jnp.sum(jnp.abs(filt), axis=0, keepdims=True)


def _hyena_mixer(p, conv_w, f1, fb1, f2, fb2, f3, freq, decay, bias):
    L = p.shape[1]
    x0, x1, v = jnp.split(_conv3(p, conv_w), 3, axis=-1)
    u = (x1 * v).astype(jnp.float32)
    h = _hyena_filter(L, f1, fb1, f2, fb2, f3, freq, decay)
    n = 2 * L
    spec = jnp.fft.rfft(u, n=n, axis=1) * jnp.fft.rfft(h, n=n, axis=0)[None]
    conv = jnp.fft.irfft(spec, n=n, axis=1)[:, L // 2:L // 2 + L]
    y = (conv + bias.astype(jnp.float32) * u) * x0.astype(jnp.float32)
    return y.astype(p.dtype)


def _shortconv_mixer(p, w):
    xc, bg, cg = jnp.split(p, 3, axis=-1)
    return bg * _conv3(cg * xc, w)


def _context_attention(q, k, v):
    B, Tc, H, d = q.shape
    scale = d ** -0.5
    nblk = Tc // CTX_QBLK
    qb = q.reshape(B, nblk, CTX_QBLK, H, d).transpose(1, 0, 2, 3, 4)

    def blk(qi):
        s = jnp.einsum('bqhd,bkhd->bhqk', qi, k, preferred_element_type=jnp.float32) * scale
        pr = jax.nn.softmax(s, axis=-1)
        return jnp.einsum('bhqk,bkhd->bqhd', pr.astype(v.dtype), v)

    o = lax.map(blk, qb)
    return o.transpose(1, 0, 2, 3, 4).reshape(B, Tc, H * d)


def _latent_attention(q, k, v, ctx_k, ctx_v, rpb):
    B, T, H, d = q.shape
    rows = T // GRID_W
    kr = min(NA_ROWS, rows)
    scale = d ** -0.5
    qg = q.reshape(B, rows, GRID_W, H, d)
    kg = k.reshape(B, rows, GRID_W, H, d)
    vg = v.reshape(B, rows, GRID_W, H, d)
    n_cb = GRID_W // QBLK_COLS
    q_col = np.arange(GRID_W).reshape(n_cb, QBLK_COLS)
    kcol_start = np.clip(np.arange(n_cb) * QBLK_COLS - (KBLK_COLS - QBLK_COLS) // 2, 0, GRID_W - KBLK_COLS)
    k_col = kcol_start[:, None] + np.arange(KBLK_COLS)[None, :]
    win_start = np.clip(q_col - NA_COLS // 2, 0, GRID_W - NA_COLS)
    kc = k_col[:, None, :]
    col_ok = jnp.asarray((kc >= win_start[:, :, None]) & (kc < win_start[:, :, None] + NA_COLS))
    dc_idx = np.clip(kc - q_col[:, :, None] + NA_COLS - 1, 0, 2 * NA_COLS - 2)
    rpb_c = rpb.astype(jnp.float32)[:, :, dc_idx]
    n_loc = kr * KBLK_COLS

    def row_fn(r):
        rs = jnp.clip(r - kr // 2, 0, rows - kr)
        k_blk = lax.dynamic_slice_in_dim(kg, rs, kr, axis=1)[:, :, k_col]
        v_blk = lax.dynamic_slice_in_dim(vg, rs, kr, axis=1)[:, :, k_col]
        q_blk = lax.dynamic_index_in_dim(qg, r, axis=1, keepdims=False).reshape(B, n_cb, QBLK_COLS, H, d)
        s_loc = jnp.einsum('bnqhd,bknjhd->bhnqkj', q_blk, k_blk, preferred_element_type=jnp.float32) * scale
        dr = rs + jnp.arange(kr) - r + (NA_ROWS - 1)
        bias = rpb_c[:, dr].transpose(0, 2, 3, 1, 4)
        s_loc = jnp.where(col_ok[None, None, :, :, None, :], s_loc + bias[None], -jnp.inf)
        s_loc = s_loc.reshape(B, H, n_cb, QBLK_COLS, n_loc)
        s_ctx = jnp.einsum('bnqhd,bchd->bhnqc', q_blk, ctx_k, preferred_element_type=jnp.float32) * scale
        pr = jax.nn.softmax(jnp.concatenate([s_loc, s_ctx], axis=-1), axis=-1).astype(v.dtype)
        p_loc = pr[..., :n_loc].reshape(B, H, n_cb, QBLK_COLS, kr, KBLK_COLS)
        o = (jnp.einsum('bhnqkj,bknjhd->bnqhd', p_loc, v_blk)
             + jnp.einsum('bhnqc,bchd->bnqhd', pr[..., n_loc:], ctx_v))
        return o.reshape(B, GRID_W, H, d)

    out = lax.map(row_fn, jnp.arange(rows))
    return out.transpose(1, 0, 2, 3, 4).reshape(B, T, H * d)


def _mixing(h, ctx_kv, P, l):
    B, L, _ = h.shape
    p = jnp.einsum('bld,de->ble', h, P['w_in'][l])
    o1 = POOL_WIDTH
    o2 = o1 + 3 * HYENA_WIDTH
    o3 = o2 + 3 * SCONV_WIDTH
    ya = _pool_mixer(p[..., :o1], P['pool_w'][l], P['pool_scale'][l])
    yb = _hyena_mixer(p[..., o1:o2], P['hyena_conv'][l], P['hyena_f1'][l], P['hyena_fb1'][l],
                      P['hyena_f2'][l], P['hyena_fb2'][l], P['hyena_f3'][l], P['hyena_freq'][l],
                      P['hyena_decay'][l], P['hyena_bias'][l])
    yc = _shortconv_mixer(p[..., o2:o3], P['sconv_w'][l])
    q, k, v = [t.reshape(B, L, NA_HEADS, NA_HEAD_DIM) for t in jnp.split(p[..., o3:], 3, axis=-1)]
    if ctx_kv is None:
        yd = _context_attention(q, k, v)
    else:
        yd = _latent_attention(q, k, v, ctx_kv[0], ctx_kv[1], P['na_rpb'][l])
    gates = jax.nn.sigmoid(jnp.einsum('bld,de->ble', h, P['w_gate'][l]) + P['b_gate'][l])
    ga, gb, gc, gd = jnp.split(gates, N_BRANCH, axis=-1)
    merged = (ga * (ya @ P['w_br_a'][l]) + gb * (yb @ P['w_br_b'][l])
              + gc * (yc @ P['w_br_c'][l]) + gd * (yd @ P['w_br_d'][l]))
    return merged @ P['w_out'][l], (k, v)


def _moe(h, P, l):
    B, L, D = h.shape
    x = h.reshape(B * L, D)
    gl = (x @ P['w_route_group'][l] + P['b_route_group'][l]).astype(jnp.float32)
    g_prob, g_idx = lax.top_k(jax.nn.softmax(gl, axis=-1), 1)
    g_sel = jax.nn.one_hot(g_idx[:, 0], MOE_GROUPS, dtype=jnp.float32)
    el = (jnp.einsum('td,gde->tge', x, P['w_route_exp'][l]) + P['b_route_exp'][l]).astype(jnp.float32)
    el_sel = jnp.einsum('tge,tg->te', el, g_sel)
    e_prob, e_idx = lax.top_k(jax.nn.softmax(el_sel, axis=-1), MOE_TOP_K)
    e_prob = e_prob / jnp.sum(e_prob, axis=-1, keepdims=True)
    w_exp = jnp.sum(jax.nn.one_hot(e_idx, MOE_EXPERTS_PER_GROUP, dtype=jnp.float32) * e_prob[..., None], axis=1) * g_prob
    combine = (g_sel[:, :, None] * w_exp[:, None, :]).astype(h.dtype)
    y = jnp.zeros_like(x)
    for g in range(MOE_GROUPS):
        a = jnp.einsum('td,edf->tef', x, P['w_e_gate'][l, g])
        b = jnp.einsum('td,edf->tef', x, P['w_e_up'][l, g])
        hid = jax.nn.silu(a) * b * combine[:, g, :, None]
        y = y + jnp.einsum('tef,efd->td', hid, P['w_e_down'][l, g])
    return y.reshape(B, L, D)


def _layer(x, cvec, ctx_kv, P, l):
    ada = jax.nn.silu(cvec) @ P['w_ada'][l] + P['b_ada'][l]
    sh1, sc1, g1, sh2, sc2, g2 = jnp.split(ada, 6, axis=-1)
    h = _rmsnorm(x, P['norm_mix'][l]) * (1.0 + sc1[:, None]) + sh1[:, None]
    mix, kv = _mixing(h, ctx_kv, P, l)
    x = x + g1[:, None] * mix
    h2 = _rmsnorm(x, P['norm_ffn'][l]) * (1.0 + sc2[:, None]) + sh2[:, None]
    x = x + g2[:, None] * _moe(h2, P, l)
    return x, kv


def setup_inputs(seed: int = 0) -> dict:
    key = jax.random.key(seed)
    ks = iter(jax.random.split(key, 48))
    f32 = jnp.float32

    def nrm(shape, scale):
        return jax.random.normal(next(ks), shape, f32) * scale

    D = D_MODEL
    PG = POOL_WIDTH // POOL_GROUPS
    G, E, F = MOE_GROUPS, MOE_EXPERTS_PER_GROUP, MOE_D_FF
    inp = {}
    inp['x_prompt'] = nrm((BATCH, SEQ, D), 1.0)
    inp['x_sample'] = nrm((DEC_BATCH, DEC_SEQ, D), 1.0)
    inp['cache_k'] = nrm((DEC_BATCH, DEPTH, PAST_LEN, NA_HEADS, NA_HEAD_DIM), 1.0)
    inp['cache_v'] = nrm((DEC_BATCH, DEPTH, PAST_LEN, NA_HEADS, NA_HEAD_DIM), 1.0)
    inp['c'] = nrm((DEC_BATCH, D), 1.0)
    inp['c_ctx'] = nrm((D,), 1.0)
    inp['w_ada'] = nrm((DEPTH, D, 6 * D), 0.5 * D ** -0.5)
    inp['b_ada'] = nrm((DEPTH, 6 * D), 0.1)
    inp['norm_mix'] = 1.0 + nrm((DEPTH, D), 0.1)
    inp['w_in'] = nrm((DEPTH, D, IN_WIDTH), D ** -0.5)
    inp['w_gate'] = nrm((DEPTH, D, N_BRANCH * D), D ** -0.5)
    inp['b_gate'] = nrm((DEPTH, N_BRANCH * D), 0.1)
    inp['pool_w'] = nrm((DEPTH, POOL_GROUPS, PG, PG), PG ** -0.5)
    inp['pool_scale'] = 1.0 + nrm((DEPTH, POOL_WIDTH), 0.1)
    inp['hyena_conv'] = nrm((DEPTH, 3, 3 * HYENA_WIDTH), 0.5)
    inp['hyena_f1'] = nrm((DEPTH, HYENA_EMB, HYENA_FFN), HYENA_EMB ** -0.5)
    inp['hyena_fb1'] = nrm((DEPTH, HYENA_FFN), 0.1)
    inp['hyena_f2'] = nrm((DEPTH, HYENA_FFN, HYENA_FFN), HYENA_FFN ** -0.5)
    inp['hyena_fb2'] = nrm((DEPTH, HYENA_FFN), 0.1)
    inp['hyena_f3'] = nrm((DEPTH, HYENA_FFN, HYENA_WIDTH), HYENA_FFN ** -0.5)
    inp['hyena_freq'] = 1.0 + nrm((DEPTH, HYENA_FFN), 0.1)
    inp['hyena_decay'] = jnp.exp(jax.random.uniform(next(ks), (DEPTH, HYENA_WIDTH), f32,
                                                    math.log(3.0), math.log(15.0)))
    inp['hyena_bias'] = nrm((DEPTH, HYENA_WIDTH), 0.5)
    inp['sconv_w'] = nrm((DEPTH, 3, SCONV_WIDTH), 0.5)
    inp['na_rpb'] = nrm((DEPTH, NA_HEADS, 2 * NA_ROWS - 1, 2 * NA_COLS - 1), 0.2)
    inp['w_br_a'] = nrm((DEPTH, POOL_WIDTH, D), POOL_WIDTH ** -0.5)
    inp['w_br_b'] = nrm((DEPTH, HYENA_WIDTH, D), HYENA_WIDTH ** -0.5)
    inp['w_br_c'] = nrm((DEPTH, SCONV_WIDTH, D), SCONV_WIDTH ** -0.5)
    inp['w_br_d'] = nrm((DEPTH, NA_WIDTH, D), NA_WIDTH ** -0.5)
    inp['w_out'] = nrm((DEPTH, D, D), D ** -0.5)
    inp['norm_ffn'] = 1.0 + nrm((DEPTH, D), 0.1)
    inp['w_route_group'] = nrm((DEPTH, D, G), D ** -0.5)
    inp['b_route_group'] = nrm((DEPTH, G), 0.01)
    inp['w_route_exp'] = nrm((DEPTH, G, D, E), D ** -0.5)
    inp['b_route_exp'] = nrm((DEPTH, G, E), 0.01)
    inp['w_e_gate'] = nrm((DEPTH, G, E, D, F), D ** -0.5)
    inp['w_e_up'] = nrm((DEPTH, G, E, D, F), D ** -0.5)
    inp['w_e_down'] = nrm((DEPTH, G, E, F, D), F ** -0.5)
    inp['norm_final'] = 1.0 + nrm((D,), 0.1)
    return inp


def reference(x_prompt, x_sample, cache_k, cache_v, c, c_ctx, w_ada, b_ada, norm_mix, w_in,
              w_gate, b_gate, pool_w, pool_scale, hyena_conv, hyena_f1, hyena_fb1, hyena_f2,
              hyena_fb2, hyena_f3, hyena_freq, hyena_decay, hyena_bias, sconv_w, na_rpb,
              w_br_a, w_br_b, w_br_c, w_br_d, w_out, norm_ffn, w_route_group, b_route_group,
              w_route_exp, b_route_exp, w_e_gate, w_e_up, w_e_down, norm_final):
    P = dict(w_ada=w_ada, b_ada=b_ada, norm_mix=norm_mix, w_in=w_in, w_gate=w_gate, b_gate=b_gate,
             pool_w=pool_w, pool_scale=pool_scale, hyena_conv=hyena_conv, hyena_f1=hyena_f1,
             hyena_fb1=hyena_fb1, hyena_f2=hyena_f2, hyena_fb2=hyena_fb2, hyena_f3=hyena_f3,
             hyena_freq=hyena_freq, hyena_decay=hyena_decay, hyena_bias=hyena_bias, sconv_w=sconv_w,
             na_rpb=na_rpb, w_br_a=w_br_a, w_br_b=w_br_b, w_br_c=w_br_c, w_br_d=w_br_d, w_out=w_out,
             norm_ffn=norm_ffn, w_route_group=w_route_group, b_route_group=b_route_group,
             w_route_exp=w_route_exp, b_route_exp=b_route_exp, w_e_gate=w_e_gate, w_e_up=w_e_up,
             w_e_down=w_e_down)
    xp = x_prompt
    c_p = jnp.broadcast_to(c_ctx[None, :], (x_prompt.shape[0], D_MODEL))
    k_list, v_list = [], []
    for l in range(DEPTH):
        xp, (k_l, v_l) = _layer(xp, c_p, None, P, l)
        k_list.append(k_l)
        v_list.append(v_l)
    y_prompt = _rmsnorm(xp, norm_final)
    new_k = jnp.stack(k_list, axis=1)
    new_v = jnp.stack(v_list, axis=1)
    xs = x_sample
    for l in range(DEPTH):
        xs, _ = _layer(xs, c, (cache_k[:, l], cache_v[:, l]), P, l)
    y_sample = _rmsnorm(xs, norm_final)
    return (y_prompt, y_sample, new_k, new_v)
```

```python
import functools
import math

import numpy as np
import jax
import jax.numpy as jnp
from jax import lax
from jax.experimental import pallas as pl
from jax.experimental.pallas import tpu as pltpu

F32 = jnp.float32
BF16 = jnp.bfloat16
HIGHEST = lax.Precision.HIGHEST

EPS = 1e-6
GRID_W = 64
NA_ROWS = 8
NA_COLS = 16
N_HEADS = 8
HEAD_DIM = 64
POOL_WINDOWS = (2, 4, 8, 16)
HYENA_BANDS = 16
MOE_GROUPS = 4
MOE_EXPERTS = 8
N_EXPERTS = MOE_GROUPS * MOE_EXPERTS
ROUTE_LANES = 128
EXPERT_LANE0 = MOE_GROUPS
HALO = 8
NEG_BIG = -1e30
VMEM_LIMIT_BYTES = 48 * 1024 * 1024


def _cparams(*sem):
    return pltpu.CompilerParams(dimension_semantics=sem, vmem_limit_bytes=VMEM_LIMIT_BYTES)


def _resident(shape):
    nd = len(shape)
    return pl.BlockSpec(shape, lambda *_: (0,) * nd, pipeline_mode=pl.Buffered(1))


def _dot(a, b):
    return jnp.dot(a, b, preferred_element_type=F32)


def _silu(x):
    return x * jax.nn.sigmoid(x)


def _ada_kernel(cv_ref, w_ref, b_ref, o_ref):
    o_ref[...] = jnp.dot(_silu(cv_ref[...]), w_ref[...], precision=HIGHEST,
                         preferred_element_type=F32) + b_ref[...]


def _ada(cvecs, w_ada, b_ada):
    depth, d, d6 = w_ada.shape
    r = cvecs.shape[0]
    return pl.pallas_call(
        _ada_kernel,
        grid=(depth, d6 // d),
        in_specs=[pl.BlockSpec((r, d), lambda l, j: (0, 0)),
                  pl.BlockSpec((None, d, d), lambda l, j: (l, 0, j)),
                  pl.BlockSpec((None, 1, d), lambda l, j: (l, 0, j))],
        out_specs=pl.BlockSpec((None, r, d), lambda l, j: (l, 0, j)),
        out_shape=jax.ShapeDtypeStruct((depth, r, d6), F32),
        compiler_params=_cparams("parallel", "parallel"),
        name="ada",
    )(cvecs, w_ada, b_ada.reshape(depth, 1, d6))


def _rms_mod(x, nw, shift, scale):
    y = x * lax.rsqrt(jnp.mean(x * x, axis=-1, keepdims=True) + EPS) * nw
    return y * (1.0 + scale) + shift


def _inproj_kernel(x_ref, mod_ref, nw_ref, w_ref, h_ref, ph_ref, ps_ref, q_ref, k_ref, v_ref, pp_ref,
                   *, widths):
    mod = mod_ref[...]
    h = _rms_mod(x_ref[...], nw_ref[...], mod[0:1, :], mod[1:2, :]).astype(BF16)
    h_ref[...] = h
    off = 0
    for ref, wd in zip((ph_ref, ps_ref, q_ref, k_ref, v_ref, pp_ref), widths):
        r = _dot(h, w_ref[:, off:off + wd])
        if ref is q_ref:
            r = r * (HEAD_DIM ** -0.5)
        ref[...] = r.astype(ref.dtype)
        off += wd


def _inproj(x, mod, nw, w_in_r, widths, tm, rows_per_mod):
    m, d = x.shape
    n = w_in_r.shape[1]
    dts = (F32, F32, BF16, F32, F32, F32)
    row = lambda i: (i, 0)
    return pl.pallas_call(
        functools.partial(_inproj_kernel, widths=widths),
        grid=(m // tm,),
        in_specs=[pl.BlockSpec((tm, d), row),
                  pl.BlockSpec((None, 8, d), lambda i: ((i * tm) // rows_per_mod, 0, 0)),
                  pl.BlockSpec((1, d), lambda i: (0, 0)),
                  _resident((d, n))],
        out_specs=[pl.BlockSpec((tm, d), row)] + [pl.BlockSpec((tm, wd), row) for wd in widths],
        out_shape=[jax.ShapeDtypeStruct((m, d), BF16)]
        + [jax.ShapeDtypeStruct((m, wd), dt) for wd, dt in zip(widths, dts)],
        compiler_params=_cparams("parallel"),
        name="inproj",
    )(x, mod, nw, w_in_r)


def _fill_padded(pad_ref, prev_ref, cur_ref, next_ref, first, last, tc):
    zero = jnp.zeros((HALO, cur_ref.shape[1]), F32)
    pad_ref[0:HALO, :] = jnp.where(first, zero, prev_ref[...])
    pad_ref[HALO:HALO + tc, :] = cur_ref[...]
    pad_ref[HALO + tc:2 * HALO + tc, :] = jnp.where(last, zero, next_ref[...])


def _local_kernel(pp_ref, pp_prev, pp_next, ps_ref, ps_prev, ps_next, ph_ref, ph_prev, ph_next,
                  pw_ref, pscale_ref, sw_ref, hw_ref,
                  ya_ref, yc_ref, u_ref, x0_ref,
                  pad_p, pad_s, pad_h, *, seq_len, tc):
    nchunk = seq_len // tc
    j = pl.program_id(0) % nchunk
    first = j == 0
    last = j == nchunk - 1
    cw = pp_ref.shape[1]
    sw = cw

    _fill_padded(pad_p, pp_prev, pp_ref, pp_next, first, last, tc)
    sh = lambda k: pad_p[HALO + k:HALO + k + tc, :]
    u = pp_ref[...]
    sums = {}
    acc = u
    lo_done, hi_done = 0, 0
    for win in POOL_WINDOWS:
        lo, hi = -(win // 2), win // 2 - 1
        for k in range(lo, lo_done):
            acc = acc + sh(k)
        for k in range(hi_done + 1, hi + 1):
            acc = acc + sh(k)
        lo_done, hi_done = lo, hi
        sums[win] = acc
    t = j * tc + lax.broadcasted_iota(jnp.int32, (tc, 1), 0)
    lane = lax.broadcasted_iota(jnp.int32, (1, cw), 1)
    gw = cw // len(POOL_WINDOWS)
    pooled = None
    for g, win in reversed(list(enumerate(POOL_WINDOWS))):
        cnt = jnp.minimum(t - win // 2 + win, seq_len) - jnp.maximum(t - win // 2, 0)
        val = sums[win] * (1.0 / cnt.astype(F32))
        pooled = val if pooled is None else jnp.where(lane < (g + 1) * gw, val, pooled)
    pooled = pooled - u
    ya = _dot(pooled.astype(BF16), pw_ref[...]) * pscale_ref[...]
    ya_ref[...] = ya.astype(BF16)

    _fill_padded(pad_s, ps_prev, ps_ref, ps_next, first, last, tc)
    w3 = sw_ref[...]
    z = lambda k: (pad_s[HALO + k:HALO + k + tc, 2 * sw:3 * sw] * pad_s[HALO + k:HALO + k + tc, 0:sw])
    conv = w3[0:1, :] * z(-1) + w3[1:2, :] * z(0) + w3[2:3, :] * z(1)
    yc_ref[...] = (ps_ref[:, sw:2 * sw] * conv).astype(BF16)

    _fill_padded(pad_h, ph_prev, ph_ref, ph_next, first, last, tc)
    hw = hw_ref[...]
    c3 = (hw[0:1, :] * pad_h[HALO - 1:HALO - 1 + tc, :] + hw[1:2, :] * pad_h[HALO:HALO + tc, :]
          + hw[2:3, :] * pad_h[HALO + 1:HALO + 1 + tc, :])
    x0_ref[...] = c3[:, 0:sw]
    u_ref[...] = c3[:, sw:2 * sw] * c3[:, 2 * sw:3 * sw]


def _local_mixers(pp, ps, ph, pool_bd, pool_scale, sconv_w, hyena_conv, seq_len, tc):
    m, cw = pp.shape
    nb = m // seq_len
    nchunk = seq_len // tc
    hpc = tc // HALO
    nhb = m // HALO
    row = lambda i: (i, 0)
    prev = lambda i: (jnp.maximum(i * hpc - 1, 0), 0)
    nxt = lambda i: (jnp.minimum((i + 1) * hpc, nhb - 1), 0)
    tl = lambda i: (i % nchunk, i // nchunk)
    const = lambda i: (0, 0)

    def trio(width):
        return [pl.BlockSpec((tc, width), row), pl.BlockSpec((HALO, width), prev),
                pl.BlockSpec((HALO, width), nxt)]

    return pl.pallas_call(
        functools.partial(_local_kernel, seq_len=seq_len, tc=tc),
        grid=(m // tc,),
        in_specs=trio(cw) + trio(3 * cw) + trio(3 * cw)
        + [pl.BlockSpec((cw, cw), const), pl.BlockSpec((1, cw), const),
           pl.BlockSpec((3, cw), const), pl.BlockSpec((3, 3 * cw), const)],
        out_specs=[pl.BlockSpec((tc, cw), row), pl.BlockSpec((tc, cw), row),
                   pl.BlockSpec((tc, cw), tl), pl.BlockSpec((tc, cw), tl)],
        out_shape=[jax.ShapeDtypeStruct((m, cw), BF16), jax.ShapeDtypeStruct((m, cw), BF16),
                   jax.ShapeDtypeStruct((seq_len, nb * cw), F32),
                   jax.ShapeDtypeStruct((seq_len, nb * cw), F32)],
        scratch_shapes=[pltpu.VMEM((tc + 2 * HALO, cw), F32), pltpu.VMEM((tc + 2 * HALO, 3 * cw), F32),
                        pltpu.VMEM((tc + 2 * HALO, 3 * cw), F32)],
        compiler_params=_cparams("parallel"),
        name="local_mixers",
    )(pp, pp, pp, ps, ps, ps, ph, ph, ph, pool_bd, pool_scale, sconv_w, hyena_conv)


def _trig(rows, cols, n):
    split = 64
    r = np.asarray(rows, np.int64)[:, None]
    c = np.asarray(cols, np.int64)
    assert c[0] % split == 0 and len(c) % split == 0 and np.all(np.diff(c) == 1)
    c0 = np.arange(split)[None, :]
    c1 = c[::split][None, :]
    ang0 = ((r * c0) % n) * (2.0 * math.pi / n)
    ang1 = ((r * c1) % n) * (2.0 * math.pi / n)
    tab = lambda a: jnp.asarray(a, F32)
    ca, sa = tab(np.cos(ang0))[:, None, :], tab(np.sin(ang0))[:, None, :]
    cb, sb = tab(np.cos(ang1))[:, :, None], tab(np.sin(ang1))[:, :, None]
    shape = (len(rows), len(c))
    return (ca * cb - sa * sb).reshape(shape), (sa * cb + ca * sb).reshape(shape)


def _dft_matrices(seq_len):
    L = seq_len
    n = 2 * L
    k = np.arange(L)
    cos_f, sin_f = _trig(k, k, n)
    alt = jnp.asarray(np.where(k % 2 == 0, 1.0, -1.0), F32)
    im = (-sin_f).at[0, :].set(alt)
    fwd = jnp.concatenate([cos_f, im], axis=0)
    tp = np.arange(L // 2, L // 2 + L)
    cos_i, sin_i = _trig(tp, k, n)
    alt_t = jnp.asarray(np.where(tp % 2 == 0, 1.0, -1.0), F32)
    inv_c = (cos_i * (2.0 / n)).at[:, 0].set(1.0 / n)
    inv_s = (sin_i * (-2.0 / n)).at[:, 0].set(alt_t / n)
    return fwd, inv_c.astype(BF16), inv_s.astype(BF16)


def _hyena_embedding(seq_len, width):
    t = np.arange(seq_len, dtype=np.float64)
    w = (2.0 * math.pi / seq_len) * t
    bands = np.linspace(1e-4, HYENA_BANDS - 1, HYENA_BANDS)
    z = np.concatenate([(t / (seq_len - 1))[:, None], np.cos(w[:, None] * bands),
                        -np.sin(w[:, None] * bands)], axis=-1)
    out = np.zeros((seq_len, width), np.float32)
    out[:, :z.shape[1]] = z
    return jnp.asarray(out)


def _filter_kernel(z_ref, f1_ref, fb1_ref, f2_ref, fb2_ref, f3_ref, freq_ref, decay_ref, o_ref, *, seq_len):
    hdot = lambda a, b: jnp.dot(a, b, precision=HIGHEST, preferred_element_type=F32)
    fr = freq_ref[...]
    hdn = jnp.sin(fr * (hdot(z_ref[...], f1_ref[...]) + fb1_ref[...]))
    hdn = jnp.sin(fr * (hdot(hdn, f2_ref[...]) + fb2_ref[...]))
    filt = hdot(hdn, f3_ref[...])
    t = lax.broadcasted_iota(jnp.int32, (seq_len, 1), 0)
    dist = jnp.abs(t - seq_len // 2).astype(F32) / (seq_len / 2)
    filt = filt * jnp.exp(-dist * decay_ref[...])
    o_ref[...] = filt / jnp.sum(jnp.abs(filt), axis=0, keepdims=True)


def _pad2(a, rows, cols):
    return jnp.zeros((rows, cols), F32).at[:a.shape[0], :a.shape[1]].set(a)


def _hyena_filter(seq_len, f1, fb1, f2, fb2, f3, freq, decay):
    p = 128
    c = f3.shape[1]
    args = (_hyena_embedding(seq_len, p), _pad2(f1, p, p), _pad2(fb1[None], 1, p), _pad2(f2, p, p),
            _pad2(fb2[None], 1, p), _pad2(f3, p, c), _pad2(freq[None], 1, p), decay[None])
    return pl.pallas_call(
        functools.partial(_filter_kernel, seq_len=seq_len),
        out_shape=jax.ShapeDtypeStruct((seq_len, c), F32),
        compiler_params=pltpu.CompilerParams(vmem_limit_bytes=VMEM_LIMIT_BYTES),
        name="hyena_filter",
    )(*args)


def _spectrum_kernel(f_ref, h_ref, o_ref):
    o_ref[...] = jnp.dot(f_ref[...], h_ref[...], precision=HIGHEST, preferred_element_type=F32)


def _filter_spectrum(fwd32, filt):
    n, L = fwd32.shape
    c = filt.shape[1]
    tf = min(n, 512)
    return pl.pallas_call(
        _spectrum_kernel,
        grid=(n // tf,),
        in_specs=[pl.BlockSpec((tf, L), lambda i: (i, 0)), pl.BlockSpec((L, c), lambda i: (0, 0))],
        out_specs=pl.BlockSpec((tf, c), lambda i: (i, 0)),
        out_shape=jax.ShapeDtypeStruct((n, c), F32),
        compiler_params=_cparams("parallel"),
        name="filter_spectrum",
    )(fwd32, filt)


def _hy_fwd_kernel(fc_ref, fs_ref, u_ref, hr_ref, hi_ref, yr_ref, yi_ref, *, cw):
    ub = u_ref[...].astype(BF16)
    ur = _dot(fc_ref[...], ub)
    ui = _dot(fs_ref[...], ub)
    tf = ur.shape[0]
    row0 = (pl.program_id(0) * tf + lax.broadcasted_iota(jnp.int32, (tf, 1), 0)) == 0
    hr, hi = hr_ref[...], hi_ref[...]
    for s in range(ur.shape[1] // cw):
        a, b = ur[:, s * cw:(s + 1) * cw], ui[:, s * cw:(s + 1) * cw]
        yr = jnp.where(row0, a * hr, a * hr - b * hi)
        yi = jnp.where(row0, b * hi, a * hi + b * hr)
        yr_ref[:, s * cw:(s + 1) * cw] = yr.astype(BF16)
        yi_ref[:, s * cw:(s + 1) * cw] = yi.astype(BF16)


def _hy_inv_kernel(ic_ref, is_ref, yr_ref, yi_ref, u_ref, x0_ref, bias_ref, o_ref):
    conv = _dot(ic_ref[...], yr_ref[...]) + _dot(is_ref[...], yi_ref[...])
    o_ref[...] = ((conv + bias_ref[...] * u_ref[...]) * x0_ref[...]).astype(BF16)


def _hyena_conv(u_t, x0_t, fwd_bf, inv_c, inv_s, spec, bias, cw):
    L, ncol = u_t.shape
    tf = min(L, 1024)
    tn = min(ncol, 512)
    nf = L // tf
    yr, yi = pl.pallas_call(
        functools.partial(_hy_fwd_kernel, cw=cw),
        grid=(nf, ncol // tn),
        in_specs=[pl.BlockSpec((tf, L), lambda i, j: (i, 0)),
                  pl.BlockSpec((tf, L), lambda i, j: (nf + i, 0)),
                  pl.BlockSpec((L, tn), lambda i, j: (0, j)),
                  pl.BlockSpec((tf, cw), lambda i, j: (i, 0)),
                  pl.BlockSpec((tf, cw), lambda i, j: (nf + i, 0))],
        out_specs=[pl.BlockSpec((tf, tn), lambda i, j: (i, j))] * 2,
        out_shape=[jax.ShapeDtypeStruct((L, ncol), BF16)] * 2,
        compiler_params=_cparams("parallel", "parallel"),
        name="hyena_dft",
    )(fwd_bf, fwd_bf, u_t, spec, spec)
    bias_t = jnp.tile(bias[None, :], (1, tn // cw))
    return pl.pallas_call(
        _hy_inv_kernel,
        grid=(nf, ncol // tn),
        in_specs=[pl.BlockSpec((tf, L), lambda i, j: (i, 0)),
                  pl.BlockSpec((tf, L), lambda i, j: (i, 0)),
                  pl.BlockSpec((L, tn), lambda i, j: (0, j)),
                  pl.BlockSpec((L, tn), lambda i, j: (0, j)),
                  pl.BlockSpec((tf, tn), lambda i, j: (i, j)),
                  pl.BlockSpec((tf, tn), lambda i, j: (i, j)),
                  pl.BlockSpec((1, tn), lambda i, j: (0, 0))],
        out_specs=pl.BlockSpec((tf, tn), lambda i, j: (i, j)),
        out_shape=jax.ShapeDtypeStruct((L, ncol), BF16),
        compiler_params=_cparams("parallel", "parallel"),
        name="hyena_idft",
    )(inv_c, inv_s, yr, yi, u_t, x0_t, bias_t)


def _softmax_pv(parts):
    m = None
    for s, _ in parts:
        mi = jnp.max(s, axis=-1, keepdims=True)
        m = mi if m is None else jnp.maximum(m, mi)
    den, out = None, None
    for s, v in parts:
        p = jnp.exp(s - m)
        d = jnp.sum(p, axis=-1, keepdims=True)
        o = _dot(p.astype(BF16), v)
        den = d if den is None else den + d
        out = o if out is None else out + o
    return out * (1.0 / den)


def _qkt(q, k):
    return lax.dot_general(q, k, (((1,), (1,)), ((), ())), preferred_element_type=F32)


def _ctx_attn_kernel(q_ref, k_ref, v_ref, o_ref):
    for h in range(N_HEADS):
        sl = slice(h * HEAD_DIM, (h + 1) * HEAD_DIM)
        k = k_ref[:, sl].astype(BF16)
        v = v_ref[:, sl].astype(BF16)
        o = _softmax_pv([(_qkt(q_ref[:, sl], k), v)])
        o_ref[:, sl] = o.astype(BF16)


def _context_attention(q, k, v, seq_len):
    m, w = q.shape
    spec = pl.BlockSpec((seq_len, w), lambda b: (b, 0))
    return pl.pallas_call(
        _ctx_attn_kernel,
        grid=(m // seq_len,),
        in_specs=[spec, spec, spec],
        out_specs=spec,
        out_shape=jax.ShapeDtypeStruct((m, w), BF16),
        compiler_params=_cparams("parallel"),
        name="context_attention",
    )(q, k, v)


LAT_QROWS = 4
LAT_KROWS = NA_ROWS + LAT_QROWS
LAT_TQ = LAT_QROWS * GRID_W


def _lat_attn_kernel(q_ref, k0_ref, k1_ref, k2_ref, v0_ref, v1_ref, v2_ref, ck_ref, cv_ref, bias_ref, o_ref):
    for h in range(N_HEADS):
        sl = slice(h * HEAD_DIM, (h + 1) * HEAD_DIM)
        q = q_ref[:, sl]
        parts = []
        for i, (kr, vr) in enumerate(((k0_ref, v0_ref), (k1_ref, v1_ref), (k2_ref, v2_ref))):
            s = _qkt(q, kr[:, sl].astype(BF16)) + bias_ref[h, :, i * LAT_TQ:(i + 1) * LAT_TQ]
            parts.append((s, vr[:, sl].astype(BF16)))
        parts.append((_qkt(q, ck_ref[:, sl].astype(BF16)), cv_ref[:, sl].astype(BF16)))
        o_ref[:, sl] = _softmax_pv(parts).astype(BF16)


def _latent_bias(rpb, rows):
    nblk = rows // LAT_QROWS
    qi = np.arange(LAT_TQ)
    ki = np.arange(LAT_KROWS * GRID_W)
    qc, kc = qi % GRID_W, ki % GRID_W
    ws = np.clip(qc - NA_COLS // 2, 0, GRID_W - NA_COLS)
    col_ok = (kc[None, :] >= ws[:, None]) & (kc[None, :] < ws[:, None] + NA_COLS)
    dc = np.clip(kc[None, :] - qc[:, None] + NA_COLS - 1, 0, 2 * NA_COLS - 2)
    out = []
    for blk in (0, 1, nblk - 1):
        kstart = LAT_QROWS * int(np.clip(blk - 1, 0, nblk - 3))
        r = LAT_QROWS * blk + qi // GRID_W
        krow = kstart + ki // GRID_W
        rs = np.clip(r - NA_ROWS // 2, 0, rows - NA_ROWS)
        row_ok = (krow[None, :] >= rs[:, None]) & (krow[None, :] < rs[:, None] + NA_ROWS)
        dr = np.clip(krow[None, :] - r[:, None] + NA_ROWS - 1, 0, 2 * NA_ROWS - 2)
        b = rpb[:, dr, dc]
        out.append(jnp.where(jnp.asarray(row_ok & col_ok)[None], b, NEG_BIG))
    return jnp.stack(out, axis=0)


def _latent_attention(q, k, v, ctx_k, ctx_v, layer, bias, seq_len):
    m, w = q.shape
    rows = seq_len // GRID_W
    nblk = rows // LAT_QROWS
    nb = m // seq_len
    tq = LAT_TQ
    nctx = ctx_k.shape[2]

    def kspec(i):
        return pl.BlockSpec((tq, w), lambda r, b: (b * nblk + jnp.clip(r - 1, 0, nblk - 3) + i, 0))

    var = lambda r, b: (jnp.where(r == 0, 0, jnp.where(r == nblk - 1, 2, 1)), 0, 0, 0)
    qspec = pl.BlockSpec((tq, w), lambda r, b: (b * nblk + r, 0))
    cspec = pl.BlockSpec((None, None, nctx, w), lambda r, b: (b, layer, 0, 0))
    return pl.pallas_call(
        _lat_attn_kernel,
        grid=(nblk, nb),
        in_specs=[qspec, kspec(0), kspec(1), kspec(2), kspec(0), kspec(1), kspec(2), cspec, cspec,
                  pl.BlockSpec((None, N_HEADS, tq, LAT_KROWS * GRID_W), var)],
        out_specs=qspec,
        out_shape=jax.ShapeDtypeStruct((m, w), BF16),
        compiler_params=_cparams("parallel", "parallel"),
        name="latent_attention",
    )(q, k, k, k, v, v, v, ctx_k, ctx_v, bias)


def _merge_kernel(x_ref, h_ref, ya_ref, yb_ref, yc_ref, yd_ref, mod_ref, nw_ref,
                  wg_ref, bg_ref, wa_ref, wb_ref, wc_ref, wd_ref, wo_ref, wr_ref, br_ref,
                  x1_ref, h2_ref, lg_ref):
    d = x_ref.shape[1]
    h = h_ref[...]
    merged = None
    for i, (y_ref, w_ref) in enumerate(((ya_ref, wa_ref), (yb_ref, wb_ref), (yc_ref, wc_ref), (yd_ref, wd_ref))):
        gate = jax.nn.sigmoid(_dot(h, wg_ref[:, i * d:(i + 1) * d]) + bg_ref[:, i * d:(i + 1) * d])
        term = gate * _dot(y_ref[...], w_ref[...])
        merged = term if merged is None else merged + term
    mod = mod_ref[...]
    x1 = x_ref[...] + mod[2:3, :] * _dot(merged.astype(BF16), wo_ref[...])
    x1_ref[...] = x1
    h2 = _rms_mod(x1, nw_ref[...], mod[3:4, :], mod[4:5, :])
    h2_hi = h2.astype(BF16)
    h2_ref[...] = h2_hi
    h2_lo = (h2 - h2_hi.astype(F32)).astype(BF16)
    lg_ref[...] = (_dot(h2_hi, wr_ref[0]) + _dot(h2_lo, wr_ref[0]) + _dot(h2_hi, wr_ref[1])) + br_ref[...]


def _merge(x, h, ya, yb_t, yc, yd, mod, nw, wts, tm, rows_per_mod, seq_len):
    m, d = x.shape
    cw = ya.shape[1]
    nchunk = seq_len // tm
    row = lambda i: (i, 0)
    const2 = lambda i: (0, 0)
    w_specs = [_resident(w.shape) for w in wts]
    return pl.pallas_call(
        _merge_kernel,
        grid=(m // tm,),
        in_specs=[pl.BlockSpec((tm, d), row), pl.BlockSpec((tm, d), row),
                  pl.BlockSpec((tm, cw), row),
                  pl.BlockSpec((tm, cw), lambda i: (i % nchunk, i // nchunk)),
                  pl.BlockSpec((tm, cw), row),
                  pl.BlockSpec((tm, yd.shape[1]), row),
                  pl.BlockSpec((None, 8, d), lambda i: ((i * tm) // rows_per_mod, 0, 0)),
                  pl.BlockSpec((1, d), const2)] + w_specs,
        out_specs=[pl.BlockSpec((tm, d), row), pl.BlockSpec((tm, d), row),
                   pl.BlockSpec((tm, ROUTE_LANES), row)],
        out_shape=[jax.ShapeDtypeStruct((m, d), F32), jax.ShapeDtypeStruct((m, d), BF16),
                   jax.ShapeDtypeStruct((m, ROUTE_LANES), F32)],
        compiler_params=_cparams("parallel"),
        name="merge",
    )(x, h, ya, yb_t, yc, yd, mod, nw, *wts)


def _router_kernel(lg_ref, comb_ref):
    lg = lg_ref[...]
    lane = lax.broadcasted_iota(jnp.int32, lg.shape, 1).astype(F32)
    neg = jnp.float32(-jnp.inf)
    big = jnp.float32(ROUTE_LANES)
    gl = jnp.where(lane < MOE_GROUPS, lg, neg)
    gm = jnp.max(gl, axis=-1, keepdims=True)
    g_prob = 1.0 / jnp.sum(jnp.exp(gl - gm), axis=-1, keepdims=True)
    gidx = jnp.min(jnp.where(gl == gm, lane, big), axis=-1, keepdims=True)
    e0 = EXPERT_LANE0 + gidx * MOE_EXPERTS
    el = jnp.where((lane >= e0) & (lane < e0 + MOE_EXPERTS), lg, neg)
    m1 = jnp.max(el, axis=-1, keepdims=True)
    i1 = jnp.min(jnp.where(el == m1, lane, big), axis=-1, keepdims=True)
    el2 = jnp.where(lane == i1, neg, el)
    m2 = jnp.max(el2, axis=-1, keepdims=True)
    i2 = jnp.min(jnp.where(el2 == m2, lane, big), axis=-1, keepdims=True)
    r = jnp.exp(m2 - m1)
    w1 = 1.0 / (1.0 + r)
    w2 = r * w1
    comb_ref[...] = jnp.where(lane == i1, w1, jnp.where(lane == i2, w2, 0.0)) * g_prob


def _router(logits, tm):
    m = logits.shape[0]
    spec = pl.BlockSpec((tm, ROUTE_LANES), lambda i: (i, 0))
    return pl.pallas_call(
        _router_kernel, grid=(m // tm,), in_specs=[spec], out_specs=spec,
        out_shape=jax.ShapeDtypeStruct((m, ROUTE_LANES), F32),
        compiler_params=_cparams("parallel"), name="router",
    )(logits)


def _moe_kernel(h_ref, comb_ref, wgu_ref, wd_ref, x1_ref, mod_ref, o_ref, acc_ref):
    e = pl.program_id(1)

    @pl.when(e == 0)
    def _():
        acc_ref[...] = jnp.zeros_like(acc_ref)

    ab = _dot(h_ref[...], wgu_ref[...])
    f = ab.shape[1] // 2
    comb = comb_ref[...]
    lane = lax.broadcasted_iota(jnp.int32, comb.shape, 1)
    cw = jnp.sum(jnp.where(lane == e + EXPERT_LANE0, comb, 0.0), axis=-1, keepdims=True)
    hid = _silu(ab[:, :f]) * ab[:, f:] * cw
    acc_ref[...] += _dot(hid.astype(BF16), wd_ref[...])

    @pl.when(e == pl.num_programs(1) - 1)
    def _():
        o_ref[...] = x1_ref[...] + mod_ref[5:6, :] * acc_ref[...]


def _moe(h2, comb, wgu, wd, x1, mod, tm, rows_per_mod):
    m, d = x1.shape
    ne, _, f2 = wgu.shape
    row = lambda i, e: (i, 0)
    return pl.pallas_call(
        _moe_kernel,
        grid=(m // tm, ne),
        in_specs=[pl.BlockSpec((tm, d), row), pl.BlockSpec((tm, ROUTE_LANES), row),
                  pl.BlockSpec((None, d, f2), lambda i, e: (e, 0, 0)),
                  pl.BlockSpec((None, f2 // 2, d), lambda i, e: (e, 0, 0)),
                  pl.BlockSpec((tm, d), row),
                  pl.BlockSpec((None, 8, d), lambda i, e: ((i * tm) // rows_per_mod, 0, 0))],
        out_specs=pl.BlockSpec((tm, d), row),
        out_shape=jax.ShapeDtypeStruct((m, d), F32),
        scratch_shapes=[pltpu.VMEM((tm, d), F32)],
        compiler_params=_cparams("parallel", "arbitrary"),
        name="moe",
    )(h2, comb, wgu, wd, x1, mod)


def _final_norm_kernel(x_ref, nw_ref, o_ref):
    x = x_ref[...]
    o_ref[...] = x * lax.rsqrt(jnp.mean(x * x, axis=-1, keepdims=True) + EPS) * nw_ref[...]


def _final_norm(x, nw, tm):
    m, d = x.shape
    spec = pl.BlockSpec((tm, d), lambda i: (i, 0))
    return pl.pallas_call(
        _final_norm_kernel, grid=(m // tm,),
        in_specs=[spec, pl.BlockSpec((1, d), lambda i: (0, 0))], out_specs=spec,
        out_shape=jax.ShapeDtypeStruct((m, d), F32),
        compiler_params=_cparams("parallel"), name="final_norm",
    )(x, nw)


def _layer_weights(P, l):
    d = P['w_in'].shape[1]
    cw = P['pool_scale'].shape[1]
    hw = P['hyena_conv'].shape[2]
    o1, o2, o3 = cw, cw + hw, cw + hw + 3 * cw
    na = (P['w_in'].shape[2] - o3) // 3
    w_in = P['w_in'][l]
    segs = [w_in[:, o1:o2], w_in[:, o2:o3], w_in[:, o3:o3 + na], w_in[:, o3 + na:o3 + 2 * na],
            w_in[:, o3 + 2 * na:], w_in[:, :o1]]
    widths = tuple(int(s.shape[1]) for s in segs)
    w_in_r = jnp.concatenate(segs, axis=1).astype(BF16)
    gw = cw // len(POOL_WINDOWS)
    pool_bd = jnp.zeros((cw, cw), F32)
    for g in range(len(POOL_WINDOWS)):
        pool_bd = pool_bd.at[g * gw:(g + 1) * gw, g * gw:(g + 1) * gw].set(P['pool_w'][l, g])
    wr = jnp.concatenate([P['w_route_group'][l],
                          jnp.transpose(P['w_route_exp'][l], (1, 0, 2)).reshape(d, N_EXPERTS)], axis=1)
    wr = jnp.zeros((d, ROUTE_LANES), F32).at[:, :wr.shape[1]].set(wr)
    wr_hi = wr.astype(BF16)
    wr_lo = (wr - wr_hi.astype(F32)).astype(BF16)
    br = jnp.concatenate([P['b_route_group'][l], P['b_route_exp'][l].reshape(-1)])
    br = jnp.zeros((1, ROUTE_LANES), F32).at[0, :br.shape[0]].set(br)
    f = P['w_e_gate'].shape[-1]
    wgu = jnp.concatenate([P['w_e_gate'][l], P['w_e_up'][l]], axis=-1).reshape(N_EXPERTS, d, 2 * f)
    return dict(
        widths=widths, w_in_r=w_in_r, norm_mix=P['norm_mix'][l][None], norm_ffn=P['norm_ffn'][l][None],
        pool_bd=pool_bd.astype(BF16), pool_scale=P['pool_scale'][l][None], sconv_w=P['sconv_w'][l],
        hyena_conv=P['hyena_conv'][l], hyena_bias=P['hyena_bias'][l],
        merge=(P['w_gate'][l].astype(BF16), P['b_gate'][l][None], P['w_br_a'][l].astype(BF16),
               P['w_br_b'][l].astype(BF16), P['w_br_c'][l].astype(BF16), P['w_br_d'][l].astype(BF16),
               P['w_out'][l].astype(BF16), jnp.stack([wr_hi, wr_lo]), br),
        wgu=wgu.astype(BF16), wd=P['w_e_down'][l].reshape(N_EXPERTS, f, d).astype(BF16),
    )


def _run_stream(x3, mods, LW, hy, attend, depth):
    b, seq_len, d = x3.shape
    m = b * seq_len
    x = x3.reshape(m, d)
    rows_per_mod = m // mods.shape[1]
    tm = min(512, seq_len)
    tc = min(256, seq_len)
    kvs = []
    for l in range(depth):
        W = LW[l]
        mod = mods[l]
        cw = W['pool_scale'].shape[1]
        h, ph, ps, q, k, v, pp = _inproj(x, mod, W['norm_mix'], W['w_in_r'], W['widths'], tm, rows_per_mod)
        kvs.append((k, v))
        ya, yc, u_t, x0_t = _local_mixers(pp, ps, ph, W['pool_bd'], W['pool_scale'], W['sconv_w'],
                                          W['hyena_conv'], seq_len, tc)
        yb_t = _hyena_conv(u_t, x0_t, hy['fwd_bf'], hy['inv_c'], hy['inv_s'], hy['spec'][l],
                           W['hyena_bias'], cw)
        yd = attend(l, q, k, v)
        x1, h2, logits = _merge(x, h, ya, yb_t, yc, yd, mod, W['norm_ffn'], W['merge'], tm,
                                rows_per_mod, seq_len)
        comb = _router(logits, tm)
        x = _moe(h2, comb, W['wgu'], W['wd'], x1, mod, min(1024, m), rows_per_mod)
    return x, kvs


def _hyena_setup(seq_len, P, depth):
    fwd32, inv_c, inv_s = _dft_matrices(seq_len)
    spec = []
    for l in range(depth):
        filt = _hyena_filter(seq_len, P['hyena_f1'][l], P['hyena_fb1'][l], P['hyena_f2'][l],
                             P['hyena_fb2'][l], P['hyena_f3'][l], P['hyena_freq'][l], P['hyena_decay'][l])
        spec.append(_filter_spectrum(fwd32, filt))
    return dict(fwd_bf=fwd32.astype(BF16), inv_c=inv_c, inv_s=inv_s, spec=spec)


def kernel(x_prompt, x_sample, cache_k, cache_v, c, c_ctx, w_ada, b_ada, norm_mix, w_in, w_gate, b_gate, pool_w, pool_scale, hyena_conv, hyena_f1, hyena_fb1, hyena_f2, hyena_fb2, hyena_f3, hyena_freq, hyena_decay, hyena_bias, sconv_w, na_rpb, w_br_a, w_br_b, w_br_c, w_br_d, w_out, norm_ffn, w_route_group, b_route_group, w_route_exp, b_route_exp, w_e_gate, w_e_up, w_e_down, norm_final):
    P = dict(w_in=w_in, w_gate=w_gate, b_gate=b_gate, pool_w=pool_w, pool_scale=pool_scale,
             hyena_conv=hyena_conv, hyena_f1=hyena_f1, hyena_fb1=hyena_fb1, hyena_f2=hyena_f2,
             hyena_fb2=hyena_fb2, hyena_f3=hyena_f3, hyena_freq=hyena_freq, hyena_decay=hyena_decay,
             hyena_bias=hyena_bias, sconv_w=sconv_w, w_br_a=w_br_a, w_br_b=w_br_b, w_br_c=w_br_c,
             w_br_d=w_br_d, w_out=w_out, norm_mix=norm_mix, norm_ffn=norm_ffn,
             w_route_group=w_route_group, b_route_group=b_route_group, w_route_exp=w_route_exp,
             b_route_exp=b_route_exp, w_e_gate=w_e_gate, w_e_up=w_e_up, w_e_down=w_e_down)
    depth, d, _ = w_ada.shape
    bp, lp, _ = x_prompt.shape
    bs, ls, _ = x_sample.shape
    assert (ls // GRID_W) % LAT_QROWS == 0 and ls // GRID_W >= LAT_KROWS

    n_c = 1 + bs
    n_pad = -(-n_c // 8) * 8
    cvecs = jnp.zeros((n_pad, d), F32).at[0].set(c_ctx).at[1:n_c].set(c)
    ada = _ada(cvecs, w_ada, b_ada).reshape(depth, n_pad, 6, d)
    ada = jnp.concatenate([ada, jnp.zeros((depth, n_pad, 2, d), F32)], axis=2)

    LW = [_layer_weights(P, l) for l in range(depth)]
    nw_final = norm_final[None]

    hy_p = _hyena_setup(lp, P, depth)
    xp, kv_p = _run_stream(x_prompt, ada[:, 0:1], LW, hy_p,
                           lambda l, q, k, v: _context_attention(q, k, v, lp), depth)
    y_prompt = _final_norm(xp, nw_final, min(512, lp)).reshape(bp, lp, d)
    new_k = jnp.stack([k.reshape(bp, lp, N_HEADS, HEAD_DIM) for k, _ in kv_p], axis=1)
    new_v = jnp.stack([v.reshape(bp, lp, N_HEADS, HEAD_DIM) for _, v in kv_p], axis=1)

    hy_s = _hyena_setup(ls, P, depth)
    past = cache_k.shape[2]
    ck = cache_k.reshape(bs, depth, past, N_HEADS * HEAD_DIM)
    cv = cache_v.reshape(bs, depth, past, N_HEADS * HEAD_DIM)
    biases = [_latent_bias(na_rpb[l], ls // GRID_W) for l in range(depth)]
    xs, _ = _run_stream(x_sample, ada[:, 1:n_c], LW, hy_s,
                        lambda l, q, k, v: _latent_attention(q, k, v, ck, cv, l, biases[l], ls),
                        depth)
    y_sample = _final_norm(xs, nw_final, min(512, ls)).reshape(bs, ls, d)
    return (y_prompt, y_sample, new_k, new_v)
```

```python
import functools
import math

import numpy as np
import jax
import jax.numpy as jnp
from jax import lax
from jax.experimental import pallas as pl
from jax.experimental.pallas import tpu as pltpu

F32 = jnp.float32
BF16 = jnp.bfloat16
HIGHEST = lax.Precision.HIGHEST

EPS = 1e-6
GRID_W = 64
NA_ROWS = 8
NA_COLS = 16
N_HEADS = 8
HEAD_DIM = 64
POOL_WINDOWS = (2, 4, 8, 16)
HYENA_BANDS = 16
MOE_GROUPS = 4
MOE_EXPERTS = 8
N_EXPERTS = MOE_GROUPS * MOE_EXPERTS
ROUTE_LANES = 128
EXPERT_LANE0 = MOE_GROUPS
HALO = 8
NEG_BIG = -1e30
VMEM_LIMIT_BYTES = 48 * 1024 * 1024


def _cparams(*sem):
    return pltpu.CompilerParams(dimension_semantics=sem, vmem_limit_bytes=VMEM_LIMIT_BYTES)


def _resident(shape):
    nd = len(shape)
    return pl.BlockSpec(shape, lambda *_: (0,) * nd, pipeline_mode=pl.Buffered(1))


def _dot(a, b):
    return jnp.dot(a, b, preferred_element_type=F32)


def _silu(x):
    return x * jax.nn.sigmoid(x)


def _ada_kernel(cv_ref, w_ref, b_ref, o_ref):
    o_ref[...] = jnp.dot(_silu(cv_ref[...]), w_ref[...], precision=HIGHEST,
                         preferred_element_type=F32) + b_ref[...]


def _ada(cvecs, w_ada, b_ada):
    depth, d, d6 = w_ada.shape
    r = cvecs.shape[0]
    return pl.pallas_call(
        _ada_kernel,
        grid=(depth, d6 // d),
        in_specs=[pl.BlockSpec((r, d), lambda l, j: (0, 0)),
                  pl.BlockSpec((None, d, d), lambda l, j: (l, 0, j)),
                  pl.BlockSpec((None, 1, d), lambda l, j: (l, 0, j))],
        out_specs=pl.BlockSpec((None, r, d), lambda l, j: (l, 0, j)),
        out_shape=jax.ShapeDtypeStruct((depth, r, d6), F32),
        compiler_params=_cparams("parallel", "parallel"),
        name="ada",
    )(cvecs, w_ada, b_ada.reshape(depth, 1, d6))


def _rms_mod(x, nw, shift, scale):
    y = x * lax.rsqrt(jnp.mean(x * x, axis=-1, keepdims=True) + EPS) * nw
    return y * (1.0 + scale) + shift


def _inproj_kernel(x_ref, mod_ref, nw_ref, w_ref, h_ref, ph_ref, ps_ref, q_ref, k_ref, v_ref, pp_ref,
                   *, widths):
    mod = mod_ref[...]
    h = _rms_mod(x_ref[...], nw_ref[...], mod[0:1, :], mod[1:2, :]).astype(BF16)
    h_ref[...] = h
    off = 0
    for ref, wd in zip((ph_ref, ps_ref, q_ref, k_ref, v_ref, pp_ref), widths):
        r = _dot(h, w_ref[:, off:off + wd])
        if ref is q_ref:
            r = r * (HEAD_DIM ** -0.5)
        ref[...] = r.astype(ref.dtype)
        off += wd


def _inproj(x, mod, nw, w_in_r, widths, tm, rows_per_mod):
    m, d = x.shape
    n = w_in_r.shape[1]
    dts = (F32, F32, BF16, F32, F32, F32)
    row = lambda i: (i, 0)
    return pl.pallas_call(
        functools.partial(_inproj_kernel, widths=widths),
        grid=(m // tm,),
        in_specs=[pl.BlockSpec((tm, d), row),
                  pl.BlockSpec((None, 8, d), lambda i: ((i * tm) // rows_per_mod, 0, 0)),
                  pl.BlockSpec((1, d), lambda i: (0, 0)),
                  _resident((d, n))],
        out_specs=[pl.BlockSpec((tm, d), row)] + [pl.BlockSpec((tm, wd), row) for wd in widths],
        out_shape=[jax.ShapeDtypeStruct((m, d), BF16)]
        + [jax.ShapeDtypeStruct((m, wd), dt) for wd, dt in zip(widths, dts)],
        compiler_params=_cparams("parallel"),
        name="inproj",
    )(x, mod, nw, w_in_r)


def _fill_padded(pad_ref, prev_ref, cur_ref, next_ref, first, last, tc):
    zero = jnp.zeros((HALO, cur_ref.shape[1]), F32)
    pad_ref[0:HALO, :] = jnp.where(first, zero, prev_ref[...])
    pad_ref[HALO:HALO + tc, :] = cur_ref[...]
    pad_ref[HALO + tc:2 * HALO + tc, :] = jnp.where(last, zero, next_ref[...])


def _local_kernel(pp_ref, pp_prev, pp_next, ps_ref, ps_prev, ps_next, ph_ref, ph_prev, ph_next,
                  pw_ref, pscale_ref, sw_ref, hw_ref,
                  ya_ref, yc_ref, u_ref, x0_ref,
                  pad_p, pad_s, pad_h, *, seq_len, tc):
    nchunk = seq_len // tc
    j = pl.program_id(0) % nchunk
    first = j == 0
    last = j == nchunk - 1
    cw = pp_ref.shape[1]
    sw = cw

    _fill_padded(pad_p, pp_prev, pp_ref, pp_next, first, last, tc)
    sh = lambda k: pad_p[HALO + k:HALO + k + tc, :]
    u = pp_ref[...]
    sums = {}
    acc = u
    lo_done, hi_done = 0, 0
    for win in POOL_WINDOWS:
        lo, hi = -(win // 2), win // 2 - 1
        for k in range(lo, lo_done):
            acc = acc + sh(k)
        for k in range(hi_done + 1, hi + 1):
            acc = acc + sh(k)
        lo_done, hi_done = lo, hi
        sums[win] = acc
    t = j * tc + lax.broadcasted_iota(jnp.int32, (tc, 1), 0)
    lane = lax.broadcasted_iota(jnp.int32, (1, cw), 1)
    gw = cw // len(POOL_WINDOWS)
    pooled = None
    for g, win in reversed(list(enumerate(POOL_WINDOWS))):
        cnt = jnp.minimum(t - win // 2 + win, seq_len) - jnp.maximum(t - win // 2, 0)
        val = sums[win] * (1.0 / cnt.astype(F32))
        pooled = val if pooled is None else jnp.where(lane < (g + 1) * gw, val, pooled)
    pooled = pooled - u
    ya = _dot(pooled.astype(BF16), pw_ref[...]) * pscale_ref[...]
    ya_ref[...] = ya.astype(BF16)

    _fill_padded(pad_s, ps_prev, ps_ref, ps_next, first, last, tc)
    w3 = sw_ref[...]
    z = lambda k: (pad_s[HALO + k:HALO + k + tc, 2 * sw:3 * sw] * pad_s[HALO + k:HALO + k + tc, 0:sw])
    conv = w3[0:1, :] * z(-1) + w3[1:2, :] * z(0) + w3[2:3, :] * z(1)
    yc_ref[...] = (ps_ref[:, sw:2 * sw] * conv).astype(BF16)

    _fill_padded(pad_h, ph_prev, ph_ref, ph_next, first, last, tc)
    hw = hw_ref[...]
    c3 = (hw[0:1, :] * pad_h[HALO - 1:HALO - 1 + tc, :] + hw[1:2, :] * pad_h[HALO:HALO + tc, :]
          + hw[2:3, :] * pad_h[HALO + 1:HALO + 1 + tc, :])
    x0_ref[...] = c3[:, 0:sw]
    u_ref[...] = c3[:, sw:2 * sw] * c3[:, 2 * sw:3 * sw]


def _local_mixers(pp, ps, ph, pool_bd, pool_scale, sconv_w, hyena_conv, seq_len, tc):
    m, cw = pp.shape
    nb = m // seq_len
    nchunk = seq_len // tc
    hpc = tc // HALO
    nhb = m // HALO
    row = lambda i: (i, 0)
    prev = lambda i: (jnp.maximum(i * hpc - 1, 0), 0)
    nxt = lambda i: (jnp.minimum((i + 1) * hpc, nhb - 1), 0)
    tl = lambda i: (i % nchunk, i // nchunk)
    const = lambda i: (0, 0)

    def trio(width):
        return [pl.BlockSpec((tc, width), row), pl.BlockSpec((HALO, width), prev),
                pl.BlockSpec((HALO, width), nxt)]

    return pl.pallas_call(
        functools.partial(_local_kernel, seq_len=seq_len, tc=tc),
        grid=(m // tc,),
        in_specs=trio(cw) + trio(3 * cw) + trio(3 * cw)
        + [pl.BlockSpec((cw, cw), const), pl.BlockSpec((1, cw), const),
           pl.BlockSpec((3, cw), const), pl.BlockSpec((3, 3 * cw), const)],
        out_specs=[pl.BlockSpec((tc, cw), row), pl.BlockSpec((tc, cw), row),
                   pl.BlockSpec((tc, cw), tl), pl.BlockSpec((tc, cw), tl)],
        out_shape=[jax.ShapeDtypeStruct((m, cw), BF16), jax.ShapeDtypeStruct((m, cw), BF16),
                   jax.ShapeDtypeStruct((seq_len, nb * cw), F32),
                   jax.ShapeDtypeStruct((seq_len, nb * cw), F32)],
        scratch_shapes=[pltpu.VMEM((tc + 2 * HALO, cw), F32), pltpu.VMEM((tc + 2 * HALO, 3 * cw), F32),
                        pltpu.VMEM((tc + 2 * HALO, 3 * cw), F32)],
        compiler_params=_cparams("parallel"),
        name="local_mixers",
    )(pp, pp, pp, ps, ps, ps, ph, ph, ph, pool_bd, pool_scale, sconv_w, hyena_conv)


def _trig(rows, cols, n):
    split = 64
    r = np.asarray(rows, np.int64)[:, None]
    c = np.asarray(cols, np.int64)
    assert c[0] % split == 0 and len(c) % split == 0 and np.all(np.diff(c) == 1)
    c0 = np.arange(split)[None, :]
    c1 = c[::split][None, :]
    ang0 = ((r * c0) % n) * (2.0 * math.pi / n)
    ang1 = ((r * c1) % n) * (2.0 * math.pi / n)
    tab = lambda a: jnp.asarray(a, F32)
    ca, sa = tab(np.cos(ang0))[:, None, :], tab(np.sin(ang0))[:, None, :]
    cb, sb = tab(np.cos(ang1))[:, :, None], tab(np.sin(ang1))[:, :, None]
    shape = (len(rows), len(c))
    return (ca * cb - sa * sb).reshape(shape), (sa * cb + ca * sb).reshape(shape)


def _dft_matrices(seq_len):
    L = seq_len
    n = 2 * L
    k = np.arange(L)
    cos_f, sin_f = _trig(k, k, n)
    alt = jnp.asarray(np.where(k % 2 == 0, 1.0, -1.0), F32)
    im = (-sin_f).at[0, :].set(alt)
    fwd = jnp.concatenate([cos_f, im], axis=0)
    tp = np.arange(L // 2, L // 2 + L)
    cos_i, sin_i = _trig(tp, k, n)
    alt_t = jnp.asarray(np.where(tp % 2 == 0, 1.0, -1.0), F32)
    inv_c = (cos_i * (2.0 / n)).at[:, 0].set(1.0 / n)
    inv_s = (sin_i * (-2.0 / n)).at[:, 0].set(alt_t / n)
    return fwd, inv_c.astype(BF16), inv_s.astype(BF16)


def _hyena_embedding(seq_len, width):
    t = np.arange(seq_len, dtype=np.float64)
    w = (2.0 * math.pi / seq_len) * t
    bands = np.linspace(1e-4, HYENA_BANDS - 1, HYENA_BANDS)
    z = np.concatenate([(t / (seq_len - 1))[:, None], np.cos(w[:, None] * bands),
                        -np.sin(w[:, None] * bands)], axis=-1)
    out = np.zeros((seq_len, width), np.float32)
    out[:, :z.shape[1]] = z
    return jnp.asarray(out)


def _filter_kernel(z_ref, f1_ref, fb1_ref, f2_ref, fb2_ref, f3_ref, freq_ref, decay_ref, o_ref, *, seq_len):
    hdot = lambda a, b: jnp.dot(a, b, precision=HIGHEST, preferred_element_type=F32)
    fr = freq_ref[...]
    hdn = jnp.sin(fr * (hdot(z_ref[...], f1_ref[...]) + fb1_ref[...]))
    hdn = jnp.sin(fr * (hdot(hdn, f2_ref[...]) + fb2_ref[...]))
    filt = hdot(hdn, f3_ref[...])
    t = lax.broadcasted_iota(jnp.int32, (seq_len, 1), 0)
    dist = jnp.abs(t - seq_len // 2).astype(F32) / (seq_len / 2)
    filt = filt * jnp.exp(-dist * decay_ref[...])
    o_ref[...] = filt / jnp.sum(jnp.abs(filt), axis=0, keepdims=True)


def _pad2(a, rows, cols):
    return jnp.zeros((rows, cols), F32).at[:a.shape[0], :a.shape[1]].set(a)


def _hyena_filter(seq_len, f1, fb1, f2, fb2, f3, freq, decay):
    p = 128
    c = f3.shape[1]
    args = (_hyena_embedding(seq_len, p), _pad2(f1, p, p), _pad2(fb1[None], 1, p), _pad2(f2, p, p),
            _pad2(fb2[None], 1, p), _pad2(f3, p, c), _pad2(freq[None], 1, p), decay[None])
    return pl.pallas_call(
        functools.partial(_filter_kernel, seq_len=seq_len),
        out_shape=jax.ShapeDtypeStruct((seq_len, c), F32),
        compiler_params=pltpu.CompilerParams(vmem_limit_bytes=VMEM_LIMIT_BYTES),
        name="hyena_filter",
    )(*args)


def _spectrum_kernel(f_ref, h_ref, o_ref):
    o_ref[...] = jnp.dot(f_ref[...], h_ref[...], precision=HIGHEST, preferred_element_type=F32)


def _filter_spectrum(fwd32, filt):
    n, L = fwd32.shape
    c = filt.shape[1]
    tf = min(n, 512)
    return pl.pallas_call(
        _spectrum_kernel,
        grid=(n // tf,),
        in_specs=[pl.BlockSpec((tf, L), lambda i: (i, 0)), pl.BlockSpec((L, c), lambda i: (0, 0))],
        out_specs=pl.BlockSpec((tf, c), lambda i: (i, 0)),
        out_shape=jax.ShapeDtypeStruct((n, c), F32),
        compiler_params=_cparams("parallel"),
        name="filter_spectrum",
    )(fwd32, filt)


def _hy_fwd_kernel(fc_ref, fs_ref, u_ref, hr_ref, hi_ref, yr_ref, yi_ref, *, cw):
    ub = u_ref[...].astype(BF16)
    ur = _dot(fc_ref[...], ub)
    ui = _dot(fs_ref[...], ub)
    tf = ur.shape[0]
    row0 = (pl.program_id(0) * tf + lax.broadcasted_iota(jnp.int32, (tf, 1), 0)) == 0
    hr, hi = hr_ref[...], hi_ref[...]
    for s in range(ur.shape[1] // cw):
        a, b = ur[:, s * cw:(s + 1) * cw], ui[:, s * cw:(s + 1) * cw]
        yr = jnp.where(row0, a * hr, a * hr - b * hi)
        yi = jnp.where(row0, b * hi, a * hi + b * hr)
        yr_ref[:, s * cw:(s + 1) * cw] = yr.astype(BF16)
        yi_ref[:, s * cw:(s + 1) * cw] = yi.astype(BF16)


def _hy_inv_kernel(ic_ref, is_ref, yr_ref, yi_ref, u_ref, x0_ref, bias_ref, o_ref):
    conv = _dot(ic_ref[...], yr_ref[...]) + _dot(is_ref[...], yi_ref[...])
    o_ref[...] = ((conv + bias_ref[...] * u_ref[...]) * x0_ref[...]).astype(BF16)


def _hyena_conv(u_t, x0_t, fwd_bf, inv_c, inv_s, spec, bias, cw):
    L, ncol = u_t.shape
    tf = min(L, 1024)
    tn = min(ncol, 512)
    nf = L // tf
    yr, yi = pl.pallas_call(
        functools.partial(_hy_fwd_kernel, cw=cw),
        grid=(nf, ncol // tn),
        in_specs=[pl.BlockSpec((tf, L), lambda i, j: (i, 0)),
                  pl.BlockSpec((tf, L), lambda i, j: (nf + i, 0)),
                  pl.BlockSpec((L, tn), lambda i, j: (0, j)),
                  pl.BlockSpec((tf, cw), lambda i, j: (i, 0)),
                  pl.BlockSpec((tf, cw), lambda i, j: (nf + i, 0))],
        out_specs=[pl.BlockSpec((tf, tn), lambda i, j: (i, j))] * 2,
        out_shape=[jax.ShapeDtypeStruct((L, ncol), BF16)] * 2,
        compiler_params=_cparams("parallel", "parallel"),
        name="hyena_dft",
    )(fwd_bf, fwd_bf, u_t, spec, spec)
    bias_t = jnp.tile(bias[None, :], (1, tn // cw))
    return pl.pallas_call(
        _hy_inv_kernel,
        grid=(nf, ncol // tn),
        in_specs=[pl.BlockSpec((tf, L), lambda i, j: (i, 0)),
                  pl.BlockSpec((tf, L), lambda i, j: (i, 0)),
                  pl.BlockSpec((L, tn), lambda i, j: (0, j)),
                  pl.BlockSpec((L, tn), lambda i, j: (0, j)),
                  pl.BlockSpec((tf, tn), lambda i, j: (i, j)),
                  pl.BlockSpec((tf, tn), lambda i, j: (i, j)),
                  pl.BlockSpec((1, tn), lambda i, j: (0, 0))],
        out_specs=pl.BlockSpec((tf, tn), lambda i, j: (i, j)),
        out_shape=jax.ShapeDtypeStruct((L, ncol), BF16),
        compiler_params=_cparams("parallel", "parallel"),
        name="hyena_idft",
    )(inv_c, inv_s, yr, yi, u_t, x0_t, bias_t)


def _softmax_pv(parts):
    m = None
    for s, _ in parts:
        mi = jnp.max(s, axis=-1, keepdims=True)
        m = mi if m is None else jnp.maximum(m, mi)
    den, out = None, None
    for s, v in parts:
        p = jnp.exp(s - m)
        d = jnp.sum(p, axis=-1, keepdims=True)
        o = _dot(p.astype(BF16), v)
        den = d if den is None else den + d
        out = o if out is None else out + o
    return out * (1.0 / den)


def _qkt(q, k):
    return lax.dot_general(q, k, (((1,), (1,)), ((), ())), preferred_element_type=F32)


def _ctx_attn_kernel(q_ref, k_ref, v_ref, o_ref):
    for h in range(N_HEADS):
        sl = slice(h * HEAD_DIM, (h + 1) * HEAD_DIM)
        k = k_ref[:, sl].astype(BF16)
        v = v_ref[:, sl].astype(BF16)
        o = _softmax_pv([(_qkt(q_ref[:, sl], k), v)])
        o_ref[:, sl] = o.astype(BF16)


def _context_attention(q, k, v, seq_len):
    m, w = q.shape
    spec = pl.BlockSpec((seq_len, w), lambda b: (b, 0))
    return pl.pallas_call(
        _ctx_attn_kernel,
        grid=(m // seq_len,),
        in_specs=[spec, spec, spec],
        out_specs=spec,
        out_shape=jax.ShapeDtypeStruct((m, w), BF16),
        compiler_params=_cparams("parallel"),
        name="context_attention",
    )(q, k, v)


LAT_QROWS = 4
LAT_KROWS = NA_ROWS + LAT_QROWS
LAT_TQ = LAT_QROWS * GRID_W


def _lat_attn_kernel(q_ref, k0_ref, k1_ref, k2_ref, v0_ref, v1_ref, v2_ref, ck_ref, cv_ref, bias_ref, o_ref):
    for h in range(N_HEADS):
        sl = slice(h * HEAD_DIM, (h + 1) * HEAD_DIM)
        q = q_ref[:, sl]
        parts = []
        for i, (kr, vr) in enumerate(((k0_ref, v0_ref), (k1_ref, v1_ref), (k2_ref, v2_ref))):
            s = _qkt(q, kr[:, sl].astype(BF16)) + bias_ref[h, :, i * LAT_TQ:(i + 1) * LAT_TQ]
            parts.append((s, vr[:, sl].astype(BF16)))
        parts.append((_qkt(q, ck_ref[:, sl].astype(BF16)), cv_ref[:, sl].astype(BF16)))
        o_ref[:, sl] = _softmax_pv(parts).astype(BF16)


def _latent_bias(rpb, rows):
    nblk = rows // LAT_QROWS
    nh, ndr, ndc = rpb.shape
    qi = np.arange(LAT_TQ)
    ki = np.arange(LAT_KROWS * GRID_W)
    qc, kc = qi % GRID_W, ki % GRID_W
    ws = np.clip(qc - NA_COLS // 2, 0, GRID_W - NA_COLS)
    col_ok = (kc[None, :] >= ws[:, None]) & (kc[None, :] < ws[:, None] + NA_COLS)
    edge = GRID_W - NA_COLS
    p = jnp.pad(rpb, ((0, 0), (0, 0), (edge, edge)), mode='edge')
    p = jnp.pad(p, ((0, 0), (0, 0), (0, 1)))
    sel, masks = [], []
    for blk in (0, 1, nblk - 1):
        kstart = LAT_QROWS * int(np.clip(blk - 1, 0, nblk - 3))
        r = LAT_QROWS * blk + np.arange(LAT_QROWS)
        krow = kstart + np.arange(LAT_KROWS)
        sel.append(np.clip(krow[None, :] - r[:, None] + NA_ROWS - 1, 0, ndr - 1).reshape(-1))
        rq = LAT_QROWS * blk + qi // GRID_W
        rk = kstart + ki // GRID_W
        rs = np.clip(rq - NA_ROWS // 2, 0, rows - NA_ROWS)
        masks.append((rk[None, :] >= rs[:, None]) & (rk[None, :] < rs[:, None] + NA_ROWS) & col_ok)
    onehot = jnp.asarray(np.concatenate(sel)[:, None] == np.arange(ndr)[None, :], F32)
    rows_sel = jnp.einsum('sd,hdc->hsc', onehot, p, precision=HIGHEST)
    s = rows_sel.shape[1]
    w2 = 2 * GRID_W
    t = jnp.broadcast_to(rows_sel[:, :, None, :], (nh, s, GRID_W, w2)).reshape(nh, s, GRID_W * w2)
    t = t[:, :, :GRID_W * (w2 - 1)].reshape(nh, s, GRID_W, w2 - 1)[..., GRID_W - 1:]
    t = t.reshape(nh, 3, LAT_QROWS, LAT_KROWS, GRID_W, GRID_W).transpose(1, 0, 2, 4, 3, 5)
    t = t.reshape(3, nh, LAT_TQ, LAT_KROWS * GRID_W)
    return jnp.where(jnp.asarray(np.stack(masks))[:, None], t, NEG_BIG)


def _latent_attention(q, k, v, ctx_k, ctx_v, layer, bias, seq_len):
    m, w = q.shape
    rows = seq_len // GRID_W
    nblk = rows // LAT_QROWS
    nb = m // seq_len
    tq = LAT_TQ
    nctx = ctx_k.shape[2]

    def kspec(i):
        return pl.BlockSpec((tq, w), lambda r, b: (b * nblk + jnp.clip(r - 1, 0, nblk - 3) + i, 0))

    var = lambda r, b: (jnp.where(r == 0, 0, jnp.where(r == nblk - 1, 2, 1)), 0, 0, 0)
    qspec = pl.BlockSpec((tq, w), lambda r, b: (b * nblk + r, 0))
    cspec = pl.BlockSpec((None, None, nctx, w), lambda r, b: (b, layer, 0, 0))
    return pl.pallas_call(
        _lat_attn_kernel,
        grid=(nblk, nb),
        in_specs=[qspec, kspec(0), kspec(1), kspec(2), kspec(0), kspec(1), kspec(2), cspec, cspec,
                  pl.BlockSpec((None, N_HEADS, tq, LAT_KROWS * GRID_W), var)],
        out_specs=qspec,
        out_shape=jax.ShapeDtypeStruct((m, w), BF16),
        compiler_params=_cparams("parallel", "parallel"),
        name="latent_attention",
    )(q, k, k, k, v, v, v, ctx_k, ctx_v, bias)


def _merge_kernel(x_ref, h_ref, ya_ref, yb_ref, yc_ref, yd_ref, mod_ref, nw_ref,
                  wg_ref, bg_ref, wa_ref, wb_ref, wc_ref, wd_ref, wo_ref, wr_ref, br_ref,
                  x1_ref, h2_ref, lg_ref):
    d = x_ref.shape[1]
    h = h_ref[...]
    merged = None
    for i, (y_ref, w_ref) in enumerate(((ya_ref, wa_ref), (yb_ref, wb_ref), (yc_ref, wc_ref), (yd_ref, wd_ref))):
        gate = jax.nn.sigmoid(_dot(h, wg_ref[:, i * d:(i + 1) * d]) + bg_ref[:, i * d:(i + 1) * d])
        term = gate * _dot(y_ref[...], w_ref[...])
        merged = term if merged is None else merged + term
    mod = mod_ref[...]
    x1 = x_ref[...] + mod[2:3, :] * _dot(merged.astype(BF16), wo_ref[...])
    x1_ref[...] = x1
    h2 = _rms_mod(x1, nw_ref[...], mod[3:4, :], mod[4:5, :])
    h2_hi = h2.astype(BF16)
    h2_ref[...] = h2_hi
    h2_lo = (h2 - h2_hi.astype(F32)).astype(BF16)
    lg_ref[...] = (_dot(h2_hi, wr_ref[0]) + _dot(h2_lo, wr_ref[0]) + _dot(h2_hi, wr_ref[1])) + br_ref[...]


def _merge(x, h, ya, yb_t, yc, yd, mod, nw, wts, tm, rows_per_mod, seq_len):
    m, d = x.shape
    cw = ya.shape[1]
    nchunk = seq_len // tm
    row = lambda i: (i, 0)
    const2 = lambda i: (0, 0)
    w_specs = [_resident(w.shape) for w in wts]
    return pl.pallas_call(
        _merge_kernel,
        grid=(m // tm,),
        in_specs=[pl.BlockSpec((tm, d), row), pl.BlockSpec((tm, d), row),
                  pl.BlockSpec((tm, cw), row),
                  pl.BlockSpec((tm, cw), lambda i: (i % nchunk, i // nchunk)),
                  pl.BlockSpec((tm, cw), row),
                  pl.BlockSpec((tm, yd.shape[1]), row),
                  pl.BlockSpec((None, 8, d), lambda i: ((i * tm) // rows_per_mod, 0, 0)),
                  pl.BlockSpec((1, d), const2)] + w_specs,
        out_specs=[pl.BlockSpec((tm, d), row), pl.BlockSpec((tm, d), row),
                   pl.BlockSpec((tm, ROUTE_LANES), row)],
        out_shape=[jax.ShapeDtypeStruct((m, d), F32), jax.ShapeDtypeStruct((m, d), BF16),
                   jax.ShapeDtypeStruct((m, ROUTE_LANES), F32)],
        compiler_params=_cparams("parallel"),
        name="merge",
    )(x, h, ya, yb_t, yc, yd, mod, nw, *wts)


def _router_kernel(lg_ref, comb_ref):
    lg = lg_ref[...]
    lane = lax.broadcasted_iota(jnp.int32, lg.shape, 1).astype(F32)
    neg = jnp.float32(-jnp.inf)
    big = jnp.float32(ROUTE_LANES)
    gl = jnp.where(lane < MOE_GROUPS, lg, neg)
    gm = jnp.max(gl, axis=-1, keepdims=True)
    g_prob = 1.0 / jnp.sum(jnp.exp(gl - gm), axis=-1, keepdims=True)
    gidx = jnp.min(jnp.where(gl == gm, lane, big), axis=-1, keepdims=True)
    e0 = EXPERT_LANE0 + gidx * MOE_EXPERTS
    el = jnp.where((lane >= e0) & (lane < e0 + MOE_EXPERTS), lg, neg)
    m1 = jnp.max(el, axis=-1, keepdims=True)
    i1 = jnp.min(jnp.where(el == m1, lane, big), axis=-1, keepdims=True)
    el2 = jnp.where(lane == i1, neg, el)
    m2 = jnp.max(el2, axis=-1, keepdims=True)
    i2 = jnp.min(jnp.where(el2 == m2, lane, big), axis=-1, keepdims=True)
    r = jnp.exp(m2 - m1)
    w1 = 1.0 / (1.0 + r)
    w2 = r * w1
    comb_ref[...] = jnp.where(lane == i1, w1, jnp.where(lane == i2, w2, 0.0)) * g_prob


def _router(logits, tm):
    m = logits.shape[0]
    spec = pl.BlockSpec((tm, ROUTE_LANES), lambda i: (i, 0))
    return pl.pallas_call(
        _router_kernel, grid=(m // tm,), in_specs=[spec], out_specs=spec,
        out_shape=jax.ShapeDtypeStruct((m, ROUTE_LANES), F32),
        compiler_params=_cparams("parallel"), name="router",
    )(logits)


def _moe_kernel(h_ref, comb_ref, wgu_ref, wd_ref, x1_ref, mod_ref, o_ref, acc_ref):
    e = pl.program_id(1)

    @pl.when(e == 0)
    def _():
        acc_ref[...] = jnp.zeros_like(acc_ref)

    ab = _dot(h_ref[...], wgu_ref[...])
    f = ab.shape[1] // 2
    comb = comb_ref[...]
    lane = lax.broadcasted_iota(jnp.int32, comb.shape, 1)
    cw = jnp.sum(jnp.where(lane == e + EXPERT_LANE0, comb, 0.0), axis=-1, keepdims=True)
    hid = _silu(ab[:, :f]) * ab[:, f:] * cw
    acc_ref[...] += _dot(hid.astype(BF16), wd_ref[...])

    @pl.when(e == pl.num_programs(1) - 1)
    def _():
        o_ref[...] = x1_ref[...] + mod_ref[5:6, :] * acc_ref[...]


def _moe(h2, comb, wgu, wd, x1, mod, tm, rows_per_mod):
    m, d = x1.shape
    ne, _, f2 = wgu.shape
    row = lambda i, e: (i, 0)
    return pl.pallas_call(
        _moe_kernel,
        grid=(m // tm, ne),
        in_specs=[pl.BlockSpec((tm, d), row), pl.BlockSpec((tm, ROUTE_LANES), row),
                  pl.BlockSpec((None, d, f2), lambda i, e: (e, 0, 0)),
                  pl.BlockSpec((None, f2 // 2, d), lambda i, e: (e, 0, 0)),
                  pl.BlockSpec((tm, d), row),
                  pl.BlockSpec((None, 8, d), lambda i, e: ((i * tm) // rows_per_mod, 0, 0))],
        out_specs=pl.BlockSpec((tm, d), row),
        out_shape=jax.ShapeDtypeStruct((m, d), F32),
        scratch_shapes=[pltpu.VMEM((tm, d), F32)],
        compiler_params=_cparams("parallel", "arbitrary"),
        name="moe",
    )(h2, comb, wgu, wd, x1, mod)


def _final_norm_kernel(x_ref, nw_ref, o_ref):
    x = x_ref[...]
    o_ref[...] = x * lax.rsqrt(jnp.mean(x * x, axis=-1, keepdims=True) + EPS) * nw_ref[...]


def _final_norm(x, nw, tm):
    m, d = x.shape
    spec = pl.BlockSpec((tm, d), lambda i: (i, 0))
    return pl.pallas_call(
        _final_norm_kernel, grid=(m // tm,),
        in_specs=[spec, pl.BlockSpec((1, d), lambda i: (0, 0))], out_specs=spec,
        out_shape=jax.ShapeDtypeStruct((m, d), F32),
        compiler_params=_cparams("parallel"), name="final_norm",
    )(x, nw)


def _layer_weights(P, l):
    d = P['w_in'].shape[1]
    cw = P['pool_scale'].shape[1]
    hw = P['hyena_conv'].shape[2]
    o1, o2, o3 = cw, cw + hw, cw + hw + 3 * cw
    na = (P['w_in'].shape[2] - o3) // 3
    w_in = P['w_in'][l]
    segs = [w_in[:, o1:o2], w_in[:, o2:o3], w_in[:, o3:o3 + na], w_in[:, o3 + na:o3 + 2 * na],
            w_in[:, o3 + 2 * na:], w_in[:, :o1]]
    widths = tuple(int(s.shape[1]) for s in segs)
    w_in_r = jnp.concatenate(segs, axis=1).astype(BF16)
    gw = cw // len(POOL_WINDOWS)
    pool_bd = jnp.zeros((cw, cw), F32)
    for g in range(len(POOL_WINDOWS)):
        pool_bd = pool_bd.at[g * gw:(g + 1) * gw, g * gw:(g + 1) * gw].set(P['pool_w'][l, g])
    wr = jnp.concatenate([P['w_route_group'][l],
                          jnp.transpose(P['w_route_exp'][l], (1, 0, 2)).reshape(d, N_EXPERTS)], axis=1)
    wr = jnp.zeros((d, ROUTE_LANES), F32).at[:, :wr.shape[1]].set(wr)
    wr_hi = wr.astype(BF16)
    wr_lo = (wr - wr_hi.astype(F32)).astype(BF16)
    br = jnp.concatenate([P['b_route_group'][l], P['b_route_exp'][l].reshape(-1)])
    br = jnp.zeros((1, ROUTE_LANES), F32).at[0, :br.shape[0]].set(br)
    f = P['w_e_gate'].shape[-1]
    wgu = jnp.concatenate([P['w_e_gate'][l], P['w_e_up'][l]], axis=-1).reshape(N_EXPERTS, d, 2 * f)
    return dict(
        widths=widths, w_in_r=w_in_r, norm_mix=P['norm_mix'][l][None], norm_ffn=P['norm_ffn'][l][None],
        pool_bd=pool_bd.astype(BF16), pool_scale=P['pool_scale'][l][None], sconv_w=P['sconv_w'][l],
        hyena_conv=P['hyena_conv'][l], hyena_bias=P['hyena_bias'][l],
        merge=(P['w_gate'][l].astype(BF16), P['b_gate'][l][None], P['w_br_a'][l].astype(BF16),
               P['w_br_b'][l].astype(BF16), P['w_br_c'][l].astype(BF16), P['w_br_d'][l].astype(BF16),
               P['w_out'][l].astype(BF16), jnp.stack([wr_hi, wr_lo]), br),
        wgu=wgu.astype(BF16), wd=P['w_e_down'][l].reshape(N_EXPERTS, f, d).astype(BF16),
    )


def _run_stream(x3, mods, LW, hy, attend, depth):
    b, seq_len, d = x3.shape
    m = b * seq_len
    x = x3.reshape(m, d)
    rows_per_mod = m // mods.shape[1]
    tm = min(512, seq_len)
    tc = min(256, seq_len)
    kvs = []
    for l in range(depth):
        W = LW[l]
        mod = mods[l]
        cw = W['pool_scale'].shape[1]
        h, ph, ps, q, k, v, pp = _inproj(x, mod, W['norm_mix'], W['w_in_r'], W['widths'], tm, rows_per_mod)
        kvs.append((k, v))
        ya, yc, u_t, x0_t = _local_mixers(pp, ps, ph, W['pool_bd'], W['pool_scale'], W['sconv_w'],
                                          W['hyena_conv'], seq_len, tc)
        yb_t = _hyena_conv(u_t, x0_t, hy['fwd_bf'], hy['inv_c'], hy['inv_s'], hy['spec'][l],
                           W['hyena_bias'], cw)
        yd = attend(l, q, k, v)
        x1, h2, logits = _merge(x, h, ya, yb_t, yc, yd, mod, W['norm_ffn'], W['merge'], tm,
                                rows_per_mod, seq_len)
        comb = _router(logits, tm)
        x = _moe(h2, comb, W['wgu'], W['wd'], x1, mod, min(1024, m), rows_per_mod)
    return x, kvs


def _hyena_setup(seq_len, P, depth):
    fwd32, inv_c, inv_s = _dft_matrices(seq_len)
    spec = []
    for l in range(depth):
        filt = _hyena_filter(seq_len, P['hyena_f1'][l], P['hyena_fb1'][l], P['hyena_f2'][l],
                             P['hyena_fb2'][l], P['hyena_f3'][l], P['hyena_freq'][l], P['hyena_decay'][l])
        spec.append(_filter_spectrum(fwd32, filt))
    return dict(fwd_bf=fwd32.astype(BF16), inv_c=inv_c, inv_s=inv_s, spec=spec)


def kernel(x_prompt, x_sample, cache_k, cache_v, c, c_ctx, w_ada, b_ada, norm_mix, w_in, w_gate, b_gate, pool_w, pool_scale, hyena_conv, hyena_f1, hyena_fb1, hyena_f2, hyena_fb2, hyena_f3, hyena_freq, hyena_decay, hyena_bias, sconv_w, na_rpb, w_br_a, w_br_b, w_br_c, w_br_d, w_out, norm_ffn, w_route_group, b_route_group, w_route_exp, b_route_exp, w_e_gate, w_e_up, w_e_down, norm_final):
    P = dict(w_in=w_in, w_gate=w_gate, b_gate=b_gate, pool_w=pool_w, pool_scale=pool_scale,
             hyena_conv=hyena_conv, hyena_f1=hyena_f1, hyena_fb1=hyena_fb1, hyena_f2=hyena_f2,
             hyena_fb2=hyena_fb2, hyena_f3=hyena_f3, hyena_freq=hyena_freq, hyena_decay=hyena_decay,
             hyena_bias=hyena_bias, sconv_w=sconv_w, w_br_a=w_br_a, w_br_b=w_br_b, w_br_c=w_br_c,
             w_br_d=w_br_d, w_out=w_out, norm_mix=norm_mix, norm_ffn=norm_ffn,
             w_route_group=w_route_group, b_route_group=b_route_group, w_route_exp=w_route_exp,
             b_route_exp=b_route_exp, w_e_gate=w_e_gate, w_e_up=w_e_up, w_e_down=w_e_down)
    depth, d, _ = w_ada.shape
    bp, lp, _ = x_prompt.shape
    bs, ls, _ = x_sample.shape
    assert (ls // GRID_W) % LAT_QROWS == 0 and ls // GRID_W >= LAT_KROWS

    n_c = 1 + bs
    n_pad = -(-n_c // 8) * 8
    cvecs = jnp.zeros((n_pad, d), F32).at[0].set(c_ctx).at[1:n_c].set(c)
    ada = _ada(cvecs, w_ada, b_ada).reshape(depth, n_pad, 6, d)
    ada = jnp.concatenate([ada, jnp.zeros((depth, n_pad, 2, d), F32)], axis=2)

    LW = [_layer_weights(P, l) for l in range(depth)]
    nw_final = norm_final[None]

    hy_p = _hyena_setup(lp, P, depth)
    xp, kv_p = _run_stream(x_prompt, ada[:, 0:1], LW, hy_p,
                           lambda l, q, k, v: _context_attention(q, k, v, lp), depth)
    y_prompt = _final_norm(xp, nw_final, min(512, lp)).reshape(bp, lp, d)
    new_k = jnp.stack([k.reshape(bp, lp, N_HEADS, HEAD_DIM) for k, _ in kv_p], axis=1)
    new_v = jnp.stack([v.reshape(bp, lp, N_HEADS, HEAD_DIM) for _, v in kv_p], axis=1)

    hy_s = _hyena_setup(ls, P, depth)
    past = cache_k.shape[2]
    ck = cache_k.reshape(bs, depth, past, N_HEADS * HEAD_DIM)
    cv = cache_v.reshape(bs, depth, past, N_HEADS * HEAD_DIM)
    biases = [_latent_bias(na_rpb[l], ls // GRID_W) for l in range(depth)]
    xs, _ = _run_stream(x_sample, ada[:, 1:n_c], LW, hy_s,
                        lambda l, q, k, v: _latent_attention(q, k, v, ck, cv, l, biases[l], ls),
                        depth)
    y_sample = _final_norm(xs, nw_final, min(512, ls)).reshape(bs, ls, d)
    return (y_prompt, y_sample, new_k, new_v)
```

```python
import functools
import math

import numpy as np
import jax
import jax.numpy as jnp
from jax import lax
from jax.experimental import pallas as pl
from jax.experimental.pallas import tpu as pltpu

F32 = jnp.float32
BF16 = jnp.bfloat16
HIGHEST = lax.Precision.HIGHEST

EPS = 1e-6
GRID_W = 64
NA_ROWS = 8
NA_COLS = 16
N_HEADS = 8
HEAD_DIM = 64
POOL_WINDOWS = (2, 4, 8, 16)
HYENA_BANDS = 16
MOE_GROUPS = 4
MOE_EXPERTS = 8
N_EXPERTS = MOE_GROUPS * MOE_EXPERTS
ROUTE_LANES = 128
EXPERT_LANE0 = MOE_GROUPS
HALO = 8
NEG_BIG = -1e30
VMEM_LIMIT_BYTES = 48 * 1024 * 1024


def _cparams(*sem):
    return pltpu.CompilerParams(dimension_semantics=sem, vmem_limit_bytes=VMEM_LIMIT_BYTES)


def _resident(shape):
    nd = len(shape)
    return pl.BlockSpec(shape, lambda *_: (0,) * nd, pipeline_mode=pl.Buffered(1))


def _dot(a, b):
    return jnp.dot(a, b, preferred_element_type=F32)


def _silu(x):
    return x * jax.nn.sigmoid(x)


def _ada_kernel(cv_ref, w_ref, b_ref, o_ref):
    o_ref[...] = jnp.dot(_silu(cv_ref[...]), w_ref[...], precision=HIGHEST,
                         preferred_element_type=F32) + b_ref[...]


def _ada(cvecs, w_ada, b_ada):
    depth, d, d6 = w_ada.shape
    r = cvecs.shape[0]
    return pl.pallas_call(
        _ada_kernel,
        grid=(depth, d6 // d),
        in_specs=[pl.BlockSpec((r, d), lambda l, j: (0, 0)),
                  pl.BlockSpec((None, d, d), lambda l, j: (l, 0, j)),
                  pl.BlockSpec((None, 1, d), lambda l, j: (l, 0, j))],
        out_specs=pl.BlockSpec((None, r, d), lambda l, j: (l, 0, j)),
        out_shape=jax.ShapeDtypeStruct((depth, r, d6), F32),
        compiler_params=_cparams("parallel", "parallel"),
        name="ada",
    )(cvecs, w_ada, b_ada.reshape(depth, 1, d6))


def _rms_mod(x, nw, shift, scale):
    y = x * lax.rsqrt(jnp.mean(x * x, axis=-1, keepdims=True) + EPS) * nw
    return y * (1.0 + scale) + shift


def _inproj_kernel(x_ref, mod_ref, nw_ref, w_ref, h_ref, ph_ref, ps_ref, q_ref, k_ref, v_ref, pp_ref,
                   *, widths):
    mod = mod_ref[...]
    h = _rms_mod(x_ref[...], nw_ref[...], mod[0:1, :], mod[1:2, :]).astype(BF16)
    h_ref[...] = h
    off = 0
    for ref, wd in zip((ph_ref, ps_ref, q_ref, k_ref, v_ref, pp_ref), widths):
        r = _dot(h, w_ref[:, off:off + wd])
        if ref is q_ref:
            r = r * (HEAD_DIM ** -0.5)
        ref[...] = r.astype(ref.dtype)
        off += wd


def _inproj(x, mod, nw, w_in_r, widths, tm, rows_per_mod):
    m, d = x.shape
    n = w_in_r.shape[1]
    dts = (F32, F32, BF16, F32, F32, F32)
    row = lambda i: (i, 0)
    return pl.pallas_call(
        functools.partial(_inproj_kernel, widths=widths),
        grid=(m // tm,),
        in_specs=[pl.BlockSpec((tm, d), row),
                  pl.BlockSpec((None, 8, d), lambda i: ((i * tm) // rows_per_mod, 0, 0)),
                  pl.BlockSpec((1, d), lambda i: (0, 0)),
                  _resident((d, n))],
        out_specs=[pl.BlockSpec((tm, d), row)] + [pl.BlockSpec((tm, wd), row) for wd in widths],
        out_shape=[jax.ShapeDtypeStruct((m, d), BF16)]
        + [jax.ShapeDtypeStruct((m, wd), dt) for wd, dt in zip(widths, dts)],
        compiler_params=_cparams("parallel"),
        name="inproj",
    )(x, mod, nw, w_in_r)


def _fill_padded(pad_ref, prev_ref, cur_ref, next_ref, first, last, tc):
    zero = jnp.zeros((HALO, cur_ref.shape[1]), F32)
    pad_ref[0:HALO, :] = jnp.where(first, zero, prev_ref[...])
    pad_ref[HALO:HALO + tc, :] = cur_ref[...]
    pad_ref[HALO + tc:2 * HALO + tc, :] = jnp.where(last, zero, next_ref[...])


def _local_kernel(pp_ref, pp_prev, pp_next, ps_ref, ps_prev, ps_next, ph_ref, ph_prev, ph_next,
                  pw_ref, pscale_ref, sw_ref, hw_ref,
                  ya_ref, yc_ref, u_ref, x0_ref,
                  pad_p, pad_s, pad_h, *, seq_len, tc):
    nchunk = seq_len // tc
    j = pl.program_id(0) % nchunk
    first = j == 0
    last = j == nchunk - 1
    cw = pp_ref.shape[1]
    sw = cw

    _fill_padded(pad_p, pp_prev, pp_ref, pp_next, first, last, tc)
    sh = lambda k: pad_p[HALO + k:HALO + k + tc, :]
    u = pp_ref[...]
    sums = {}
    acc = u
    lo_done, hi_done = 0, 0
    for win in POOL_WINDOWS:
        lo, hi = -(win // 2), win // 2 - 1
        for k in range(lo, lo_done):
            acc = acc + sh(k)
        for k in range(hi_done + 1, hi + 1):
            acc = acc + sh(k)
        lo_done, hi_done = lo, hi
        sums[win] = acc
    t = j * tc + lax.broadcasted_iota(jnp.int32, (tc, 1), 0)
    lane = lax.broadcasted_iota(jnp.int32, (1, cw), 1)
    gw = cw // len(POOL_WINDOWS)
    pooled = None
    for g, win in reversed(list(enumerate(POOL_WINDOWS))):
        cnt = jnp.minimum(t - win // 2 + win, seq_len) - jnp.maximum(t - win // 2, 0)
        val = sums[win] * (1.0 / cnt.astype(F32))
        pooled = val if pooled is None else jnp.where(lane < (g + 1) * gw, val, pooled)
    pooled = pooled - u
    ya = _dot(pooled.astype(BF16), pw_ref[...]) * pscale_ref[...]
    ya_ref[...] = ya.astype(BF16)

    _fill_padded(pad_s, ps_prev, ps_ref, ps_next, first, last, tc)
    w3 = sw_ref[...]
    z = lambda k: (pad_s[HALO + k:HALO + k + tc, 2 * sw:3 * sw] * pad_s[HALO + k:HALO + k + tc, 0:sw])
    conv = w3[0:1, :] * z(-1) + w3[1:2, :] * z(0) + w3[2:3, :] * z(1)
    yc_ref[...] = (ps_ref[:, sw:2 * sw] * conv).astype(BF16)

    _fill_padded(pad_h, ph_prev, ph_ref, ph_next, first, last, tc)
    hw = hw_ref[...]
    c3 = (hw[0:1, :] * pad_h[HALO - 1:HALO - 1 + tc, :] + hw[1:2, :] * pad_h[HALO:HALO + tc, :]
          + hw[2:3, :] * pad_h[HALO + 1:HALO + 1 + tc, :])
    x0_ref[...] = c3[:, 0:sw]
    u_ref[...] = c3[:, sw:2 * sw] * c3[:, 2 * sw:3 * sw]


def _local_mixers(pp, ps, ph, pool_bd, pool_scale, sconv_w, hyena_conv, seq_len, tc):
    m, cw = pp.shape
    nb = m // seq_len
    nchunk = seq_len // tc
    hpc = tc // HALO
    nhb = m // HALO
    row = lambda i: (i, 0)
    prev = lambda i: (jnp.maximum(i * hpc - 1, 0), 0)
    nxt = lambda i: (jnp.minimum((i + 1) * hpc, nhb - 1), 0)
    tl = lambda i: (i % nchunk, i // nchunk)
    const = lambda i: (0, 0)

    def trio(width):
        return [pl.BlockSpec((tc, width), row), pl.BlockSpec((HALO, width), prev),
                pl.BlockSpec((HALO, width), nxt)]

    return pl.pallas_call(
        functools.partial(_local_kernel, seq_len=seq_len, tc=tc),
        grid=(m // tc,),
        in_specs=trio(cw) + trio(3 * cw) + trio(3 * cw)
        + [pl.BlockSpec((cw, cw), const), pl.BlockSpec((1, cw), const),
           pl.BlockSpec((3, cw), const), pl.BlockSpec((3, 3 * cw), const)],
        out_specs=[pl.BlockSpec((tc, cw), row), pl.BlockSpec((tc, cw), row),
                   pl.BlockSpec((tc, cw), tl), pl.BlockSpec((tc, cw), tl)],
        out_shape=[jax.ShapeDtypeStruct((m, cw), BF16), jax.ShapeDtypeStruct((m, cw), BF16),
                   jax.ShapeDtypeStruct((seq_len, nb * cw), F32),
                   jax.ShapeDtypeStruct((seq_len, nb * cw), F32)],
        scratch_shapes=[pltpu.VMEM((tc + 2 * HALO, cw), F32), pltpu.VMEM((tc + 2 * HALO, 3 * cw), F32),
                        pltpu.VMEM((tc + 2 * HALO, 3 * cw), F32)],
        compiler_params=_cparams("parallel"),
        name="local_mixers",
    )(pp, pp, pp, ps, ps, ps, ph, ph, ph, pool_bd, pool_scale, sconv_w, hyena_conv)


def _trig(rows, cols, n):
    split = 64
    r = np.asarray(rows, np.int64)[:, None]
    c = np.asarray(cols, np.int64)
    assert c[0] % split == 0 and len(c) % split == 0 and np.all(np.diff(c) == 1)
    c0 = np.arange(split)[None, :]
    c1 = c[::split][None, :]
    ang0 = ((r * c0) % n) * (2.0 * math.pi / n)
    ang1 = ((r * c1) % n) * (2.0 * math.pi / n)
    tab = lambda a: jnp.asarray(a, F32)
    ca, sa = tab(np.cos(ang0))[:, None, :], tab(np.sin(ang0))[:, None, :]
    cb, sb = tab(np.cos(ang1))[:, :, None], tab(np.sin(ang1))[:, :, None]
    shape = (len(rows), len(c))
    return (ca * cb - sa * sb).reshape(shape), (sa * cb + ca * sb).reshape(shape)


def _dft_matrices(seq_len):
    L = seq_len
    n = 2 * L
    k = np.arange(L)
    cos_f, sin_f = _trig(k, k, n)
    alt = jnp.asarray(np.where(k % 2 == 0, 1.0, -1.0), F32)
    im = (-sin_f).at[0, :].set(alt)
    fwd = jnp.concatenate([cos_f, im], axis=0)
    tp = np.arange(L // 2, L // 2 + L)
    cos_i, sin_i = _trig(tp, k, n)
    alt_t = jnp.asarray(np.where(tp % 2 == 0, 1.0, -1.0), F32)
    inv_c = (cos_i * (2.0 / n)).at[:, 0].set(1.0 / n)
    inv_s = (sin_i * (-2.0 / n)).at[:, 0].set(alt_t / n)
    return fwd, inv_c.astype(BF16), inv_s.astype(BF16)


def _hyena_embedding(seq_len, width):
    t = np.arange(seq_len, dtype=np.float64)
    w = (2.0 * math.pi / seq_len) * t
    bands = np.linspace(1e-4, HYENA_BANDS - 1, HYENA_BANDS)
    z = np.concatenate([(t / (seq_len - 1))[:, None], np.cos(w[:, None] * bands),
                        -np.sin(w[:, None] * bands)], axis=-1)
    out = np.zeros((seq_len, width), np.float32)
    out[:, :z.shape[1]] = z
    return jnp.asarray(out)


def _filter_kernel(z_ref, f1_ref, fb1_ref, f2_ref, fb2_ref, f3_ref, freq_ref, decay_ref, o_ref, *, seq_len):
    hdot = lambda a, b: jnp.dot(a, b, precision=HIGHEST, preferred_element_type=F32)
    fr = freq_ref[...]
    hdn = jnp.sin(fr * (hdot(z_ref[...], f1_ref[...]) + fb1_ref[...]))
    hdn = jnp.sin(fr * (hdot(hdn, f2_ref[...]) + fb2_ref[...]))
    filt = hdot(hdn, f3_ref[...])
    t = lax.broadcasted_iota(jnp.int32, (seq_len, 1), 0)
    dist = jnp.abs(t - seq_len // 2).astype(F32) / (seq_len / 2)
    filt = filt * jnp.exp(-dist * decay_ref[...])
    o_ref[...] = filt / jnp.sum(jnp.abs(filt), axis=0, keepdims=True)


def _pad2(a, rows, cols):
    return jnp.zeros((rows, cols), F32).at[:a.shape[0], :a.shape[1]].set(a)


def _hyena_filter(seq_len, f1, fb1, f2, fb2, f3, freq, decay):
    p = 128
    c = f3.shape[1]
    args = (_hyena_embedding(seq_len, p), _pad2(f1, p, p), _pad2(fb1[None], 1, p), _pad2(f2, p, p),
            _pad2(fb2[None], 1, p), _pad2(f3, p, c), _pad2(freq[None], 1, p), decay[None])
    return pl.pallas_call(
        functools.partial(_filter_kernel, seq_len=seq_len),
        out_shape=jax.ShapeDtypeStruct((seq_len, c), F32),
        compiler_params=pltpu.CompilerParams(vmem_limit_bytes=VMEM_LIMIT_BYTES),
        name="hyena_filter",
    )(*args)


def _spectrum_kernel(f_ref, h_ref, o_ref):
    o_ref[...] = jnp.dot(f_ref[...], h_ref[...], precision=HIGHEST, preferred_element_type=F32)


def _filter_spectrum(fwd32, filt):
    n, L = fwd32.shape
    c = filt.shape[1]
    tf = min(n, 512)
    return pl.pallas_call(
        _spectrum_kernel,
        grid=(n // tf,),
        in_specs=[pl.BlockSpec((tf, L), lambda i: (i, 0)), pl.BlockSpec((L, c), lambda i: (0, 0))],
        out_specs=pl.BlockSpec((tf, c), lambda i: (i, 0)),
        out_shape=jax.ShapeDtypeStruct((n, c), F32),
        compiler_params=_cparams("parallel"),
        name="filter_spectrum",
    )(fwd32, filt)


def _hy_fwd_kernel(fc_ref, fs_ref, u_ref, hr_ref, hi_ref, yr_ref, yi_ref, *, cw):
    ub = u_ref[...].astype(BF16)
    ur = _dot(fc_ref[...], ub)
    ui = _dot(fs_ref[...], ub)
    tf = ur.shape[0]
    row0 = (pl.program_id(0) * tf + lax.broadcasted_iota(jnp.int32, (tf, 1), 0)) == 0
    hr, hi = hr_ref[...], hi_ref[...]
    for s in range(ur.shape[1] // cw):
        a, b = ur[:, s * cw:(s + 1) * cw], ui[:, s * cw:(s + 1) * cw]
        yr = jnp.where(row0, a * hr, a * hr - b * hi)
        yi = jnp.where(row0, b * hi, a * hi + b * hr)
        yr_ref[:, s * cw:(s + 1) * cw] = yr.astype(BF16)
        yi_ref[:, s * cw:(s + 1) * cw] = yi.astype(BF16)


def _hy_inv_kernel(ic_ref, is_ref, yr_ref, yi_ref, u_ref, x0_ref, bias_ref, o_ref):
    conv = _dot(ic_ref[...], yr_ref[...]) + _dot(is_ref[...], yi_ref[...])
    o_ref[...] = ((conv + bias_ref[...] * u_ref[...]) * x0_ref[...]).astype(BF16)


def _hyena_conv(u_t, x0_t, fwd_bf, inv_c, inv_s, spec, bias, cw):
    L, ncol = u_t.shape
    tf = min(L, 1024)
    tn = min(ncol, 512)
    nf = L // tf
    yr, yi = pl.pallas_call(
        functools.partial(_hy_fwd_kernel, cw=cw),
        grid=(nf, ncol // tn),
        in_specs=[pl.BlockSpec((tf, L), lambda i, j: (i, 0)),
                  pl.BlockSpec((tf, L), lambda i, j: (nf + i, 0)),
                  pl.BlockSpec((L, tn), lambda i, j: (0, j)),
                  pl.BlockSpec((tf, cw), lambda i, j: (i, 0)),
                  pl.BlockSpec((tf, cw), lambda i, j: (nf + i, 0))],
        out_specs=[pl.BlockSpec((tf, tn), lambda i, j: (i, j))] * 2,
        out_shape=[jax.ShapeDtypeStruct((L, ncol), BF16)] * 2,
        compiler_params=_cparams("parallel", "parallel"),
        name="hyena_dft",
    )(fwd_bf, fwd_bf, u_t, spec, spec)
    bias_t = jnp.tile(bias[None, :], (1, tn // cw))
    return pl.pallas_call(
        _hy_inv_kernel,
        grid=(nf, ncol // tn),
        in_specs=[pl.BlockSpec((tf, L), lambda i, j: (i, 0)),
                  pl.BlockSpec((tf, L), lambda i, j: (i, 0)),
                  pl.BlockSpec((L, tn), lambda i, j: (0, j)),
                  pl.BlockSpec((L, tn), lambda i, j: (0, j)),
                  pl.BlockSpec((tf, tn), lambda i, j: (i, j)),
                  pl.BlockSpec((tf, tn), lambda i, j: (i, j)),
                  pl.BlockSpec((1, tn), lambda i, j: (0, 0))],
        out_specs=pl.BlockSpec((tf, tn), lambda i, j: (i, j)),
        out_shape=jax.ShapeDtypeStruct((L, ncol), BF16),
        compiler_params=_cparams("parallel", "parallel"),
        name="hyena_idft",
    )(inv_c, inv_s, yr, yi, u_t, x0_t, bias_t)


def _softmax_pv(parts):
    m = None
    for s, _ in parts:
        mi = jnp.max(s, axis=-1, keepdims=True)
        m = mi if m is None else jnp.maximum(m, mi)
    den, out = None, None
    for s, v in parts:
        p = jnp.exp(s - m)
        d = jnp.sum(p, axis=-1, keepdims=True)
        o = _dot(p.astype(BF16), v)
        den = d if den is None else den + d
        out = o if out is None else out + o
    return out * (1.0 / den)


def _qkt(q, k):
    return lax.dot_general(q, k, (((1,), (1,)), ((), ())), preferred_element_type=F32)


def _ctx_attn_kernel(q_ref, k_ref, v_ref, o_ref):
    for h in range(N_HEADS):
        sl = slice(h * HEAD_DIM, (h + 1) * HEAD_DIM)
        k = k_ref[:, sl].astype(BF16)
        v = v_ref[:, sl].astype(BF16)
        o = _softmax_pv([(_qkt(q_ref[:, sl], k), v)])
        o_ref[:, sl] = o.astype(BF16)


def _context_attention(q, k, v, seq_len):
    m, w = q.shape
    spec = pl.BlockSpec((seq_len, w), lambda b: (b, 0))
    return pl.pallas_call(
        _ctx_attn_kernel,
        grid=(m // seq_len,),
        in_specs=[spec, spec, spec],
        out_specs=spec,
        out_shape=jax.ShapeDtypeStruct((m, w), BF16),
        compiler_params=_cparams("parallel"),
        name="context_attention",
    )(q, k, v)


LAT_QROWS = 4
LAT_KROWS = NA_ROWS + LAT_QROWS
LAT_TQ = LAT_QROWS * GRID_W


def _lat_attn_kernel(q_ref, k0_ref, k1_ref, k2_ref, v0_ref, v1_ref, v2_ref, ck_ref, cv_ref, bias_ref, o_ref):
    for h in range(N_HEADS):
        sl = slice(h * HEAD_DIM, (h + 1) * HEAD_DIM)
        q = q_ref[:, sl]
        parts = []
        for i, (kr, vr) in enumerate(((k0_ref, v0_ref), (k1_ref, v1_ref), (k2_ref, v2_ref))):
            s = _qkt(q, kr[:, sl].astype(BF16)) + bias_ref[h, :, i * LAT_TQ:(i + 1) * LAT_TQ]
            parts.append((s, vr[:, sl].astype(BF16)))
        parts.append((_qkt(q, ck_ref[:, sl].astype(BF16)), cv_ref[:, sl].astype(BF16)))
        o_ref[:, sl] = _softmax_pv(parts).astype(BF16)


def _latent_bias(rpb, rows):
    nblk = rows // LAT_QROWS
    nh, ndr, ndc = rpb.shape
    qi = np.arange(LAT_TQ)
    ki = np.arange(LAT_KROWS * GRID_W)
    qc, kc = qi % GRID_W, ki % GRID_W
    ws = np.clip(qc - NA_COLS // 2, 0, GRID_W - NA_COLS)
    col_ok = (kc[None, :] >= ws[:, None]) & (kc[None, :] < ws[:, None] + NA_COLS)
    edge = GRID_W - NA_COLS
    p = jnp.pad(rpb, ((0, 0), (0, 0), (edge, edge)), mode='edge')
    p = jnp.pad(p, ((0, 0), (0, 0), (0, 1)))
    sel, masks = [], []
    for blk in (0, 1, nblk - 1):
        kstart = LAT_QROWS * int(np.clip(blk - 1, 0, nblk - 3))
        r = LAT_QROWS * blk + np.arange(LAT_QROWS)
        krow = kstart + np.arange(LAT_KROWS)
        sel.append(np.clip(krow[None, :] - r[:, None] + NA_ROWS - 1, 0, ndr - 1).reshape(-1))
        rq = LAT_QROWS * blk + qi // GRID_W
        rk = kstart + ki // GRID_W
        rs = np.clip(rq - NA_ROWS // 2, 0, rows - NA_ROWS)
        masks.append((rk[None, :] >= rs[:, None]) & (rk[None, :] < rs[:, None] + NA_ROWS) & col_ok)
    onehot = jnp.asarray(np.concatenate(sel)[:, None] == np.arange(ndr)[None, :], F32)
    rows_sel = jnp.einsum('sd,hdc->hsc', onehot, p, precision=HIGHEST)
    s = rows_sel.shape[1]
    w2 = 2 * GRID_W
    t = jnp.broadcast_to(rows_sel[:, :, None, :], (nh, s, GRID_W, w2)).reshape(nh, s, GRID_W * w2)
    t = t[:, :, :GRID_W * (w2 - 1)].reshape(nh, s, GRID_W, w2 - 1)[..., GRID_W - 1:]
    t = t.reshape(nh, 3, LAT_QROWS, LAT_KROWS, GRID_W, GRID_W).transpose(1, 0, 2, 4, 3, 5)
    t = t.reshape(3, nh, LAT_TQ, LAT_KROWS * GRID_W)
    return jnp.where(jnp.asarray(np.stack(masks))[:, None], t, NEG_BIG)


def _latent_attention(q, k, v, ctx_k, ctx_v, layer, bias, seq_len):
    m, w = q.shape
    rows = seq_len // GRID_W
    nblk = rows // LAT_QROWS
    nb = m // seq_len
    tq = LAT_TQ
    nctx = ctx_k.shape[2]

    def kspec(i):
        return pl.BlockSpec((tq, w), lambda r, b: (b * nblk + jnp.clip(r - 1, 0, nblk - 3) + i, 0))

    var = lambda r, b: (jnp.where(r == 0, 0, jnp.where(r == nblk - 1, 2, 1)), 0, 0, 0)
    qspec = pl.BlockSpec((tq, w), lambda r, b: (b * nblk + r, 0))
    cspec = pl.BlockSpec((None, None, nctx, w), lambda r, b: (b, layer, 0, 0))
    return pl.pallas_call(
        _lat_attn_kernel,
        grid=(nblk, nb),
        in_specs=[qspec, kspec(0), kspec(1), kspec(2), kspec(0), kspec(1), kspec(2), cspec, cspec,
                  pl.BlockSpec((None, N_HEADS, tq, LAT_KROWS * GRID_W), var)],
        out_specs=qspec,
        out_shape=jax.ShapeDtypeStruct((m, w), BF16),
        compiler_params=_cparams("parallel", "parallel"),
        name="latent_attention",
    )(q, k, k, k, v, v, v, ctx_k, ctx_v, bias)


def _merge_kernel(x_ref, h_ref, ya_ref, yb_ref, yc_ref, yd_ref, mod_ref, nw_ref,
                  wg_ref, bg_ref, wa_ref, wb_ref, wc_ref, wd_ref, wo_ref, wr_ref, br_ref,
                  x1_ref, h2_ref, lg_ref):
    d = x_ref.shape[1]
    h = h_ref[...]
    merged = None
    for i, (y_ref, w_ref) in enumerate(((ya_ref, wa_ref), (yb_ref, wb_ref), (yc_ref, wc_ref), (yd_ref, wd_ref))):
        gate = jax.nn.sigmoid(_dot(h, wg_ref[:, i * d:(i + 1) * d]) + bg_ref[:, i * d:(i + 1) * d])
        term = gate * _dot(y_ref[...], w_ref[...])
        merged = term if merged is None else merged + term
    mod = mod_ref[...]
    x1 = x_ref[...] + mod[2:3, :] * _dot(merged.astype(BF16), wo_ref[...])
    x1_ref[...] = x1
    h2 = _rms_mod(x1, nw_ref[...], mod[3:4, :], mod[4:5, :])
    h2_ref[...] = h2
    h2_hi = h2.astype(BF16)
    h2_lo = (h2 - h2_hi.astype(F32)).astype(BF16)
    lg_ref[...] = (_dot(h2_hi, wr_ref[0]) + _dot(h2_lo, wr_ref[0]) + _dot(h2_hi, wr_ref[1])) + br_ref[...]


def _merge(x, h, ya, yb_t, yc, yd, mod, nw, wts, tm, rows_per_mod, seq_len):
    m, d = x.shape
    cw = ya.shape[1]
    nchunk = seq_len // tm
    row = lambda i: (i, 0)
    const2 = lambda i: (0, 0)
    w_specs = [_resident(w.shape) for w in wts]
    return pl.pallas_call(
        _merge_kernel,
        grid=(m // tm,),
        in_specs=[pl.BlockSpec((tm, d), row), pl.BlockSpec((tm, d), row),
                  pl.BlockSpec((tm, cw), row),
                  pl.BlockSpec((tm, cw), lambda i: (i % nchunk, i // nchunk)),
                  pl.BlockSpec((tm, cw), row),
                  pl.BlockSpec((tm, yd.shape[1]), row),
                  pl.BlockSpec((None, 8, d), lambda i: ((i * tm) // rows_per_mod, 0, 0)),
                  pl.BlockSpec((1, d), const2)] + w_specs,
        out_specs=[pl.BlockSpec((tm, d), row), pl.BlockSpec((tm, d), row),
                   pl.BlockSpec((tm, ROUTE_LANES), row)],
        out_shape=[jax.ShapeDtypeStruct((m, d), F32), jax.ShapeDtypeStruct((m, d), F32),
                   jax.ShapeDtypeStruct((m, ROUTE_LANES), F32)],
        compiler_params=_cparams("parallel"),
        name="merge",
    )(x, h, ya, yb_t, yc, yd, mod, nw, *wts)


ROUTE_TM = 512
EXPERT_TR = 256
MOVE_TM = 256
R_E1, R_E2, R_RANK1, R_RANK2, R_W1, R_W2 = range(6)


def _route_kernel(lg_ref, route_ref, cnt_ref, carry_ref):
    @pl.when(pl.program_id(0) == 0)
    def _():
        carry_ref[...] = jnp.zeros_like(carry_ref)

    lg = lg_ref[...]
    lane = lax.broadcasted_iota(jnp.int32, lg.shape, 1).astype(F32)
    neg = jnp.float32(-jnp.inf)
    big = jnp.float32(ROUTE_LANES)
    gl = jnp.where(lane < MOE_GROUPS, lg, neg)
    gm = jnp.max(gl, axis=-1, keepdims=True)
    g_prob = 1.0 / jnp.sum(jnp.exp(gl - gm), axis=-1, keepdims=True)
    gidx = jnp.min(jnp.where(gl == gm, lane, big), axis=-1, keepdims=True)
    e0 = EXPERT_LANE0 + gidx * MOE_EXPERTS
    el = jnp.where((lane >= e0) & (lane < e0 + MOE_EXPERTS), lg, neg)
    m1 = jnp.max(el, axis=-1, keepdims=True)
    i1 = jnp.min(jnp.where(el == m1, lane, big), axis=-1, keepdims=True)
    el2 = jnp.where(lane == i1, neg, el)
    m2 = jnp.max(el2, axis=-1, keepdims=True)
    i2 = jnp.min(jnp.where(el2 == m2, lane, big), axis=-1, keepdims=True)
    r = jnp.exp(m2 - m1)
    w1 = 1.0 / (1.0 + r)
    w2 = r * w1
    two_hot = jnp.where((lane == i1) | (lane == i2), 1.0, 0.0)
    tm = lg.shape[0]
    ltri = (lax.broadcasted_iota(jnp.int32, (tm, tm), 1) < lax.broadcasted_iota(jnp.int32, (tm, tm), 0))
    rank = carry_ref[0:1, :] + _dot(ltri.astype(BF16), two_hot.astype(BF16))
    pick = lambda idx: jnp.sum(jnp.where(lane == idx, rank, 0.0), axis=-1, keepdims=True)
    rec = jnp.zeros_like(lg)
    for col, val in ((R_E1, i1), (R_E2, i2), (R_RANK1, pick(i1)), (R_RANK2, pick(i2)),
                     (R_W1, w1 * g_prob), (R_W2, w2 * g_prob)):
        rec = jnp.where(lane == col, val, rec)
    route_ref[...] = rec
    carry_ref[...] = carry_ref[...] + jnp.sum(two_hot, axis=0, keepdims=True)
    cnt_ref[...] = carry_ref[...]


def _route(logits):
    m = logits.shape[0]
    spec = pl.BlockSpec((ROUTE_TM, ROUTE_LANES), lambda i: (i, 0))
    return pl.pallas_call(
        _route_kernel, grid=(m // ROUTE_TM,), in_specs=[spec],
        out_specs=[spec, pl.BlockSpec((8, ROUTE_LANES), lambda i: (0, 0))],
        out_shape=[jax.ShapeDtypeStruct((m, ROUTE_LANES), F32), jax.ShapeDtypeStruct((8, ROUTE_LANES), F32)],
        scratch_shapes=[pltpu.VMEM((8, ROUTE_LANES), F32)],
        compiler_params=_cparams("arbitrary"), name="route",
    )(logits)


def _plan_kernel(cnt_ref, seg_ref, tile_ref):
    cnt = cnt_ref[...]
    lane = lax.broadcasted_iota(jnp.int32, cnt.shape, 1)
    is_e = (lane >= EXPERT_LANE0) & (lane < EXPERT_LANE0 + N_EXPERTS)
    size = jnp.where(is_e, jnp.floor((cnt + (EXPERT_TR - 1)) * (1.0 / EXPERT_TR)) * EXPERT_TR, 0.0)
    upper = (lax.broadcasted_iota(jnp.int32, (ROUTE_LANES, ROUTE_LANES), 0)
             < lax.broadcasted_iota(jnp.int32, (ROUTE_LANES, ROUTE_LANES), 1)).astype(F32)
    start = jnp.dot(size, upper, precision=HIGHEST, preferred_element_type=F32)
    end = start + size
    total = jnp.max(end, axis=-1, keepdims=True)
    row = lax.broadcasted_iota(jnp.int32, cnt.shape, 0)
    seg = jnp.where(row == 0, start, jnp.where(row == 1, size, jnp.where(row == 2, end, total * (1.0 / EXPERT_TR))))
    seg_ref[...] = seg.astype(jnp.int32)
    nt = tile_ref.shape[0]
    t0 = (lax.broadcasted_iota(jnp.int32, (nt, ROUTE_LANES), 0) * EXPERT_TR).astype(F32)
    lane_t = lax.broadcasted_iota(jnp.int32, (nt, ROUTE_LANES), 1)
    done = jnp.where((lane_t >= EXPERT_LANE0) & (lane_t < EXPERT_LANE0 + N_EXPERTS) & (end[0:1, :] <= t0), 1.0, 0.0)
    te = jnp.minimum(jnp.sum(done, axis=-1, keepdims=True), N_EXPERTS - 1.0)
    tile_ref[...] = jnp.broadcast_to(te, (nt, ROUTE_LANES)).astype(jnp.int32)


def _plan(counts, n_tiles):
    nt = -(-n_tiles // 8) * 8
    return pl.pallas_call(
        _plan_kernel,
        out_shape=[jax.ShapeDtypeStruct((8, ROUTE_LANES), jnp.int32),
                   jax.ShapeDtypeStruct((nt, ROUTE_LANES), jnp.int32)],
        name="plan",
    )(counts)


def _positions_kernel(route_ref, seg_ref, pos_ref):
    rec = route_ref[...]
    start = seg_ref[0:1, :].astype(F32)
    lane = lax.broadcasted_iota(jnp.int32, rec.shape, 1).astype(F32)
    seg_start = lambda col: jnp.sum(jnp.where(lane == rec[:, col:col + 1], start, 0.0), axis=-1, keepdims=True)
    p1 = seg_start(R_E1) + rec[:, R_RANK1:R_RANK1 + 1]
    p2 = seg_start(R_E2) + rec[:, R_RANK2:R_RANK2 + 1]
    both = jnp.where(lane == 0.0, p1, jnp.where(lane == 1.0, p2, 0.0))
    pos_ref[...] = jnp.transpose(both)[0:8, :].astype(jnp.int32)


def _positions(route, seg):
    m = route.shape[0]
    return pl.pallas_call(
        _positions_kernel, grid=(m // ROUTE_TM,),
        in_specs=[pl.BlockSpec((ROUTE_TM, ROUTE_LANES), lambda i: (i, 0)),
                  pl.BlockSpec((8, ROUTE_LANES), lambda i: (0, 0))],
        out_specs=pl.BlockSpec((8, ROUTE_TM), lambda i: (0, i)),
        out_shape=jax.ShapeDtypeStruct((8, m), jnp.int32),
        compiler_params=_cparams("parallel"), name="positions",
    )(route, seg)


SEG_START, SEG_SIZE, SEG_END, SEG_TILES = range(4)


def _row_copies(src_of, dst_of, sem):
    copies = [pltpu.make_async_copy(src_of(j, k), dst_of(j, k), sem) for j in range(MOVE_TM) for k in (0, 1)]
    for c in copies:
        c.start()
    for c in copies:
        c.wait()


def _dispatch_kernel(seg_s, pos_s, h2_ref, xs_ref, zbuf, zsem, sem, *, n_tiles):
    @pl.when(pl.program_id(0) == 0)
    def _():
        zbuf[...] = jnp.zeros_like(zbuf)

        def zero_tiles(go):
            for e in range(N_EXPERTS):
                lane = EXPERT_LANE0 + e

                @pl.when(seg_s[SEG_SIZE, lane] > 0)
                def _():
                    start = pl.multiple_of(seg_s[SEG_END, lane] - EXPERT_TR, EXPERT_TR)
                    go(pltpu.make_async_copy(zbuf, xs_ref.at[pl.ds(start, EXPERT_TR)], zsem.at[e]))

                tile = n_tiles - N_EXPERTS + e

                @pl.when(tile >= seg_s[SEG_TILES, 0])
                def _():
                    dst = xs_ref.at[pl.ds(tile * EXPERT_TR, EXPERT_TR)]
                    go(pltpu.make_async_copy(zbuf, dst, zsem.at[N_EXPERTS + e]))

        zero_tiles(lambda c: c.start())
        zero_tiles(lambda c: c.wait())

    _row_copies(lambda j, k: h2_ref.at[pl.ds(j, 1)], lambda j, k: xs_ref.at[pl.ds(pos_s[k, j], 1)], sem)


def _dispatch(h2, pos, seg, n_tiles):
    m, d = h2.shape
    return pl.pallas_call(
        functools.partial(_dispatch_kernel, n_tiles=n_tiles),
        grid_spec=pltpu.PrefetchScalarGridSpec(
            num_scalar_prefetch=1, grid=(m // MOVE_TM,),
            in_specs=[pl.BlockSpec((8, MOVE_TM), lambda i, seg: (0, i), memory_space=pltpu.SMEM),
                      pl.BlockSpec((MOVE_TM, d), lambda i, seg: (i, 0))],
            out_specs=pl.BlockSpec(memory_space=pl.ANY),
            scratch_shapes=[pltpu.VMEM((EXPERT_TR, d), F32), pltpu.SemaphoreType.DMA((2 * N_EXPERTS,)),
                            pltpu.SemaphoreType.DMA]),
        out_shape=jax.ShapeDtypeStruct((n_tiles * EXPERT_TR, d), F32),
        compiler_params=_cparams("arbitrary"), name="dispatch",
    )(seg, pos, h2)


def _experts_kernel(te_s, seg_s, xs_ref, wg_ref, wu_ref, wd_ref, ys_ref):
    used = pl.program_id(0) < seg_s[SEG_TILES, 0]

    @pl.when(used)
    def _():
        x = xs_ref[...].astype(BF16)
        a = _dot(x, wg_ref[...].astype(BF16))
        b = _dot(x, wu_ref[...].astype(BF16))
        ys_ref[...] = _dot((_silu(a) * b).astype(BF16), wd_ref[...].astype(BF16))

    @pl.when(jnp.logical_not(used))
    def _():
        ys_ref[...] = jnp.zeros_like(ys_ref)


def _experts(xs, tile_expert, seg, w_gate, w_up, w_down, layer):
    p, d = xs.shape
    f = w_gate.shape[-1]
    last = lambda j, seg: jnp.minimum(j, seg[SEG_TILES, 0] - 1)
    wmap = lambda j, te, seg: (layer * N_EXPERTS + te[last(j, seg)], 0, 0)
    return pl.pallas_call(
        _experts_kernel,
        grid_spec=pltpu.PrefetchScalarGridSpec(
            num_scalar_prefetch=2, grid=(p // EXPERT_TR,),
            in_specs=[pl.BlockSpec((EXPERT_TR, d), lambda j, te, seg: (last(j, seg), 0)),
                      pl.BlockSpec((None, d, f), wmap), pl.BlockSpec((None, d, f), wmap),
                      pl.BlockSpec((None, f, d), wmap)],
            out_specs=pl.BlockSpec((EXPERT_TR, d), lambda j, te, seg: (j, 0))),
        out_shape=jax.ShapeDtypeStruct((p, d), F32),
        compiler_params=_cparams("arbitrary"), name="experts",
    )(tile_expert, seg, xs, w_gate, w_up, w_down)


def _combine_kernel(pos_s, route_ref, x1_ref, mod_ref, ys_ref, o_ref, y1buf, y2buf, sem):
    bufs = (y1buf, y2buf)
    _row_copies(lambda j, k: ys_ref.at[pl.ds(pos_s[k, j], 1)], lambda j, k: bufs[k].at[pl.ds(j, 1)], sem)
    rec = route_ref[...]
    moe = rec[:, R_W1:R_W1 + 1] * y1buf[...] + rec[:, R_W2:R_W2 + 1] * y2buf[...]
    o_ref[...] = x1_ref[...] + mod_ref[5:6, :] * moe


def _combine(ys, pos, route, x1, mod, rows_per_mod):
    m, d = x1.shape
    row = lambda i: (i, 0)
    return pl.pallas_call(
        _combine_kernel,
        grid=(m // MOVE_TM,),
        in_specs=[pl.BlockSpec((8, MOVE_TM), lambda i: (0, i), memory_space=pltpu.SMEM),
                  pl.BlockSpec((MOVE_TM, ROUTE_LANES), row), pl.BlockSpec((MOVE_TM, d), row),
                  pl.BlockSpec((None, 8, d), lambda i: ((i * MOVE_TM) // rows_per_mod, 0, 0)),
                  pl.BlockSpec(memory_space=pl.ANY)],
        out_specs=pl.BlockSpec((MOVE_TM, d), row),
        out_shape=jax.ShapeDtypeStruct((m, d), F32),
        scratch_shapes=[pltpu.VMEM((MOVE_TM, d), F32), pltpu.VMEM((MOVE_TM, d), F32), pltpu.SemaphoreType.DMA],
        compiler_params=_cparams("arbitrary"), name="combine",
    )(pos, route, x1, mod, ys)


def _final_norm_kernel(x_ref, nw_ref, o_ref):
    x = x_ref[...]
    o_ref[...] = x * lax.rsqrt(jnp.mean(x * x, axis=-1, keepdims=True) + EPS) * nw_ref[...]


def _final_norm(x, nw, tm):
    m, d = x.shape
    spec = pl.BlockSpec((tm, d), lambda i: (i, 0))
    return pl.pallas_call(
        _final_norm_kernel, grid=(m // tm,),
        in_specs=[spec, pl.BlockSpec((1, d), lambda i: (0, 0))], out_specs=spec,
        out_shape=jax.ShapeDtypeStruct((m, d), F32),
        compiler_params=_cparams("parallel"), name="final_norm",
    )(x, nw)


def _layer_weights(P, l):
    d = P['w_in'].shape[1]
    cw = P['pool_scale'].shape[1]
    hw = P['hyena_conv'].shape[2]
    o1, o2, o3 = cw, cw + hw, cw + hw + 3 * cw
    na = (P['w_in'].shape[2] - o3) // 3
    w_in = P['w_in'][l]
    segs = [w_in[:, o1:o2], w_in[:, o2:o3], w_in[:, o3:o3 + na], w_in[:, o3 + na:o3 + 2 * na],
            w_in[:, o3 + 2 * na:], w_in[:, :o1]]
    widths = tuple(int(s.shape[1]) for s in segs)
    w_in_r = jnp.concatenate(segs, axis=1).astype(BF16)
    gw = cw // len(POOL_WINDOWS)
    pool_bd = jnp.zeros((cw, cw), F32)
    for g in range(len(POOL_WINDOWS)):
        pool_bd = pool_bd.at[g * gw:(g + 1) * gw, g * gw:(g + 1) * gw].set(P['pool_w'][l, g])
    wr = jnp.concatenate([P['w_route_group'][l],
                          jnp.transpose(P['w_route_exp'][l], (1, 0, 2)).reshape(d, N_EXPERTS)], axis=1)
    wr = jnp.zeros((d, ROUTE_LANES), F32).at[:, :wr.shape[1]].set(wr)
    wr_hi = wr.astype(BF16)
    wr_lo = (wr - wr_hi.astype(F32)).astype(BF16)
    br = jnp.concatenate([P['b_route_group'][l], P['b_route_exp'][l].reshape(-1)])
    br = jnp.zeros((1, ROUTE_LANES), F32).at[0, :br.shape[0]].set(br)
    return dict(
        widths=widths, w_in_r=w_in_r, norm_mix=P['norm_mix'][l][None], norm_ffn=P['norm_ffn'][l][None],
        pool_bd=pool_bd.astype(BF16), pool_scale=P['pool_scale'][l][None], sconv_w=P['sconv_w'][l],
        hyena_conv=P['hyena_conv'][l], hyena_bias=P['hyena_bias'][l],
        merge=(P['w_gate'][l].astype(BF16), P['b_gate'][l][None], P['w_br_a'][l].astype(BF16),
               P['w_br_b'][l].astype(BF16), P['w_br_c'][l].astype(BF16), P['w_br_d'][l].astype(BF16),
               P['w_out'][l].astype(BF16), jnp.stack([wr_hi, wr_lo]), br),
    )


def _run_stream(x3, mods, LW, EW, hy, attend, depth):
    b, seq_len, d = x3.shape
    m = b * seq_len
    x = x3.reshape(m, d)
    rows_per_mod = m // mods.shape[1]
    tm = min(512, seq_len)
    tc = min(256, seq_len)
    kvs = []
    for l in range(depth):
        W = LW[l]
        mod = mods[l]
        cw = W['pool_scale'].shape[1]
        h, ph, ps, q, k, v, pp = _inproj(x, mod, W['norm_mix'], W['w_in_r'], W['widths'], tm, rows_per_mod)
        kvs.append((k, v))
        ya, yc, u_t, x0_t = _local_mixers(pp, ps, ph, W['pool_bd'], W['pool_scale'], W['sconv_w'],
                                          W['hyena_conv'], seq_len, tc)
        yb_t = _hyena_conv(u_t, x0_t, hy['fwd_bf'], hy['inv_c'], hy['inv_s'], hy['spec'][l],
                           W['hyena_bias'], cw)
        yd = attend(l, q, k, v)
        x1, h2, logits = _merge(x, h, ya, yb_t, yc, yd, mod, W['norm_ffn'], W['merge'], tm,
                                rows_per_mod, seq_len)
        route, counts = _route(logits)
        n_tiles = 2 * m // EXPERT_TR + N_EXPERTS
        seg, tile_tab = _plan(counts, n_tiles)
        pos = _positions(route, seg)
        xs = _dispatch(h2, pos, seg, n_tiles)
        ys = _experts(xs, tile_tab[:, 0], seg, EW[0], EW[1], EW[2], l)
        x = _combine(ys, pos, route, x1, mod, rows_per_mod)
    return x, kvs


def _hyena_setup(seq_len, P, depth):
    fwd32, inv_c, inv_s = _dft_matrices(seq_len)
    spec = []
    for l in range(depth):
        filt = _hyena_filter(seq_len, P['hyena_f1'][l], P['hyena_fb1'][l], P['hyena_f2'][l],
                             P['hyena_fb2'][l], P['hyena_f3'][l], P['hyena_freq'][l], P['hyena_decay'][l])
        spec.append(_filter_spectrum(fwd32, filt))
    return dict(fwd_bf=fwd32.astype(BF16), inv_c=inv_c, inv_s=inv_s, spec=spec)


def kernel(x_prompt, x_sample, cache_k, cache_v, c, c_ctx, w_ada, b_ada, norm_mix, w_in, w_gate, b_gate, pool_w, pool_scale, hyena_conv, hyena_f1, hyena_fb1, hyena_f2, hyena_fb2, hyena_f3, hyena_freq, hyena_decay, hyena_bias, sconv_w, na_rpb, w_br_a, w_br_b, w_br_c, w_br_d, w_out, norm_ffn, w_route_group, b_route_group, w_route_exp, b_route_exp, w_e_gate, w_e_up, w_e_down, norm_final):
    P = dict(w_in=w_in, w_gate=w_gate, b_gate=b_gate, pool_w=pool_w, pool_scale=pool_scale,
             hyena_conv=hyena_conv, hyena_f1=hyena_f1, hyena_fb1=hyena_fb1, hyena_f2=hyena_f2,
             hyena_fb2=hyena_fb2, hyena_f3=hyena_f3, hyena_freq=hyena_freq, hyena_decay=hyena_decay,
             hyena_bias=hyena_bias, sconv_w=sconv_w, w_br_a=w_br_a, w_br_b=w_br_b, w_br_c=w_br_c,
             w_br_d=w_br_d, w_out=w_out, norm_mix=norm_mix, norm_ffn=norm_ffn,
             w_route_group=w_route_group, b_route_group=b_route_group, w_route_exp=w_route_exp,
             b_route_exp=b_route_exp, w_e_gate=w_e_gate, w_e_up=w_e_up, w_e_down=w_e_down)
    depth, d, _ = w_ada.shape
    bp, lp, _ = x_prompt.shape
    bs, ls, _ = x_sample.shape
    assert (ls // GRID_W) % LAT_QROWS == 0 and ls // GRID_W >= LAT_KROWS

    n_c = 1 + bs
    n_pad = -(-n_c // 8) * 8
    cvecs = jnp.zeros((n_pad, d), F32).at[0].set(c_ctx).at[1:n_c].set(c)
    ada = _ada(cvecs, w_ada, b_ada).reshape(depth, n_pad, 6, d)
    ada = jnp.concatenate([ada, jnp.zeros((depth, n_pad, 2, d), F32)], axis=2)

    LW = [_layer_weights(P, l) for l in range(depth)]
    nw_final = norm_final[None]
    f = w_e_gate.shape[-1]
    EW = (w_e_gate.reshape(depth * N_EXPERTS, d, f), w_e_up.reshape(depth * N_EXPERTS, d, f),
          w_e_down.reshape(depth * N_EXPERTS, f, d))

    hy_p = _hyena_setup(lp, P, depth)
    xp, kv_p = _run_stream(x_prompt, ada[:, 0:1], LW, EW, hy_p,
                           lambda l, q, k, v: _context_attention(q, k, v, lp), depth)
    y_prompt = _final_norm(xp, nw_final, min(512, lp)).reshape(bp, lp, d)
    new_k = jnp.stack([k.reshape(bp, lp, N_HEADS, HEAD_DIM) for k, _ in kv_p], axis=1)
    new_v = jnp.stack([v.reshape(bp, lp, N_HEADS, HEAD_DIM) for _, v in kv_p], axis=1)

    hy_s = _hyena_setup(ls, P, depth)
    past = cache_k.shape[2]
    ck = cache_k.reshape(bs, depth, past, N_HEADS * HEAD_DIM)
    cv = cache_v.reshape(bs, depth, past, N_HEADS * HEAD_DIM)
    biases = [_latent_bias(na_rpb[l], ls // GRID_W) for l in range(depth)]
    xs, _ = _run_stream(x_sample, ada[:, 1:n_c], LW, EW, hy_s,
                        lambda l, q, k, v: _latent_attention(q, k, v, ck, cv, l, biases[l], ls),
                        depth)
    y_sample = _final_norm(xs, nw_final, min(512, ls)).reshape(bs, ls, d)
    return (y_prompt, y_sample, new_k, new_v)
```

```python
import functools
import math

import numpy as np
import jax
import jax.numpy as jnp
from jax import lax
from jax.experimental import pallas as pl
from jax.experimental.pallas import tpu as pltpu

F32 = jnp.float32
BF16 = jnp.bfloat16
HIGHEST = lax.Precision.HIGHEST

EPS = 1e-6
GRID_W = 64
NA_ROWS = 8
NA_COLS = 16
N_HEADS = 8
HEAD_DIM = 64
POOL_WINDOWS = (2, 4, 8, 16)
HYENA_BANDS = 16
MOE_GROUPS = 4
MOE_EXPERTS = 8
N_EXPERTS = MOE_GROUPS * MOE_EXPERTS
ROUTE_LANES = 128
EXPERT_LANE0 = MOE_GROUPS
HALO = 8
NEG_BIG = -1e30
VMEM_LIMIT_BYTES = 48 * 1024 * 1024


def _cparams(*sem):
    return pltpu.CompilerParams(dimension_semantics=sem, vmem_limit_bytes=VMEM_LIMIT_BYTES)


def _resident(shape):
    nd = len(shape)
    return pl.BlockSpec(shape, lambda *_: (0,) * nd, pipeline_mode=pl.Buffered(1))


def _dot(a, b):
    return jnp.dot(a, b, preferred_element_type=F32)


def _silu(x):
    return x * jax.nn.sigmoid(x)


def _ada_kernel(cv_ref, w_ref, b_ref, o_ref):
    o_ref[...] = jnp.dot(_silu(cv_ref[...]), w_ref[...], precision=HIGHEST,
                         preferred_element_type=F32) + b_ref[...]


def _ada(cvecs, w_ada, b_ada):
    depth, d, d6 = w_ada.shape
    r = cvecs.shape[0]
    return pl.pallas_call(
        _ada_kernel,
        grid=(depth, d6 // d),
        in_specs=[pl.BlockSpec((r, d), lambda l, j: (0, 0)),
                  pl.BlockSpec((None, d, d), lambda l, j: (l, 0, j)),
                  pl.BlockSpec((None, 1, d), lambda l, j: (l, 0, j))],
        out_specs=pl.BlockSpec((None, r, d), lambda l, j: (l, 0, j)),
        out_shape=jax.ShapeDtypeStruct((depth, r, d6), F32),
        compiler_params=_cparams("parallel", "parallel"),
        name="ada",
    )(cvecs, w_ada, b_ada.reshape(depth, 1, d6))


def _rms_mod(x, nw, shift, scale):
    y = x * lax.rsqrt(jnp.mean(x * x, axis=-1, keepdims=True) + EPS) * nw
    return y * (1.0 + scale) + shift


def _inproj_kernel(x_ref, mod_ref, nw_ref, w_ref, h_ref, ph_ref, ps_ref, q_ref, k_ref, v_ref, pp_ref,
                   *, widths):
    mod = mod_ref[...]
    h = _rms_mod(x_ref[...], nw_ref[...], mod[0:1, :], mod[1:2, :]).astype(BF16)
    h_ref[...] = h
    off = 0
    for ref, wd in zip((ph_ref, ps_ref, q_ref, k_ref, v_ref, pp_ref), widths):
        r = _dot(h, w_ref[:, off:off + wd])
        if ref is q_ref:
            r = r * (HEAD_DIM ** -0.5)
        ref[...] = r.astype(ref.dtype)
        off += wd


def _inproj(x, mod, nw, w_in_r, widths, tm, rows_per_mod):
    m, d = x.shape
    n = w_in_r.shape[1]
    dts = (F32, F32, BF16, F32, F32, F32)
    row = lambda i: (i, 0)
    return pl.pallas_call(
        functools.partial(_inproj_kernel, widths=widths),
        grid=(m // tm,),
        in_specs=[pl.BlockSpec((tm, d), row),
                  pl.BlockSpec((None, 8, d), lambda i: ((i * tm) // rows_per_mod, 0, 0)),
                  pl.BlockSpec((1, d), lambda i: (0, 0)),
                  _resident((d, n))],
        out_specs=[pl.BlockSpec((tm, d), row)] + [pl.BlockSpec((tm, wd), row) for wd in widths],
        out_shape=[jax.ShapeDtypeStruct((m, d), BF16)]
        + [jax.ShapeDtypeStruct((m, wd), dt) for wd, dt in zip(widths, dts)],
        compiler_params=_cparams("parallel"),
        name="inproj",
    )(x, mod, nw, w_in_r)


def _fill_padded(pad_ref, prev_ref, cur_ref, next_ref, first, last, tc):
    zero = jnp.zeros((HALO, cur_ref.shape[1]), F32)
    pad_ref[0:HALO, :] = jnp.where(first, zero, prev_ref[...])
    pad_ref[HALO:HALO + tc, :] = cur_ref[...]
    pad_ref[HALO + tc:2 * HALO + tc, :] = jnp.where(last, zero, next_ref[...])


def _local_kernel(pp_ref, pp_prev, pp_next, ps_ref, ps_prev, ps_next, ph_ref, ph_prev, ph_next,
                  pw_ref, pscale_ref, sw_ref, hw_ref,
                  ya_ref, yc_ref, u_ref, x0_ref,
                  pad_p, pad_s, pad_h, *, seq_len, tc):
    nchunk = seq_len // tc
    j = pl.program_id(0) % nchunk
    first = j == 0
    last = j == nchunk - 1
    cw = pp_ref.shape[1]
    sw = cw

    _fill_padded(pad_p, pp_prev, pp_ref, pp_next, first, last, tc)
    sh = lambda k: pad_p[HALO + k:HALO + k + tc, :]
    u = pp_ref[...]
    sums = {}
    acc = u
    lo_done, hi_done = 0, 0
    for win in POOL_WINDOWS:
        lo, hi = -(win // 2), win // 2 - 1
        for k in range(lo, lo_done):
            acc = acc + sh(k)
        for k in range(hi_done + 1, hi + 1):
            acc = acc + sh(k)
        lo_done, hi_done = lo, hi
        sums[win] = acc
    t = j * tc + lax.broadcasted_iota(jnp.int32, (tc, 1), 0)
    lane = lax.broadcasted_iota(jnp.int32, (1, cw), 1)
    gw = cw // len(POOL_WINDOWS)
    pooled = None
    for g, win in reversed(list(enumerate(POOL_WINDOWS))):
        cnt = jnp.minimum(t - win // 2 + win, seq_len) - jnp.maximum(t - win // 2, 0)
        val = sums[win] * (1.0 / cnt.astype(F32))
        pooled = val if pooled is None else jnp.where(lane < (g + 1) * gw, val, pooled)
    pooled = pooled - u
    ya = _dot(pooled.astype(BF16), pw_ref[...]) * pscale_ref[...]
    ya_ref[...] = ya.astype(BF16)

    _fill_padded(pad_s, ps_prev, ps_ref, ps_next, first, last, tc)
    w3 = sw_ref[...]
    z = lambda k: (pad_s[HALO + k:HALO + k + tc, 2 * sw:3 * sw] * pad_s[HALO + k:HALO + k + tc, 0:sw])
    conv = w3[0:1, :] * z(-1) + w3[1:2, :] * z(0) + w3[2:3, :] * z(1)
    yc_ref[...] = (ps_ref[:, sw:2 * sw] * conv).astype(BF16)

    _fill_padded(pad_h, ph_prev, ph_ref, ph_next, first, last, tc)
    hw = hw_ref[...]
    c3 = (hw[0:1, :] * pad_h[HALO - 1:HALO - 1 + tc, :] + hw[1:2, :] * pad_h[HALO:HALO + tc, :]
          + hw[2:3, :] * pad_h[HALO + 1:HALO + 1 + tc, :])
    x0_ref[...] = c3[:, 0:sw]
    u_ref[...] = c3[:, sw:2 * sw] * c3[:, 2 * sw:3 * sw]


def _local_mixers(pp, ps, ph, pool_bd, pool_scale, sconv_w, hyena_conv, seq_len, tc):
    m, cw = pp.shape
    nb = m // seq_len
    nchunk = seq_len // tc
    hpc = tc // HALO
    nhb = m // HALO
    row = lambda i: (i, 0)
    prev = lambda i: (jnp.maximum(i * hpc - 1, 0), 0)
    nxt = lambda i: (jnp.minimum((i + 1) * hpc, nhb - 1), 0)
    tl = lambda i: (i % nchunk, i // nchunk)
    const = lambda i: (0, 0)

    def trio(width):
        return [pl.BlockSpec((tc, width), row), pl.BlockSpec((HALO, width), prev),
                pl.BlockSpec((HALO, width), nxt)]

    return pl.pallas_call(
        functools.partial(_local_kernel, seq_len=seq_len, tc=tc),
        grid=(m // tc,),
        in_specs=trio(cw) + trio(3 * cw) + trio(3 * cw)
        + [pl.BlockSpec((cw, cw), const), pl.BlockSpec((1, cw), const),
           pl.BlockSpec((3, cw), const), pl.BlockSpec((3, 3 * cw), const)],
        out_specs=[pl.BlockSpec((tc, cw), row), pl.BlockSpec((tc, cw), row),
                   pl.BlockSpec((tc, cw), tl), pl.BlockSpec((tc, cw), tl)],
        out_shape=[jax.ShapeDtypeStruct((m, cw), BF16), jax.ShapeDtypeStruct((m, cw), BF16),
                   jax.ShapeDtypeStruct((seq_len, nb * cw), F32),
                   jax.ShapeDtypeStruct((seq_len, nb * cw), F32)],
        scratch_shapes=[pltpu.VMEM((tc + 2 * HALO, cw), F32), pltpu.VMEM((tc + 2 * HALO, 3 * cw), F32),
                        pltpu.VMEM((tc + 2 * HALO, 3 * cw), F32)],
        compiler_params=_cparams("parallel"),
        name="local_mixers",
    )(pp, pp, pp, ps, ps, ps, ph, ph, ph, pool_bd, pool_scale, sconv_w, hyena_conv)


def _trig(rows, cols, n):
    split = 64
    r = np.asarray(rows, np.int64)[:, None]
    c = np.asarray(cols, np.int64)
    assert c[0] % split == 0 and len(c) % split == 0 and np.all(np.diff(c) == 1)
    c0 = np.arange(split)[None, :]
    c1 = c[::split][None, :]
    ang0 = ((r * c0) % n) * (2.0 * math.pi / n)
    ang1 = ((r * c1) % n) * (2.0 * math.pi / n)
    tab = lambda a: jnp.asarray(a, F32)
    ca, sa = tab(np.cos(ang0))[:, None, :], tab(np.sin(ang0))[:, None, :]
    cb, sb = tab(np.cos(ang1))[:, :, None], tab(np.sin(ang1))[:, :, None]
    shape = (len(rows), len(c))
    return (ca * cb - sa * sb).reshape(shape), (sa * cb + ca * sb).reshape(shape)


def _dft_matrices(seq_len):
    L = seq_len
    n = 2 * L
    k = np.arange(L)
    cos_f, sin_f = _trig(k, k, n)
    alt = jnp.asarray(np.where(k % 2 == 0, 1.0, -1.0), F32)
    im = (-sin_f).at[0, :].set(alt)
    fwd = jnp.concatenate([cos_f, im], axis=0)
    tp = np.arange(L // 2, L // 2 + L)
    cos_i, sin_i = _trig(tp, k, n)
    alt_t = jnp.asarray(np.where(tp % 2 == 0, 1.0, -1.0), F32)
    inv_c = (cos_i * (2.0 / n)).at[:, 0].set(1.0 / n)
    inv_s = (sin_i * (-2.0 / n)).at[:, 0].set(alt_t / n)
    return fwd, inv_c.astype(BF16), inv_s.astype(BF16)


def _hyena_embedding(seq_len, width):
    t = np.arange(seq_len, dtype=np.float64)
    w = (2.0 * math.pi / seq_len) * t
    bands = np.linspace(1e-4, HYENA_BANDS - 1, HYENA_BANDS)
    z = np.concatenate([(t / (seq_len - 1))[:, None], np.cos(w[:, None] * bands),
                        -np.sin(w[:, None] * bands)], axis=-1)
    out = np.zeros((seq_len, width), np.float32)
    out[:, :z.shape[1]] = z
    return jnp.asarray(out)


def _filter_kernel(z_ref, f1_ref, fb1_ref, f2_ref, fb2_ref, f3_ref, freq_ref, decay_ref, o_ref, *, seq_len):
    hdot = lambda a, b: jnp.dot(a, b, precision=HIGHEST, preferred_element_type=F32)
    fr = freq_ref[...]
    hdn = jnp.sin(fr * (hdot(z_ref[...], f1_ref[...]) + fb1_ref[...]))
    hdn = jnp.sin(fr * (hdot(hdn, f2_ref[...]) + fb2_ref[...]))
    filt = hdot(hdn, f3_ref[...])
    t = lax.broadcasted_iota(jnp.int32, (seq_len, 1), 0)
    dist = jnp.abs(t - seq_len // 2).astype(F32) / (seq_len / 2)
    filt = filt * jnp.exp(-dist * decay_ref[...])
    o_ref[...] = filt / jnp.sum(jnp.abs(filt), axis=0, keepdims=True)


def _pad2(a, rows, cols):
    return jnp.zeros((rows, cols), F32).at[:a.shape[0], :a.shape[1]].set(a)


def _hyena_filter(seq_len, f1, fb1, f2, fb2, f3, freq, decay):
    p = 128
    c = f3.shape[1]
    args = (_hyena_embedding(seq_len, p), _pad2(f1, p, p), _pad2(fb1[None], 1, p), _pad2(f2, p, p),
            _pad2(fb2[None], 1, p), _pad2(f3, p, c), _pad2(freq[None], 1, p), decay[None])
    return pl.pallas_call(
        functools.partial(_filter_kernel, seq_len=seq_len),
        out_shape=jax.ShapeDtypeStruct((seq_len, c), F32),
        compiler_params=pltpu.CompilerParams(vmem_limit_bytes=VMEM_LIMIT_BYTES),
        name="hyena_filter",
    )(*args)


def _spectrum_kernel(f_ref, h_ref, o_ref):
    o_ref[...] = jnp.dot(f_ref[...], h_ref[...], precision=HIGHEST, preferred_element_type=F32)


def _filter_spectrum(fwd32, filt):
    n, L = fwd32.shape
    c = filt.shape[1]
    tf = min(n, 512)
    return pl.pallas_call(
        _spectrum_kernel,
        grid=(n // tf,),
        in_specs=[pl.BlockSpec((tf, L), lambda i: (i, 0)), pl.BlockSpec((L, c), lambda i: (0, 0))],
        out_specs=pl.BlockSpec((tf, c), lambda i: (i, 0)),
        out_shape=jax.ShapeDtypeStruct((n, c), F32),
        compiler_params=_cparams("parallel"),
        name="filter_spectrum",
    )(fwd32, filt)


def _hy_fwd_kernel(fc_ref, fs_ref, u_ref, hr_ref, hi_ref, yr_ref, yi_ref, *, cw):
    ub = u_ref[...].astype(BF16)
    ur = _dot(fc_ref[...], ub)
    ui = _dot(fs_ref[...], ub)
    tf = ur.shape[0]
    row0 = (pl.program_id(0) * tf + lax.broadcasted_iota(jnp.int32, (tf, 1), 0)) == 0
    hr, hi = hr_ref[...], hi_ref[...]
    for s in range(ur.shape[1] // cw):
        a, b = ur[:, s * cw:(s + 1) * cw], ui[:, s * cw:(s + 1) * cw]
        yr = jnp.where(row0, a * hr, a * hr - b * hi)
        yi = jnp.where(row0, b * hi, a * hi + b * hr)
        yr_ref[:, s * cw:(s + 1) * cw] = yr.astype(BF16)
        yi_ref[:, s * cw:(s + 1) * cw] = yi.astype(BF16)


def _hy_inv_kernel(ic_ref, is_ref, yr_ref, yi_ref, u_ref, x0_ref, bias_ref, o_ref):
    conv = _dot(ic_ref[...], yr_ref[...]) + _dot(is_ref[...], yi_ref[...])
    o_ref[...] = ((conv + bias_ref[...] * u_ref[...]) * x0_ref[...]).astype(BF16)


def _hyena_conv(u_t, x0_t, fwd_bf, inv_c, inv_s, spec, bias, cw):
    L, ncol = u_t.shape
    tf = min(L, 1024)
    tn = min(ncol, 512)
    nf = L // tf
    yr, yi = pl.pallas_call(
        functools.partial(_hy_fwd_kernel, cw=cw),
        grid=(nf, ncol // tn),
        in_specs=[pl.BlockSpec((tf, L), lambda i, j: (i, 0)),
                  pl.BlockSpec((tf, L), lambda i, j: (nf + i, 0)),
                  pl.BlockSpec((L, tn), lambda i, j: (0, j)),
                  pl.BlockSpec((tf, cw), lambda i, j: (i, 0)),
                  pl.BlockSpec((tf, cw), lambda i, j: (nf + i, 0))],
        out_specs=[pl.BlockSpec((tf, tn), lambda i, j: (i, j))] * 2,
        out_shape=[jax.ShapeDtypeStruct((L, ncol), BF16)] * 2,
        compiler_params=_cparams("parallel", "parallel"),
        name="hyena_dft",
    )(fwd_bf, fwd_bf, u_t, spec, spec)
    bias_t = jnp.tile(bias[None, :], (1, tn // cw))
    return pl.pallas_call(
        _hy_inv_kernel,
        grid=(nf, ncol // tn),
        in_specs=[pl.BlockSpec((tf, L), lambda i, j: (i, 0)),
                  pl.BlockSpec((tf, L), lambda i, j: (i, 0)),
                  pl.BlockSpec((L, tn), lambda i, j: (0, j)),
                  pl.BlockSpec((L, tn), lambda i, j: (0, j)),
                  pl.BlockSpec((tf, tn), lambda i, j: (i, j)),
                  pl.BlockSpec((tf, tn), lambda i, j: (i, j)),
                  pl.BlockSpec((1, tn), lambda i, j: (0, 0))],
        out_specs=pl.BlockSpec((tf, tn), lambda i, j: (i, j)),
        out_shape=jax.ShapeDtypeStruct((L, ncol), BF16),
        compiler_params=_cparams("parallel", "parallel"),
        name="hyena_idft",
    )(inv_c, inv_s, yr, yi, u_t, x0_t, bias_t)


def _softmax_pv(parts):
    m = None
    for s, _ in parts:
        mi = jnp.max(s, axis=-1, keepdims=True)
        m = mi if m is None else jnp.maximum(m, mi)
    den, out = None, None
    for s, v in parts:
        p = jnp.exp(s - m)
        d = jnp.sum(p, axis=-1, keepdims=True)
        o = _dot(p.astype(BF16), v)
        den = d if den is None else den + d
        out = o if out is None else out + o
    return out * (1.0 / den)


def _qkt(q, k):
    return lax.dot_general(q, k, (((1,), (1,)), ((), ())), preferred_element_type=F32)


def _ctx_attn_kernel(q_ref, k_ref, v_ref, o_ref):
    for h in range(N_HEADS):
        sl = slice(h * HEAD_DIM, (h + 1) * HEAD_DIM)
        k = k_ref[:, sl].astype(BF16)
        v = v_ref[:, sl].astype(BF16)
        o = _softmax_pv([(_qkt(q_ref[:, sl], k), v)])
        o_ref[:, sl] = o.astype(BF16)


def _context_attention(q, k, v, seq_len):
    m, w = q.shape
    spec = pl.BlockSpec((seq_len, w), lambda b: (b, 0))
    return pl.pallas_call(
        _ctx_attn_kernel,
        grid=(m // seq_len,),
        in_specs=[spec, spec, spec],
        out_specs=spec,
        out_shape=jax.ShapeDtypeStruct((m, w), BF16),
        compiler_params=_cparams("parallel"),
        name="context_attention",
    )(q, k, v)


LAT_QROWS = 4
LAT_KROWS = NA_ROWS + LAT_QROWS
LAT_TQ = LAT_QROWS * GRID_W


def _lat_attn_kernel(q_ref, k0_ref, k1_ref, k2_ref, v0_ref, v1_ref, v2_ref, ck_ref, cv_ref, g_ref, o_ref,
                     *, rows):
    nblk = rows // LAT_QROWS
    rblk = pl.program_id(0)
    kblk = jnp.clip(rblk - 1, 0, nblk - 3)
    slot = {}
    for ri in range(LAT_QROWS):
        r = rblk * LAT_QROWS + ri
        rs = jnp.clip(r - NA_ROWS // 2, 0, rows - NA_ROWS)
        for kj in range(LAT_KROWS):
            krow = kblk * LAT_QROWS + kj
            in_window = (krow >= rs) & (krow < rs + NA_ROWS)
            slot[ri, kj] = jnp.where(in_window, krow - r + NA_ROWS - 1, 2 * NA_ROWS - 1)
    low_half = lax.broadcasted_iota(jnp.int32, (GRID_W, 2 * GRID_W), 1) < GRID_W

    def bias_block(h, i):
        row_blocks = []
        for ri in range(LAT_QROWS):
            tiles = []
            for kj in range(LAT_QROWS * i, LAT_QROWS * (i + 1), 2):
                tiles.append(jnp.where(low_half, g_ref[h, slot[ri, kj]], g_ref[h, slot[ri, kj + 1]]))
            row_blocks.append(jnp.concatenate(tiles, axis=1))
        return jnp.concatenate(row_blocks, axis=0)

    for h in range(N_HEADS):
        sl = slice(h * HEAD_DIM, (h + 1) * HEAD_DIM)
        q = q_ref[:, sl]
        parts = []
        for i, (kr, vr) in enumerate(((k0_ref, v0_ref), (k1_ref, v1_ref), (k2_ref, v2_ref))):
            s = _qkt(q, kr[:, sl].astype(BF16)) + bias_block(h, i)
            parts.append((s, vr[:, sl].astype(BF16)))
        parts.append((_qkt(q, ck_ref[:, sl].astype(BF16)), cv_ref[:, sl].astype(BF16)))
        o_ref[:, sl] = _softmax_pv(parts).astype(BF16)


def _latent_bias(rpb, rows):
    del rows
    nh, ndr, ndc = rpb.shape
    qc = np.arange(GRID_W)[:, None]
    kc = (np.arange(2 * GRID_W) % GRID_W)[None, :]
    ws = np.clip(qc - NA_COLS // 2, 0, GRID_W - NA_COLS)
    col_ok = ((kc >= ws) & (kc < ws + NA_COLS)).reshape(-1)
    dc = np.clip(kc - qc + NA_COLS - 1, 0, ndc - 1).reshape(-1)
    onehot = jnp.asarray((dc[None, :] == np.arange(ndc)[:, None]) & col_ok[None, :], F32)
    g = jnp.einsum('hrd,dx->hrx', rpb, onehot, precision=HIGHEST)
    g = g + jnp.asarray(np.where(col_ok, 0.0, NEG_BIG), F32)
    g = jnp.concatenate([g, jnp.full((nh, 2 * NA_ROWS - ndr, g.shape[-1]), NEG_BIG, F32)], axis=1)
    return g.reshape(nh, 2 * NA_ROWS, GRID_W, 2 * GRID_W)


def _latent_attention(q, k, v, ctx_k, ctx_v, layer, bias, seq_len):
    m, w = q.shape
    rows = seq_len // GRID_W
    nblk = rows // LAT_QROWS
    nb = m // seq_len
    tq = LAT_TQ
    nctx = ctx_k.shape[2]

    def kspec(i):
        return pl.BlockSpec((tq, w), lambda r, b: (b * nblk + jnp.clip(r - 1, 0, nblk - 3) + i, 0))

    qspec = pl.BlockSpec((tq, w), lambda r, b: (b * nblk + r, 0))
    cspec = pl.BlockSpec((None, None, nctx, w), lambda r, b: (b, layer, 0, 0))
    return pl.pallas_call(
        functools.partial(_lat_attn_kernel, rows=rows),
        grid=(nblk, nb),
        in_specs=[qspec, kspec(0), kspec(1), kspec(2), kspec(0), kspec(1), kspec(2), cspec, cspec,
                  _resident(bias.shape)],
        out_specs=qspec,
        out_shape=jax.ShapeDtypeStruct((m, w), BF16),
        compiler_params=_cparams("parallel", "parallel"),
        name="latent_attention",
    )(q, k, k, k, v, v, v, ctx_k, ctx_v, bias)


def _merge_kernel(x_ref, h_ref, ya_ref, yb_ref, yc_ref, yd_ref, mod_ref, nw_ref,
                  wg_ref, bg_ref, wa_ref, wb_ref, wc_ref, wd_ref, wo_ref, wr_ref, br_ref,
                  x1_ref, h2_ref, lg_ref):
    d = x_ref.shape[1]
    h = h_ref[...]
    merged = None
    for i, (y_ref, w_ref) in enumerate(((ya_ref, wa_ref), (yb_ref, wb_ref), (yc_ref, wc_ref), (yd_ref, wd_ref))):
        gate = jax.nn.sigmoid(_dot(h, wg_ref[:, i * d:(i + 1) * d]) + bg_ref[:, i * d:(i + 1) * d])
        term = gate * _dot(y_ref[...], w_ref[...])
        merged = term if merged is None else merged + term
    mod = mod_ref[...]
    x1 = x_ref[...] + mod[2:3, :] * _dot(merged.astype(BF16), wo_ref[...])
    x1_ref[...] = x1
    h2 = _rms_mod(x1, nw_ref[...], mod[3:4, :], mod[4:5, :])
    h2_ref[...] = h2
    h2_hi = h2.astype(BF16)
    h2_lo = (h2 - h2_hi.astype(F32)).astype(BF16)
    lg_ref[...] = (_dot(h2_hi, wr_ref[0]) + _dot(h2_lo, wr_ref[0]) + _dot(h2_hi, wr_ref[1])) + br_ref[...]


def _merge(x, h, ya, yb_t, yc, yd, mod, nw, wts, tm, rows_per_mod, seq_len):
    m, d = x.shape
    cw = ya.shape[1]
    nchunk = seq_len // tm
    row = lambda i: (i, 0)
    const2 = lambda i: (0, 0)
    w_specs = [_resident(w.shape) for w in wts]
    return pl.pallas_call(
        _merge_kernel,
        grid=(m // tm,),
        in_specs=[pl.BlockSpec((tm, d), row), pl.BlockSpec((tm, d), row),
                  pl.BlockSpec((tm, cw), row),
                  pl.BlockSpec((tm, cw), lambda i: (i % nchunk, i // nchunk)),
                  pl.BlockSpec((tm, cw), row),
                  pl.BlockSpec((tm, yd.shape[1]), row),
                  pl.BlockSpec((None, 8, d), lambda i: ((i * tm) // rows_per_mod, 0, 0)),
                  pl.BlockSpec((1, d), const2)] + w_specs,
        out_specs=[pl.BlockSpec((tm, d), row), pl.BlockSpec((tm, d), row),
                   pl.BlockSpec((tm, ROUTE_LANES), row)],
        out_shape=[jax.ShapeDtypeStruct((m, d), F32), jax.ShapeDtypeStruct((m, d), F32),
                   jax.ShapeDtypeStruct((m, ROUTE_LANES), F32)],
        compiler_params=_cparams("parallel"),
        name="merge",
    )(x, h, ya, yb_t, yc, yd, mod, nw, *wts)


ROUTE_TM = 512
EXPERT_TR = 256
MOVE_TM = 256
R_E1, R_E2, R_RANK1, R_RANK2, R_W1, R_W2 = range(6)


def _route_kernel(lg_ref, route_ref, cnt_ref, carry_ref):
    @pl.when(pl.program_id(0) == 0)
    def _():
        carry_ref[...] = jnp.zeros_like(carry_ref)

    lg = lg_ref[...]
    lane = lax.broadcasted_iota(jnp.int32, lg.shape, 1).astype(F32)
    neg = jnp.float32(-jnp.inf)
    big = jnp.float32(ROUTE_LANES)
    gl = jnp.where(lane < MOE_GROUPS, lg, neg)
    gm = jnp.max(gl, axis=-1, keepdims=True)
    g_prob = 1.0 / jnp.sum(jnp.exp(gl - gm), axis=-1, keepdims=True)
    gidx = jnp.min(jnp.where(gl == gm, lane, big), axis=-1, keepdims=True)
    e0 = EXPERT_LANE0 + gidx * MOE_EXPERTS
    el = jnp.where((lane >= e0) & (lane < e0 + MOE_EXPERTS), lg, neg)
    m1 = jnp.max(el, axis=-1, keepdims=True)
    i1 = jnp.min(jnp.where(el == m1, lane, big), axis=-1, keepdims=True)
    el2 = jnp.where(lane == i1, neg, el)
    m2 = jnp.max(el2, axis=-1, keepdims=True)
    i2 = jnp.min(jnp.where(el2 == m2, lane, big), axis=-1, keepdims=True)
    r = jnp.exp(m2 - m1)
    w1 = 1.0 / (1.0 + r)
    w2 = r * w1
    two_hot = jnp.where((lane == i1) | (lane == i2), 1.0, 0.0)
    tm = lg.shape[0]
    ltri = (lax.broadcasted_iota(jnp.int32, (tm, tm), 1) < lax.broadcasted_iota(jnp.int32, (tm, tm), 0))
    rank = carry_ref[0:1, :] + _dot(ltri.astype(BF16), two_hot.astype(BF16))
    pick = lambda idx: jnp.sum(jnp.where(lane == idx, rank, 0.0), axis=-1, keepdims=True)
    rec = jnp.zeros_like(lg)
    for col, val in ((R_E1, i1), (R_E2, i2), (R_RANK1, pick(i1)), (R_RANK2, pick(i2)),
                     (R_W1, w1 * g_prob), (R_W2, w2 * g_prob)):
        rec = jnp.where(lane == col, val, rec)
    route_ref[...] = rec
    carry_ref[...] = carry_ref[...] + jnp.sum(two_hot, axis=0, keepdims=True)
    cnt_ref[...] = carry_ref[...]


def _route(logits):
    m = logits.shape[0]
    spec = pl.BlockSpec((ROUTE_TM, ROUTE_LANES), lambda i: (i, 0))
    return pl.pallas_call(
        _route_kernel, grid=(m // ROUTE_TM,), in_specs=[spec],
        out_specs=[spec, pl.BlockSpec((8, ROUTE_LANES), lambda i: (0, 0))],
        out_shape=[jax.ShapeDtypeStruct((m, ROUTE_LANES), F32), jax.ShapeDtypeStruct((8, ROUTE_LANES), F32)],
        scratch_shapes=[pltpu.VMEM((8, ROUTE_LANES), F32)],
        compiler_params=_cparams("arbitrary"), name="route",
    )(logits)


def _plan_kernel(cnt_ref, seg_ref, tile_ref):
    cnt = cnt_ref[...]
    lane = lax.broadcasted_iota(jnp.int32, cnt.shape, 1)
    is_e = (lane >= EXPERT_LANE0) & (lane < EXPERT_LANE0 + N_EXPERTS)
    size = jnp.where(is_e, jnp.floor((cnt + (EXPERT_TR - 1)) * (1.0 / EXPERT_TR)) * EXPERT_TR, 0.0)
    upper = (lax.broadcasted_iota(jnp.int32, (ROUTE_LANES, ROUTE_LANES), 0)
             < lax.broadcasted_iota(jnp.int32, (ROUTE_LANES, ROUTE_LANES), 1)).astype(F32)
    start = jnp.dot(size, upper, precision=HIGHEST, preferred_element_type=F32)
    end = start + size
    total = jnp.max(end, axis=-1, keepdims=True)
    row = lax.broadcasted_iota(jnp.int32, cnt.shape, 0)
    seg = jnp.where(row == 0, start, jnp.where(row == 1, size, jnp.where(row == 2, end, total * (1.0 / EXPERT_TR))))
    seg_ref[...] = seg.astype(jnp.int32)
    nt = tile_ref.shape[0]
    t0 = (lax.broadcasted_iota(jnp.int32, (nt, ROUTE_LANES), 0) * EXPERT_TR).astype(F32)
    lane_t = lax.broadcasted_iota(jnp.int32, (nt, ROUTE_LANES), 1)
    done = jnp.where((lane_t >= EXPERT_LANE0) & (lane_t < EXPERT_LANE0 + N_EXPERTS) & (end[0:1, :] <= t0), 1.0, 0.0)
    te = jnp.minimum(jnp.sum(done, axis=-1, keepdims=True), N_EXPERTS - 1.0)
    tile_ref[...] = jnp.broadcast_to(te, (nt, ROUTE_LANES)).astype(jnp.int32)


def _plan(counts, n_tiles):
    nt = -(-n_tiles // 8) * 8
    return pl.pallas_call(
        _plan_kernel,
        out_shape=[jax.ShapeDtypeStruct((8, ROUTE_LANES), jnp.int32),
                   jax.ShapeDtypeStruct((nt, ROUTE_LANES), jnp.int32)],
        name="plan",
    )(counts)


def _positions_kernel(route_ref, seg_ref, pos_ref):
    rec = route_ref[...]
    start = seg_ref[0:1, :].astype(F32)
    lane = lax.broadcasted_iota(jnp.int32, rec.shape, 1).astype(F32)
    seg_start = lambda col: jnp.sum(jnp.where(lane == rec[:, col:col + 1], start, 0.0), axis=-1, keepdims=True)
    p1 = seg_start(R_E1) + rec[:, R_RANK1:R_RANK1 + 1]
    p2 = seg_start(R_E2) + rec[:, R_RANK2:R_RANK2 + 1]
    both = jnp.where(lane == 0.0, p1, jnp.where(lane == 1.0, p2, 0.0))
    pos_ref[...] = jnp.transpose(both)[0:8, :].astype(jnp.int32)


def _positions(route, seg):
    m = route.shape[0]
    return pl.pallas_call(
        _positions_kernel, grid=(m // ROUTE_TM,),
        in_specs=[pl.BlockSpec((ROUTE_TM, ROUTE_LANES), lambda i: (i, 0)),
                  pl.BlockSpec((8, ROUTE_LANES), lambda i: (0, 0))],
        out_specs=pl.BlockSpec((8, ROUTE_TM), lambda i: (0, i)),
        out_shape=jax.ShapeDtypeStruct((8, m), jnp.int32),
        compiler_params=_cparams("parallel"), name="positions",
    )(route, seg)


SEG_START, SEG_SIZE, SEG_END, SEG_TILES = range(4)


def _row_copies(src_of, dst_of, sem):
    copies = [pltpu.make_async_copy(src_of(j, k), dst_of(j, k), sem) for j in range(MOVE_TM) for k in (0, 1)]
    for n, c in enumerate(copies):
        c.start(priority=n % 2)
    for c in copies:
        c.wait()


def _dispatch_kernel(seg_s, pos_s, h2_ref, xs_ref, zbuf, zsem, sem, *, n_tiles):
    @pl.when(pl.program_id(0) == 0)
    def _():
        zbuf[...] = jnp.zeros_like(zbuf)

        def zero_tiles(go):
            for e in range(N_EXPERTS):
                lane = EXPERT_LANE0 + e

                @pl.when(seg_s[SEG_SIZE, lane] > 0)
                def _():
                    start = pl.multiple_of(seg_s[SEG_END, lane] - EXPERT_TR, EXPERT_TR)
                    go(pltpu.make_async_copy(zbuf, xs_ref.at[pl.ds(start, EXPERT_TR)], zsem.at[e]))

                tile = n_tiles - N_EXPERTS + e

                @pl.when(tile >= seg_s[SEG_TILES, 0])
                def _():
                    dst = xs_ref.at[pl.ds(tile * EXPERT_TR, EXPERT_TR)]
                    go(pltpu.make_async_copy(zbuf, dst, zsem.at[N_EXPERTS + e]))

        zero_tiles(lambda c: c.start())
        zero_tiles(lambda c: c.wait())

    _row_copies(lambda j, k: h2_ref.at[pl.ds(j, 1)], lambda j, k: xs_ref.at[pl.ds(pos_s[k, j], 1)], sem)


def _dispatch(h2, pos, seg, n_tiles):
    m, d = h2.shape
    return pl.pallas_call(
        functools.partial(_dispatch_kernel, n_tiles=n_tiles),
        grid_spec=pltpu.PrefetchScalarGridSpec(
            num_scalar_prefetch=1, grid=(m // MOVE_TM,),
            in_specs=[pl.BlockSpec((8, MOVE_TM), lambda i, seg: (0, i), memory_space=pltpu.SMEM),
                      pl.BlockSpec((MOVE_TM, d), lambda i, seg: (i, 0))],
            out_specs=pl.BlockSpec(memory_space=pl.ANY),
            scratch_shapes=[pltpu.VMEM((EXPERT_TR, d), F32), pltpu.SemaphoreType.DMA((2 * N_EXPERTS,)),
                            pltpu.SemaphoreType.DMA]),
        out_shape=jax.ShapeDtypeStruct((n_tiles * EXPERT_TR, d), F32),
        compiler_params=_cparams("arbitrary"), name="dispatch",
    )(seg, pos, h2)


def _experts_kernel(te_s, seg_s, xs_ref, wg_ref, wu_ref, wd_ref, ys_ref):
    used = pl.program_id(0) < seg_s[SEG_TILES, 0]

    @pl.when(used)
    def _():
        x = xs_ref[...].astype(BF16)
        a = _dot(x, wg_ref[...].astype(BF16))
        b = _dot(x, wu_ref[...].astype(BF16))
        ys_ref[...] = _dot((_silu(a) * b).astype(BF16), wd_ref[...].astype(BF16))

    @pl.when(jnp.logical_not(used))
    def _():
        ys_ref[...] = jnp.zeros_like(ys_ref)


def _experts(xs, tile_expert, seg, w_gate, w_up, w_down, layer):
    p, d = xs.shape
    f = w_gate.shape[-1]
    last = lambda j, seg: jnp.minimum(j, seg[SEG_TILES, 0] - 1)
    wmap = lambda j, te, seg: (layer * N_EXPERTS + te[last(j, seg)], 0, 0)
    return pl.pallas_call(
        _experts_kernel,
        grid_spec=pltpu.PrefetchScalarGridSpec(
            num_scalar_prefetch=2, grid=(p // EXPERT_TR,),
            in_specs=[pl.BlockSpec((EXPERT_TR, d), lambda j, te, seg: (last(j, seg), 0)),
                      pl.BlockSpec((None, d, f), wmap), pl.BlockSpec((None, d, f), wmap),
                      pl.BlockSpec((None, f, d), wmap)],
            out_specs=pl.BlockSpec((EXPERT_TR, d), lambda j, te, seg: (j, 0))),
        out_shape=jax.ShapeDtypeStruct((p, d), F32),
        compiler_params=_cparams("arbitrary"), name="experts",
    )(tile_expert, seg, xs, w_gate, w_up, w_down)


def _combine_kernel(pos_s, route_ref, x1_ref, mod_ref, ys_ref, o_ref, y1buf, y2buf, sem):
    bufs = (y1buf, y2buf)
    _row_copies(lambda j, k: ys_ref.at[pl.ds(pos_s[k, j], 1)], lambda j, k: bufs[k].at[pl.ds(j, 1)], sem)
    rec = route_ref[...]
    moe = rec[:, R_W1:R_W1 + 1] * y1buf[...] + rec[:, R_W2:R_W2 + 1] * y2buf[...]
    o_ref[...] = x1_ref[...] + mod_ref[5:6, :] * moe


def _combine(ys, pos, route, x1, mod, rows_per_mod):
    m, d = x1.shape
    row = lambda i: (i, 0)
    return pl.pallas_call(
        _combine_kernel,
        grid=(m // MOVE_TM,),
        in_specs=[pl.BlockSpec((8, MOVE_TM), lambda i: (0, i), memory_space=pltpu.SMEM),
                  pl.BlockSpec((MOVE_TM, ROUTE_LANES), row), pl.BlockSpec((MOVE_TM, d), row),
                  pl.BlockSpec((None, 8, d), lambda i: ((i * MOVE_TM) // rows_per_mod, 0, 0)),
                  pl.BlockSpec(memory_space=pl.ANY)],
        out_specs=pl.BlockSpec((MOVE_TM, d), row),
        out_shape=jax.ShapeDtypeStruct((m, d), F32),
        scratch_shapes=[pltpu.VMEM((MOVE_TM, d), F32), pltpu.VMEM((MOVE_TM, d), F32), pltpu.SemaphoreType.DMA],
        compiler_params=_cparams("arbitrary"), name="combine",
    )(pos, route, x1, mod, ys)


def _final_norm_kernel(x_ref, nw_ref, o_ref):
    x = x_ref[...]
    o_ref[...] = x * lax.rsqrt(jnp.mean(x * x, axis=-1, keepdims=True) + EPS) * nw_ref[...]


def _final_norm(x, nw, tm):
    m, d = x.shape
    spec = pl.BlockSpec((tm, d), lambda i: (i, 0))
    return pl.pallas_call(
        _final_norm_kernel, grid=(m // tm,),
        in_specs=[spec, pl.BlockSpec((1, d), lambda i: (0, 0))], out_specs=spec,
        out_shape=jax.ShapeDtypeStruct((m, d), F32),
        compiler_params=_cparams("parallel"), name="final_norm",
    )(x, nw)


def _layer_weights(P, l):
    d = P['w_in'].shape[1]
    cw = P['pool_scale'].shape[1]
    hw = P['hyena_conv'].shape[2]
    o1, o2, o3 = cw, cw + hw, cw + hw + 3 * cw
    na = (P['w_in'].shape[2] - o3) // 3
    w_in = P['w_in'][l]
    segs = [w_in[:, o1:o2], w_in[:, o2:o3], w_in[:, o3:o3 + na], w_in[:, o3 + na:o3 + 2 * na],
            w_in[:, o3 + 2 * na:], w_in[:, :o1]]
    widths = tuple(int(s.shape[1]) for s in segs)
    w_in_r = jnp.concatenate(segs, axis=1).astype(BF16)
    gw = cw // len(POOL_WINDOWS)
    pool_bd = jnp.zeros((cw, cw), F32)
    for g in range(len(POOL_WINDOWS)):
        pool_bd = pool_bd.at[g * gw:(g + 1) * gw, g * gw:(g + 1) * gw].set(P['pool_w'][l, g])
    wr = jnp.concatenate([P['w_route_group'][l],
                          jnp.transpose(P['w_route_exp'][l], (1, 0, 2)).reshape(d, N_EXPERTS)], axis=1)
    wr = jnp.zeros((d, ROUTE_LANES), F32).at[:, :wr.shape[1]].set(wr)
    wr_hi = wr.astype(BF16)
    wr_lo = (wr - wr_hi.astype(F32)).astype(BF16)
    br = jnp.concatenate([P['b_route_group'][l], P['b_route_exp'][l].reshape(-1)])
    br = jnp.zeros((1, ROUTE_LANES), F32).at[0, :br.shape[0]].set(br)
    return dict(
        widths=widths, w_in_r=w_in_r, norm_mix=P['norm_mix'][l][None], norm_ffn=P['norm_ffn'][l][None],
        pool_bd=pool_bd.astype(BF16), pool_scale=P['pool_scale'][l][None], sconv_w=P['sconv_w'][l],
        hyena_conv=P['hyena_conv'][l], hyena_bias=P['hyena_bias'][l],
        merge=(P['w_gate'][l].astype(BF16), P['b_gate'][l][None], P['w_br_a'][l].astype(BF16),
               P['w_br_b'][l].astype(BF16), P['w_br_c'][l].astype(BF16), P['w_br_d'][l].astype(BF16),
               P['w_out'][l].astype(BF16), jnp.stack([wr_hi, wr_lo]), br),
    )


def _run_stream(x3, mods, LW, EW, hy, attend, depth):
    b, seq_len, d = x3.shape
    m = b * seq_len
    x = x3.reshape(m, d)
    rows_per_mod = m // mods.shape[1]
    tm = min(512, seq_len)
    tc = min(256, seq_len)
    kvs = []
    for l in range(depth):
        W = LW[l]
        mod = mods[l]
        cw = W['pool_scale'].shape[1]
        h, ph, ps, q, k, v, pp = _inproj(x, mod, W['norm_mix'], W['w_in_r'], W['widths'], tm, rows_per_mod)
        kvs.append((k, v))
        ya, yc, u_t, x0_t = _local_mixers(pp, ps, ph, W['pool_bd'], W['pool_scale'], W['sconv_w'],
                                          W['hyena_conv'], seq_len, tc)
        yb_t = _hyena_conv(u_t, x0_t, hy['fwd_bf'], hy['inv_c'], hy['inv_s'], hy['spec'][l],
                           W['hyena_bias'], cw)
        yd = attend(l, q, k, v)
        x1, h2, logits = _merge(x, h, ya, yb_t, yc, yd, mod, W['norm_ffn'], W['merge'], tm,
                                rows_per_mod, seq_len)
        route, counts = _route(logits)
        n_tiles = 2 * m // EXPERT_TR + N_EXPERTS
        seg, tile_tab = _plan(counts, n_tiles)
        pos = _positions(route, seg)
        xs = _dispatch(h2, pos, seg, n_tiles)
        ys = _experts(xs, tile_tab[:, 0], seg, EW[0], EW[1], EW[2], l)
        x = _combine(ys, pos, route, x1, mod, rows_per_mod)
    return x, kvs


def _hyena_setup(seq_len, P, depth):
    fwd32, inv_c, inv_s = _dft_matrices(seq_len)
    spec = []
    for l in range(depth):
        filt = _hyena_filter(seq_len, P['hyena_f1'][l], P['hyena_fb1'][l], P['hyena_f2'][l],
                             P['hyena_fb2'][l], P['hyena_f3'][l], P['hyena_freq'][l], P['hyena_decay'][l])
        spec.append(_filter_spectrum(fwd32, filt))
    return dict(fwd_bf=fwd32.astype(BF16), inv_c=inv_c, inv_s=inv_s, spec=spec)


def kernel(x_prompt, x_sample, cache_k, cache_v, c, c_ctx, w_ada, b_ada, norm_mix, w_in, w_gate, b_gate, pool_w, pool_scale, hyena_conv, hyena_f1, hyena_fb1, hyena_f2, hyena_fb2, hyena_f3, hyena_freq, hyena_decay, hyena_bias, sconv_w, na_rpb, w_br_a, w_br_b, w_br_c, w_br_d, w_out, norm_ffn, w_route_group, b_route_group, w_route_exp, b_route_exp, w_e_gate, w_e_up, w_e_down, norm_final):
    P = dict(w_in=w_in, w_gate=w_gate, b_gate=b_gate, pool_w=pool_w, pool_scale=pool_scale,
             hyena_conv=hyena_conv, hyena_f1=hyena_f1, hyena_fb1=hyena_fb1, hyena_f2=hyena_f2,
             hyena_fb2=hyena_fb2, hyena_f3=hyena_f3, hyena_freq=hyena_freq, hyena_decay=hyena_decay,
             hyena_bias=hyena_bias, sconv_w=sconv_w, w_br_a=w_br_a, w_br_b=w_br_b, w_br_c=w_br_c,
             w_br_d=w_br_d, w_out=w_out, norm_mix=norm_mix, norm_ffn=norm_ffn,
             w_route_group=w_route_group, b_route_group=b_route_group, w_route_exp=w_route_exp,
             b_route_exp=b_route_exp, w_e_gate=w_e_gate, w_e_up=w_e_up, w_e_down=w_e_down)
    depth, d, _ = w_ada.shape
    bp, lp, _ = x_prompt.shape
    bs, ls, _ = x_sample.shape
    assert (ls // GRID_W) % LAT_QROWS == 0 and ls // GRID_W >= LAT_KROWS

    n_c = 1 + bs
    n_pad = -(-n_c // 8) * 8
    cvecs = jnp.zeros((n_pad, d), F32).at[0].set(c_ctx).at[1:n_c].set(c)
    ada = _ada(cvecs, w_ada, b_ada).reshape(depth, n_pad, 6, d)
    ada = jnp.concatenate([ada, jnp.zeros((depth, n_pad, 2, d), F32)], axis=2)

    LW = [_layer_weights(P, l) for l in range(depth)]
    nw_final = norm_final[None]
    f = w_e_gate.shape[-1]
    EW = (w_e_gate.reshape(depth * N_EXPERTS, d, f), w_e_up.reshape(depth * N_EXPERTS, d, f),
          w_e_down.reshape(depth * N_EXPERTS, f, d))

    hy_p = _hyena_setup(lp, P, depth)
    xp, kv_p = _run_stream(x_prompt, ada[:, 0:1], LW, EW, hy_p,
                           lambda l, q, k, v: _context_attention(q, k, v, lp), depth)
    y_prompt = _final_norm(xp, nw_final, min(512, lp)).reshape(bp, lp, d)
    new_k = jnp.stack([k.reshape(bp, lp, N_HEADS, HEAD_DIM) for k, _ in kv_p], axis=1)
    new_v = jnp.stack([v.reshape(bp, lp, N_HEADS, HEAD_DIM) for _, v in kv_p], axis=1)

    hy_s = _hyena_setup(ls, P, depth)
    past = cache_k.shape[2]
    ck = cache_k.reshape(bs, depth, past, N_HEADS * HEAD_DIM)
    cv = cache_v.reshape(bs, depth, past, N_HEADS * HEAD_DIM)
    biases = [_latent_bias(na_rpb[l], ls // GRID_W) for l in range(depth)]
    xs, _ = _run_stream(x_sample, ada[:, 1:n_c], LW, EW, hy_s,
                        lambda l, q, k, v: _latent_attention(q, k, v, ck, cv, l, biases[l], ls),
                        depth)
    y_sample = _final_norm(xs, nw_final, min(512, ls)).reshape(bs, ls, d)
    return (y_prompt, y_sample, new_k, new_v)
```

```python
import functools
import math

import numpy as np
import jax
import jax.numpy as jnp
from jax import lax
from jax.experimental import pallas as pl
from jax.experimental.pallas import tpu as pltpu

F32 = jnp.float32
BF16 = jnp.bfloat16
HIGHEST = lax.Precision.HIGHEST

EPS = 1e-6
GRID_W = 64
NA_ROWS = 8
NA_COLS = 16
N_HEADS = 8
HEAD_DIM = 64
POOL_WINDOWS = (2, 4, 8, 16)
HYENA_BANDS = 16
MOE_GROUPS = 4
MOE_EXPERTS = 8
N_EXPERTS = MOE_GROUPS * MOE_EXPERTS
ROUTE_LANES = 128
EXPERT_LANE0 = MOE_GROUPS
HALO = 8
NEG_BIG = -1e30
VMEM_LIMIT_BYTES = 48 * 1024 * 1024


def _cparams(*sem):
    return pltpu.CompilerParams(dimension_semantics=sem, vmem_limit_bytes=VMEM_LIMIT_BYTES)


def _resident(shape):
    nd = len(shape)
    return pl.BlockSpec(shape, lambda *_: (0,) * nd, pipeline_mode=pl.Buffered(1))


def _dot(a, b):
    return jnp.dot(a, b, preferred_element_type=F32)


def _silu(x):
    return x * jax.nn.sigmoid(x)


def _ada_kernel(cv_ref, w_ref, b_ref, o_ref):
    o_ref[...] = jnp.dot(_silu(cv_ref[...]), w_ref[...], precision=HIGHEST,
                         preferred_element_type=F32) + b_ref[...]


def _ada(cvecs, w_ada, b_ada):
    depth, d, d6 = w_ada.shape
    r = cvecs.shape[0]
    return pl.pallas_call(
        _ada_kernel,
        grid=(depth, d6 // d),
        in_specs=[pl.BlockSpec((r, d), lambda l, j: (0, 0)),
                  pl.BlockSpec((None, d, d), lambda l, j: (l, 0, j)),
                  pl.BlockSpec((None, 1, d), lambda l, j: (l, 0, j))],
        out_specs=pl.BlockSpec((None, r, d), lambda l, j: (l, 0, j)),
        out_shape=jax.ShapeDtypeStruct((depth, r, d6), F32),
        compiler_params=_cparams("parallel", "parallel"),
        name="ada",
    )(cvecs, w_ada, b_ada.reshape(depth, 1, d6))


def _rms_mod(x, nw, shift, scale):
    y = x * lax.rsqrt(jnp.mean(x * x, axis=-1, keepdims=True) + EPS) * nw
    return y * (1.0 + scale) + shift


def _inproj_kernel(x_ref, mod_ref, nw_ref, w_ref, h_ref, ph_ref, ps_ref, q_ref, k_ref, v_ref, pp_ref,
                   *, widths):
    mod = mod_ref[...]
    h = _rms_mod(x_ref[...], nw_ref[...], mod[0:1, :], mod[1:2, :]).astype(BF16)
    h_ref[...] = h
    off = 0
    for ref, wd in zip((ph_ref, ps_ref, q_ref, k_ref, v_ref, pp_ref), widths):
        r = _dot(h, w_ref[:, off:off + wd])
        if ref is q_ref:
            r = r * (HEAD_DIM ** -0.5)
        ref[...] = r.astype(ref.dtype)
        off += wd


def _inproj(x, mod, nw, w_in_r, widths, tm, rows_per_mod):
    m, d = x.shape
    n = w_in_r.shape[1]
    dts = (F32, F32, BF16, F32, F32, F32)
    row = lambda i: (i, 0)
    return pl.pallas_call(
        functools.partial(_inproj_kernel, widths=widths),
        grid=(m // tm,),
        in_specs=[pl.BlockSpec((tm, d), row),
                  pl.BlockSpec((None, 8, d), lambda i: ((i * tm) // rows_per_mod, 0, 0)),
                  pl.BlockSpec((1, d), lambda i: (0, 0)),
                  _resident((d, n))],
        out_specs=[pl.BlockSpec((tm, d), row)] + [pl.BlockSpec((tm, wd), row) for wd in widths],
        out_shape=[jax.ShapeDtypeStruct((m, d), BF16)]
        + [jax.ShapeDtypeStruct((m, wd), dt) for wd, dt in zip(widths, dts)],
        compiler_params=_cparams("parallel"),
        name="inproj",
    )(x, mod, nw, w_in_r)


def _fill_padded(pad_ref, prev_ref, cur_ref, next_ref, first, last, tc):
    zero = jnp.zeros((HALO, cur_ref.shape[1]), F32)
    pad_ref[0:HALO, :] = jnp.where(first, zero, prev_ref[...])
    pad_ref[HALO:HALO + tc, :] = cur_ref[...]
    pad_ref[HALO + tc:2 * HALO + tc, :] = jnp.where(last, zero, next_ref[...])


def _local_kernel(pp_ref, pp_prev, pp_next, ps_ref, ps_prev, ps_next, ph_ref, ph_prev, ph_next,
                  pw_ref, pscale_ref, sw_ref, hw_ref,
                  ya_ref, yc_ref, u_ref, x0_ref,
                  pad_p, pad_s, pad_h, *, seq_len, tc):
    nchunk = seq_len // tc
    j = pl.program_id(0) % nchunk
    first = j == 0
    last = j == nchunk - 1
    cw = pp_ref.shape[1]
    sw = cw

    _fill_padded(pad_p, pp_prev, pp_ref, pp_next, first, last, tc)
    sh = lambda k: pad_p[HALO + k:HALO + k + tc, :]
    u = pp_ref[...]
    sums = {}
    acc = u
    lo_done, hi_done = 0, 0
    for win in POOL_WINDOWS:
        lo, hi = -(win // 2), win // 2 - 1
        for k in range(lo, lo_done):
            acc = acc + sh(k)
        for k in range(hi_done + 1, hi + 1):
            acc = acc + sh(k)
        lo_done, hi_done = lo, hi
        sums[win] = acc
    t = j * tc + lax.broadcasted_iota(jnp.int32, (tc, 1), 0)
    lane = lax.broadcasted_iota(jnp.int32, (1, cw), 1)
    gw = cw // len(POOL_WINDOWS)
    pooled = None
    for g, win in reversed(list(enumerate(POOL_WINDOWS))):
        cnt = jnp.minimum(t - win // 2 + win, seq_len) - jnp.maximum(t - win // 2, 0)
        val = sums[win] * (1.0 / cnt.astype(F32))
        pooled = val if pooled is None else jnp.where(lane < (g + 1) * gw, val, pooled)
    pooled = pooled - u
    ya = _dot(pooled.astype(BF16), pw_ref[...]) * pscale_ref[...]
    ya_ref[...] = ya.astype(BF16)

    _fill_padded(pad_s, ps_prev, ps_ref, ps_next, first, last, tc)
    w3 = sw_ref[...]
    z = lambda k: (pad_s[HALO + k:HALO + k + tc, 2 * sw:3 * sw] * pad_s[HALO + k:HALO + k + tc, 0:sw])
    conv = w3[0:1, :] * z(-1) + w3[1:2, :] * z(0) + w3[2:3, :] * z(1)
    yc_ref[...] = (ps_ref[:, sw:2 * sw] * conv).astype(BF16)

    _fill_padded(pad_h, ph_prev, ph_ref, ph_next, first, last, tc)
    hw = hw_ref[...]
    c3 = (hw[0:1, :] * pad_h[HALO - 1:HALO - 1 + tc, :] + hw[1:2, :] * pad_h[HALO:HALO + tc, :]
          + hw[2:3, :] * pad_h[HALO + 1:HALO + 1 + tc, :])
    x0_ref[...] = c3[:, 0:sw]
    u_ref[...] = c3[:, sw:2 * sw] * c3[:, 2 * sw:3 * sw]


def _local_mixers(pp, ps, ph, pool_bd, pool_scale, sconv_w, hyena_conv, seq_len, tc):
    m, cw = pp.shape
    nb = m // seq_len
    nchunk = seq_len // tc
    hpc = tc // HALO
    nhb = m // HALO
    row = lambda i: (i, 0)
    prev = lambda i: (jnp.maximum(i * hpc - 1, 0), 0)
    nxt = lambda i: (jnp.minimum((i + 1) * hpc, nhb - 1), 0)
    tl = lambda i: (i % nchunk, i // nchunk)
    const = lambda i: (0, 0)

    def trio(width):
        return [pl.BlockSpec((tc, width), row), pl.BlockSpec((HALO, width), prev),
                pl.BlockSpec((HALO, width), nxt)]

    return pl.pallas_call(
        functools.partial(_local_kernel, seq_len=seq_len, tc=tc),
        grid=(m // tc,),
        in_specs=trio(cw) + trio(3 * cw) + trio(3 * cw)
        + [pl.BlockSpec((cw, cw), const), pl.BlockSpec((1, cw), const),
           pl.BlockSpec((3, cw), const), pl.BlockSpec((3, 3 * cw), const)],
        out_specs=[pl.BlockSpec((tc, cw), row), pl.BlockSpec((tc, cw), row),
                   pl.BlockSpec((tc, cw), tl), pl.BlockSpec((tc, cw), tl)],
        out_shape=[jax.ShapeDtypeStruct((m, cw), BF16), jax.ShapeDtypeStruct((m, cw), BF16),
                   jax.ShapeDtypeStruct((seq_len, nb * cw), F32),
                   jax.ShapeDtypeStruct((seq_len, nb * cw), F32)],
        scratch_shapes=[pltpu.VMEM((tc + 2 * HALO, cw), F32), pltpu.VMEM((tc + 2 * HALO, 3 * cw), F32),
                        pltpu.VMEM((tc + 2 * HALO, 3 * cw), F32)],
        compiler_params=_cparams("parallel"),
        name="local_mixers",
    )(pp, pp, pp, ps, ps, ps, ph, ph, ph, pool_bd, pool_scale, sconv_w, hyena_conv)


def _trig(rows, cols, n):
    split = 64
    r = np.asarray(rows, np.int64)[:, None]
    c = np.asarray(cols, np.int64)
    assert c[0] % split == 0 and len(c) % split == 0 and np.all(np.diff(c) == 1)
    c0 = np.arange(split)[None, :]
    c1 = c[::split][None, :]
    ang0 = ((r * c0) % n) * (2.0 * math.pi / n)
    ang1 = ((r * c1) % n) * (2.0 * math.pi / n)
    tab = lambda a: jnp.asarray(a, F32)
    ca, sa = tab(np.cos(ang0))[:, None, :], tab(np.sin(ang0))[:, None, :]
    cb, sb = tab(np.cos(ang1))[:, :, None], tab(np.sin(ang1))[:, :, None]
    shape = (len(rows), len(c))
    return (ca * cb - sa * sb).reshape(shape), (sa * cb + ca * sb).reshape(shape)


def _dft_matrices(seq_len):
    L = seq_len
    n = 2 * L
    k = np.arange(L)
    cos_f, sin_f = _trig(k, k, n)
    alt = jnp.asarray(np.where(k % 2 == 0, 1.0, -1.0), F32)
    im = (-sin_f).at[0, :].set(alt)
    fwd = jnp.concatenate([cos_f, im], axis=0)
    tp = np.arange(L // 2, L // 2 + L)
    cos_i, sin_i = _trig(tp, k, n)
    alt_t = jnp.asarray(np.where(tp % 2 == 0, 1.0, -1.0), F32)
    inv_c = (cos_i * (2.0 / n)).at[:, 0].set(1.0 / n)
    inv_s = (sin_i * (-2.0 / n)).at[:, 0].set(alt_t / n)
    return fwd, inv_c.astype(BF16), inv_s.astype(BF16)


def _hyena_embedding(seq_len, width):
    t = np.arange(seq_len, dtype=np.float64)
    w = (2.0 * math.pi / seq_len) * t
    bands = np.linspace(1e-4, HYENA_BANDS - 1, HYENA_BANDS)
    z = np.concatenate([(t / (seq_len - 1))[:, None], np.cos(w[:, None] * bands),
                        -np.sin(w[:, None] * bands)], axis=-1)
    out = np.zeros((seq_len, width), np.float32)
    out[:, :z.shape[1]] = z
    return jnp.asarray(out)


def _filter_kernel(z_ref, f1_ref, fb1_ref, f2_ref, fb2_ref, f3_ref, freq_ref, decay_ref, o_ref, *, seq_len):
    hdot = lambda a, b: jnp.dot(a, b, precision=HIGHEST, preferred_element_type=F32)
    fr = freq_ref[...]
    hdn = jnp.sin(fr * (hdot(z_ref[...], f1_ref[...]) + fb1_ref[...]))
    hdn = jnp.sin(fr * (hdot(hdn, f2_ref[...]) + fb2_ref[...]))
    filt = hdot(hdn, f3_ref[...])
    t = lax.broadcasted_iota(jnp.int32, (seq_len, 1), 0)
    dist = jnp.abs(t - seq_len // 2).astype(F32) / (seq_len / 2)
    filt = filt * jnp.exp(-dist * decay_ref[...])
    o_ref[...] = filt / jnp.sum(jnp.abs(filt), axis=0, keepdims=True)


def _pad2(a, rows, cols):
    return jnp.zeros((rows, cols), F32).at[:a.shape[0], :a.shape[1]].set(a)


def _hyena_filter(seq_len, f1, fb1, f2, fb2, f3, freq, decay):
    p = 128
    c = f3.shape[1]
    args = (_hyena_embedding(seq_len, p), _pad2(f1, p, p), _pad2(fb1[None], 1, p), _pad2(f2, p, p),
            _pad2(fb2[None], 1, p), _pad2(f3, p, c), _pad2(freq[None], 1, p), decay[None])
    return pl.pallas_call(
        functools.partial(_filter_kernel, seq_len=seq_len),
        out_shape=jax.ShapeDtypeStruct((seq_len, c), F32),
        compiler_params=pltpu.CompilerParams(vmem_limit_bytes=VMEM_LIMIT_BYTES),
        name="hyena_filter",
    )(*args)


def _spectrum_kernel(f_ref, h_ref, o_ref):
    o_ref[...] = jnp.dot(f_ref[...], h_ref[...], precision=HIGHEST, preferred_element_type=F32)


def _filter_spectrum(fwd32, filt):
    n, L = fwd32.shape
    c = filt.shape[1]
    tf = min(n, 512)
    return pl.pallas_call(
        _spectrum_kernel,
        grid=(n // tf,),
        in_specs=[pl.BlockSpec((tf, L), lambda i: (i, 0)), pl.BlockSpec((L, c), lambda i: (0, 0))],
        out_specs=pl.BlockSpec((tf, c), lambda i: (i, 0)),
        out_shape=jax.ShapeDtypeStruct((n, c), F32),
        compiler_params=_cparams("parallel"),
        name="filter_spectrum",
    )(fwd32, filt)


def _hy_fwd_kernel(fc_ref, fs_ref, u_ref, hr_ref, hi_ref, yr_ref, yi_ref, *, cw):
    ub = u_ref[...].astype(BF16)
    ur = _dot(fc_ref[...], ub)
    ui = _dot(fs_ref[...], ub)
    tf = ur.shape[0]
    row0 = (pl.program_id(0) * tf + lax.broadcasted_iota(jnp.int32, (tf, 1), 0)) == 0
    hr, hi = hr_ref[...], hi_ref[...]
    for s in range(ur.shape[1] // cw):
        a, b = ur[:, s * cw:(s + 1) * cw], ui[:, s * cw:(s + 1) * cw]
        yr = jnp.where(row0, a * hr, a * hr - b * hi)
        yi = jnp.where(row0, b * hi, a * hi + b * hr)
        yr_ref[:, s * cw:(s + 1) * cw] = yr.astype(BF16)
        yi_ref[:, s * cw:(s + 1) * cw] = yi.astype(BF16)


def _hy_inv_kernel(ic_ref, is_ref, yr_ref, yi_ref, u_ref, x0_ref, bias_ref, o_ref):
    conv = _dot(ic_ref[...], yr_ref[...]) + _dot(is_ref[...], yi_ref[...])
    o_ref[...] = ((conv + bias_ref[...] * u_ref[...]) * x0_ref[...]).astype(BF16)


def _hyena_conv(u_t, x0_t, fwd_bf, inv_c, inv_s, spec, bias, cw):
    L, ncol = u_t.shape
    tf = min(L, 1024)
    tn = min(ncol, 512)
    nf = L // tf
    yr, yi = pl.pallas_call(
        functools.partial(_hy_fwd_kernel, cw=cw),
        grid=(nf, ncol // tn),
        in_specs=[pl.BlockSpec((tf, L), lambda i, j: (i, 0)),
                  pl.BlockSpec((tf, L), lambda i, j: (nf + i, 0)),
                  pl.BlockSpec((L, tn), lambda i, j: (0, j)),
                  pl.BlockSpec((tf, cw), lambda i, j: (i, 0)),
                  pl.BlockSpec((tf, cw), lambda i, j: (nf + i, 0))],
        out_specs=[pl.BlockSpec((tf, tn), lambda i, j: (i, j))] * 2,
        out_shape=[jax.ShapeDtypeStruct((L, ncol), BF16)] * 2,
        compiler_params=_cparams("parallel", "parallel"),
        name="hyena_dft",
    )(fwd_bf, fwd_bf, u_t, spec, spec)
    bias_t = jnp.tile(bias[None, :], (1, tn // cw))
    return pl.pallas_call(
        _hy_inv_kernel,
        grid=(nf, ncol // tn),
        in_specs=[pl.BlockSpec((tf, L), lambda i, j: (i, 0)),
                  pl.BlockSpec((tf, L), lambda i, j: (i, 0)),
                  pl.BlockSpec((L, tn), lambda i, j: (0, j)),
                  pl.BlockSpec((L, tn), lambda i, j: (0, j)),
                  pl.BlockSpec((tf, tn), lambda i, j: (i, j)),
                  pl.BlockSpec((tf, tn), lambda i, j: (i, j)),
                  pl.BlockSpec((1, tn), lambda i, j: (0, 0))],
        out_specs=pl.BlockSpec((tf, tn), lambda i, j: (i, j)),
        out_shape=jax.ShapeDtypeStruct((L, ncol), BF16),
        compiler_params=_cparams("parallel", "parallel"),
        name="hyena_idft",
    )(inv_c, inv_s, yr, yi, u_t, x0_t, bias_t)


def _qkt(q, k):
    return lax.dot_general(q, k, (((1,), (1,)), ((), ())), preferred_element_type=F32)


PAIR = 2 * HEAD_DIM


def _attend_pair(q_pair, parts, bias_of=None):
    low = lax.broadcasted_iota(jnp.int32, (1, PAIR), 1) < HEAD_DIM
    one = jnp.ones((), BF16)
    accs = []
    for half in (0, 1):
        mine = low if half == 0 else jnp.logical_not(low)
        qh = jnp.where(mine, q_pair, jnp.zeros((), BF16))
        scores = []
        for i, (k, _) in enumerate(parts):
            s = _qkt(qh, k)
            b = None if bias_of is None else bias_of(half, i)
            scores.append(s if b is None else s + b)
        m = None
        for s in scores:
            mi = jnp.max(s, axis=-1, keepdims=True)
            m = mi if m is None else jnp.maximum(m, mi)
        acc = None
        for s, (_, v) in zip(scores, parts):
            p = jnp.exp((s - m).astype(BF16))
            o = _dot(p, jnp.where(mine, v, one))
            acc = o if acc is None else acc + o
        accs.append(acc)
    num = jnp.where(low, accs[0], accs[1])
    den = pltpu.roll(jnp.where(low, accs[1], accs[0]), HEAD_DIM, axis=1)
    return num * (1.0 / den)


def _ctx_attn_kernel(q_ref, k_ref, v_ref, o_ref):
    for j in range(N_HEADS // 2):
        sl = slice(j * PAIR, (j + 1) * PAIR)
        parts = [(k_ref[:, sl].astype(BF16), v_ref[:, sl].astype(BF16))]
        o_ref[:, sl] = _attend_pair(q_ref[:, sl], parts).astype(BF16)


def _context_attention(q, k, v, seq_len):
    m, w = q.shape
    spec = pl.BlockSpec((seq_len, w), lambda b: (b, 0))
    return pl.pallas_call(
        _ctx_attn_kernel,
        grid=(m // seq_len,),
        in_specs=[spec, spec, spec],
        out_specs=spec,
        out_shape=jax.ShapeDtypeStruct((m, w), BF16),
        compiler_params=_cparams("parallel"),
        name="context_attention",
    )(q, k, v)


LAT_QROWS = 4
LAT_KROWS = NA_ROWS + LAT_QROWS
LAT_TQ = LAT_QROWS * GRID_W


def _lat_attn_kernel(q_ref, k0_ref, k1_ref, k2_ref, v0_ref, v1_ref, v2_ref, ck_ref, cv_ref, g_ref, o_ref,
                     *, rows):
    nblk = rows // LAT_QROWS
    rblk = pl.program_id(0)
    kblk = jnp.clip(rblk - 1, 0, nblk - 3)
    slot = {}
    for ri in range(LAT_QROWS):
        r = rblk * LAT_QROWS + ri
        rs = jnp.clip(r - NA_ROWS // 2, 0, rows - NA_ROWS)
        for kj in range(LAT_KROWS):
            krow = kblk * LAT_QROWS + kj
            in_window = (krow >= rs) & (krow < rs + NA_ROWS)
            slot[ri, kj] = jnp.where(in_window, krow - r + NA_ROWS - 1, 2 * NA_ROWS - 1)
    low_half = lax.broadcasted_iota(jnp.int32, (GRID_W, 2 * GRID_W), 1) < GRID_W

    def bias_block(h, i):
        row_blocks = []
        for ri in range(LAT_QROWS):
            tiles = []
            for kj in range(LAT_QROWS * i, LAT_QROWS * (i + 1), 2):
                tiles.append(jnp.where(low_half, g_ref[h, slot[ri, kj]], g_ref[h, slot[ri, kj + 1]]))
            row_blocks.append(jnp.concatenate(tiles, axis=1))
        return jnp.concatenate(row_blocks, axis=0)

    n_local = 3
    for j in range(N_HEADS // 2):
        sl = slice(j * PAIR, (j + 1) * PAIR)
        parts = [(kr[:, sl].astype(BF16), vr[:, sl].astype(BF16))
                 for kr, vr in ((k0_ref, v0_ref), (k1_ref, v1_ref), (k2_ref, v2_ref), (ck_ref, cv_ref))]
        bias_of = lambda half, i, j=j: bias_block(2 * j + half, i) if i < n_local else None
        o_ref[:, sl] = _attend_pair(q_ref[:, sl], parts, bias_of).astype(BF16)


def _latent_bias(rpb, rows):
    del rows
    nh, ndr, ndc = rpb.shape
    qc = np.arange(GRID_W)[:, None]
    kc = (np.arange(2 * GRID_W) % GRID_W)[None, :]
    ws = np.clip(qc - NA_COLS // 2, 0, GRID_W - NA_COLS)
    col_ok = ((kc >= ws) & (kc < ws + NA_COLS)).reshape(-1)
    dc = np.clip(kc - qc + NA_COLS - 1, 0, ndc - 1).reshape(-1)
    onehot = jnp.asarray((dc[None, :] == np.arange(ndc)[:, None]) & col_ok[None, :], F32)
    g = jnp.einsum('hrd,dx->hrx', rpb, onehot, precision=HIGHEST)
    g = g + jnp.asarray(np.where(col_ok, 0.0, NEG_BIG), F32)
    g = jnp.concatenate([g, jnp.full((nh, 2 * NA_ROWS - ndr, g.shape[-1]), NEG_BIG, F32)], axis=1)
    return g.reshape(nh, 2 * NA_ROWS, GRID_W, 2 * GRID_W)


def _latent_attention(q, k, v, ctx_k, ctx_v, layer, bias, seq_len):
    m, w = q.shape
    rows = seq_len // GRID_W
    nblk = rows // LAT_QROWS
    nb = m // seq_len
    tq = LAT_TQ
    nctx = ctx_k.shape[2]

    def kspec(i):
        return pl.BlockSpec((tq, w), lambda r, b: (b * nblk + jnp.clip(r - 1, 0, nblk - 3) + i, 0))

    qspec = pl.BlockSpec((tq, w), lambda r, b: (b * nblk + r, 0))
    cspec = pl.BlockSpec((None, None, nctx, w), lambda r, b: (b, layer, 0, 0))
    return pl.pallas_call(
        functools.partial(_lat_attn_kernel, rows=rows),
        grid=(nblk, nb),
        in_specs=[qspec, kspec(0), kspec(1), kspec(2), kspec(0), kspec(1), kspec(2), cspec, cspec,
                  _resident(bias.shape)],
        out_specs=qspec,
        out_shape=jax.ShapeDtypeStruct((m, w), BF16),
        compiler_params=_cparams("parallel", "parallel"),
        name="latent_attention",
    )(q, k, k, k, v, v, v, ctx_k, ctx_v, bias)


def _merge_kernel(x_ref, h_ref, ya_ref, yb_ref, yc_ref, yd_ref, mod_ref, nw_ref,
                  wg_ref, bg_ref, wa_ref, wb_ref, wc_ref, wd_ref, wo_ref, wr_ref, br_ref,
                  x1_ref, h2_ref, lg_ref):
    d = x_ref.shape[1]
    h = h_ref[...]
    merged = None
    for i, (y_ref, w_ref) in enumerate(((ya_ref, wa_ref), (yb_ref, wb_ref), (yc_ref, wc_ref), (yd_ref, wd_ref))):
        gate = jax.nn.sigmoid(_dot(h, wg_ref[:, i * d:(i + 1) * d]) + bg_ref[:, i * d:(i + 1) * d])
        term = gate * _dot(y_ref[...], w_ref[...])
        merged = term if merged is None else merged + term
    mod = mod_ref[...]
    x1 = x_ref[...] + mod[2:3, :] * _dot(merged.astype(BF16), wo_ref[...])
    x1_ref[...] = x1
    h2 = _rms_mod(x1, nw_ref[...], mod[3:4, :], mod[4:5, :])
    h2_ref[...] = h2
    h2_hi = h2.astype(BF16)
    h2_lo = (h2 - h2_hi.astype(F32)).astype(BF16)
    lg_ref[...] = (_dot(h2_hi, wr_ref[0]) + _dot(h2_lo, wr_ref[0]) + _dot(h2_hi, wr_ref[1])) + br_ref[...]


def _merge(x, h, ya, yb_t, yc, yd, mod, nw, wts, tm, rows_per_mod, seq_len):
    m, d = x.shape
    cw = ya.shape[1]
    nchunk = seq_len // tm
    row = lambda i: (i, 0)
    const2 = lambda i: (0, 0)
    w_specs = [_resident(w.shape) for w in wts]
    return pl.pallas_call(
        _merge_kernel,
        grid=(m // tm,),
        in_specs=[pl.BlockSpec((tm, d), row), pl.BlockSpec((tm, d), row),
                  pl.BlockSpec((tm, cw), row),
                  pl.BlockSpec((tm, cw), lambda i: (i % nchunk, i // nchunk)),
                  pl.BlockSpec((tm, cw), row),
                  pl.BlockSpec((tm, yd.shape[1]), row),
                  pl.BlockSpec((None, 8, d), lambda i: ((i * tm) // rows_per_mod, 0, 0)),
                  pl.BlockSpec((1, d), const2)] + w_specs,
        out_specs=[pl.BlockSpec((tm, d), row), pl.BlockSpec((tm, d), row),
                   pl.BlockSpec((tm, ROUTE_LANES), row)],
        out_shape=[jax.ShapeDtypeStruct((m, d), F32), jax.ShapeDtypeStruct((m, d), F32),
                   jax.ShapeDtypeStruct((m, ROUTE_LANES), F32)],
        compiler_params=_cparams("parallel"),
        name="merge",
    )(x, h, ya, yb_t, yc, yd, mod, nw, *wts)


ROUTE_TM = 512
EXPERT_TR = 256
MOVE_TM = 512
INPROJ_TM = 1024
R_E1, R_E2, R_RANK1, R_RANK2, R_W1, R_W2 = range(6)


def _route_kernel(lg_ref, route_ref, cnt_ref, carry_ref):
    @pl.when(pl.program_id(0) == 0)
    def _():
        carry_ref[...] = jnp.zeros_like(carry_ref)

    lg = lg_ref[...]
    lane = lax.broadcasted_iota(jnp.int32, lg.shape, 1).astype(F32)
    neg = jnp.float32(-jnp.inf)
    big = jnp.float32(ROUTE_LANES)
    gl = jnp.where(lane < MOE_GROUPS, lg, neg)
    gm = jnp.max(gl, axis=-1, keepdims=True)
    g_prob = 1.0 / jnp.sum(jnp.exp(gl - gm), axis=-1, keepdims=True)
    gidx = jnp.min(jnp.where(gl == gm, lane, big), axis=-1, keepdims=True)
    e0 = EXPERT_LANE0 + gidx * MOE_EXPERTS
    el = jnp.where((lane >= e0) & (lane < e0 + MOE_EXPERTS), lg, neg)
    m1 = jnp.max(el, axis=-1, keepdims=True)
    i1 = jnp.min(jnp.where(el == m1, lane, big), axis=-1, keepdims=True)
    el2 = jnp.where(lane == i1, neg, el)
    m2 = jnp.max(el2, axis=-1, keepdims=True)
    i2 = jnp.min(jnp.where(el2 == m2, lane, big), axis=-1, keepdims=True)
    r = jnp.exp(m2 - m1)
    w1 = 1.0 / (1.0 + r)
    w2 = r * w1
    two_hot = jnp.where((lane == i1) | (lane == i2), 1.0, 0.0)
    tm = lg.shape[0]
    ltri = (lax.broadcasted_iota(jnp.int32, (tm, tm), 1) < lax.broadcasted_iota(jnp.int32, (tm, tm), 0))
    rank = carry_ref[0:1, :] + _dot(ltri.astype(BF16), two_hot.astype(BF16))
    pick = lambda idx: jnp.sum(jnp.where(lane == idx, rank, 0.0), axis=-1, keepdims=True)
    rec = jnp.zeros_like(lg)
    for col, val in ((R_E1, i1), (R_E2, i2), (R_RANK1, pick(i1)), (R_RANK2, pick(i2)),
                     (R_W1, w1 * g_prob), (R_W2, w2 * g_prob)):
        rec = jnp.where(lane == col, val, rec)
    route_ref[...] = rec
    carry_ref[...] = carry_ref[...] + jnp.sum(two_hot, axis=0, keepdims=True)
    cnt_ref[...] = carry_ref[...]


def _route(logits):
    m = logits.shape[0]
    spec = pl.BlockSpec((ROUTE_TM, ROUTE_LANES), lambda i: (i, 0))
    return pl.pallas_call(
        _route_kernel, grid=(m // ROUTE_TM,), in_specs=[spec],
        out_specs=[spec, pl.BlockSpec((8, ROUTE_LANES), lambda i: (0, 0))],
        out_shape=[jax.ShapeDtypeStruct((m, ROUTE_LANES), F32), jax.ShapeDtypeStruct((8, ROUTE_LANES), F32)],
        scratch_shapes=[pltpu.VMEM((8, ROUTE_LANES), F32)],
        compiler_params=_cparams("arbitrary"), name="route",
    )(logits)


def _plan_kernel(cnt_ref, seg_ref, tile_ref):
    cnt = cnt_ref[...]
    lane = lax.broadcasted_iota(jnp.int32, cnt.shape, 1)
    is_e = (lane >= EXPERT_LANE0) & (lane < EXPERT_LANE0 + N_EXPERTS)
    size = jnp.where(is_e, jnp.floor((cnt + (EXPERT_TR - 1)) * (1.0 / EXPERT_TR)) * EXPERT_TR, 0.0)
    upper = (lax.broadcasted_iota(jnp.int32, (ROUTE_LANES, ROUTE_LANES), 0)
             < lax.broadcasted_iota(jnp.int32, (ROUTE_LANES, ROUTE_LANES), 1)).astype(F32)
    start = jnp.dot(size, upper, precision=HIGHEST, preferred_element_type=F32)
    end = start + size
    total = jnp.max(end, axis=-1, keepdims=True)
    row = lax.broadcasted_iota(jnp.int32, cnt.shape, 0)
    seg = jnp.where(row == 0, start, jnp.where(row == 1, size, jnp.where(row == 2, end, total * (1.0 / EXPERT_TR))))
    seg_ref[...] = seg.astype(jnp.int32)
    nt = tile_ref.shape[0]
    t0 = (lax.broadcasted_iota(jnp.int32, (nt, ROUTE_LANES), 0) * EXPERT_TR).astype(F32)
    lane_t = lax.broadcasted_iota(jnp.int32, (nt, ROUTE_LANES), 1)
    done = jnp.where((lane_t >= EXPERT_LANE0) & (lane_t < EXPERT_LANE0 + N_EXPERTS) & (end[0:1, :] <= t0), 1.0, 0.0)
    te = jnp.minimum(jnp.sum(done, axis=-1, keepdims=True), N_EXPERTS - 1.0)
    tile_ref[...] = jnp.broadcast_to(te, (nt, ROUTE_LANES)).astype(jnp.int32)


def _plan(counts, n_tiles):
    nt = -(-n_tiles // 8) * 8
    return pl.pallas_call(
        _plan_kernel,
        out_shape=[jax.ShapeDtypeStruct((8, ROUTE_LANES), jnp.int32),
                   jax.ShapeDtypeStruct((nt, ROUTE_LANES), jnp.int32)],
        name="plan",
    )(counts)


def _positions_kernel(route_ref, seg_ref, pos_ref):
    rec = route_ref[...]
    start = seg_ref[0:1, :].astype(F32)
    lane = lax.broadcasted_iota(jnp.int32, rec.shape, 1).astype(F32)
    seg_start = lambda col: jnp.sum(jnp.where(lane == rec[:, col:col + 1], start, 0.0), axis=-1, keepdims=True)
    p1 = seg_start(R_E1) + rec[:, R_RANK1:R_RANK1 + 1]
    p2 = seg_start(R_E2) + rec[:, R_RANK2:R_RANK2 + 1]
    both = jnp.where(lane == 0.0, p1, jnp.where(lane == 1.0, p2, 0.0))
    pos_ref[...] = jnp.transpose(both)[0:8, :].astype(jnp.int32)


def _positions(route, seg):
    m = route.shape[0]
    return pl.pallas_call(
        _positions_kernel, grid=(m // ROUTE_TM,),
        in_specs=[pl.BlockSpec((ROUTE_TM, ROUTE_LANES), lambda i: (i, 0)),
                  pl.BlockSpec((8, ROUTE_LANES), lambda i: (0, 0))],
        out_specs=pl.BlockSpec((8, ROUTE_TM), lambda i: (0, i)),
        out_shape=jax.ShapeDtypeStruct((8, m), jnp.int32),
        compiler_params=_cparams("parallel"), name="positions",
    )(route, seg)


SEG_START, SEG_SIZE, SEG_END, SEG_TILES = range(4)


def _row_copies(src_of, dst_of, sem):
    copies = [pltpu.make_async_copy(src_of(j, k), dst_of(j, k), sem) for j in range(MOVE_TM) for k in (0, 1)]
    for n, c in enumerate(copies):
        c.start(priority=n % 2)
    for c in copies:
        c.wait()


def _dispatch_kernel(seg_s, pos_s, h2_ref, xs_ref, zbuf, zsem, sem, *, n_tiles):
    @pl.when(pl.program_id(0) == 0)
    def _():
        zbuf[...] = jnp.zeros_like(zbuf)

        def zero_tiles(go):
            for e in range(N_EXPERTS):
                lane = EXPERT_LANE0 + e

                @pl.when(seg_s[SEG_SIZE, lane] > 0)
                def _():
                    start = pl.multiple_of(seg_s[SEG_END, lane] - EXPERT_TR, EXPERT_TR)
                    go(pltpu.make_async_copy(zbuf, xs_ref.at[pl.ds(start, EXPERT_TR)], zsem.at[e]))

                tile = n_tiles - N_EXPERTS + e

                @pl.when(tile >= seg_s[SEG_TILES, 0])
                def _():
                    dst = xs_ref.at[pl.ds(tile * EXPERT_TR, EXPERT_TR)]
                    go(pltpu.make_async_copy(zbuf, dst, zsem.at[N_EXPERTS + e]))

        zero_tiles(lambda c: c.start())
        zero_tiles(lambda c: c.wait())

    _row_copies(lambda j, k: h2_ref.at[pl.ds(j, 1)], lambda j, k: xs_ref.at[pl.ds(pos_s[k, j], 1)], sem)


def _dispatch(h2, pos, seg, n_tiles):
    m, d = h2.shape
    return pl.pallas_call(
        functools.partial(_dispatch_kernel, n_tiles=n_tiles),
        grid_spec=pltpu.PrefetchScalarGridSpec(
            num_scalar_prefetch=1, grid=(m // MOVE_TM,),
            in_specs=[pl.BlockSpec((8, MOVE_TM), lambda i, seg: (0, i), memory_space=pltpu.SMEM),
                      pl.BlockSpec((MOVE_TM, d), lambda i, seg: (i, 0))],
            out_specs=pl.BlockSpec(memory_space=pl.ANY),
            scratch_shapes=[pltpu.VMEM((EXPERT_TR, d), F32), pltpu.SemaphoreType.DMA((2 * N_EXPERTS,)),
                            pltpu.SemaphoreType.DMA]),
        out_shape=jax.ShapeDtypeStruct((n_tiles * EXPERT_TR, d), F32),
        compiler_params=_cparams("arbitrary"), name="dispatch",
    )(seg, pos, h2)


def _experts_kernel(te_s, seg_s, xs_ref, wg_ref, wu_ref, wd_ref, ys_ref):
    used = pl.program_id(0) < seg_s[SEG_TILES, 0]

    @pl.when(used)
    def _():
        x = xs_ref[...].astype(BF16)
        a = _dot(x, wg_ref[...].astype(BF16))
        b = _dot(x, wu_ref[...].astype(BF16))
        ys_ref[...] = _dot((_silu(a) * b).astype(BF16), wd_ref[...].astype(BF16))

    @pl.when(jnp.logical_not(used))
    def _():
        ys_ref[...] = jnp.zeros_like(ys_ref)


def _experts(xs, tile_expert, seg, w_gate, w_up, w_down, layer):
    p, d = xs.shape
    f = w_gate.shape[-1]
    last = lambda j, seg: jnp.minimum(j, seg[SEG_TILES, 0] - 1)
    wmap = lambda j, te, seg: (layer * N_EXPERTS + te[last(j, seg)], 0, 0)
    return pl.pallas_call(
        _experts_kernel,
        grid_spec=pltpu.PrefetchScalarGridSpec(
            num_scalar_prefetch=2, grid=(p // EXPERT_TR,),
            in_specs=[pl.BlockSpec((EXPERT_TR, d), lambda j, te, seg: (last(j, seg), 0)),
                      pl.BlockSpec((None, d, f), wmap), pl.BlockSpec((None, d, f), wmap),
                      pl.BlockSpec((None, f, d), wmap)],
            out_specs=pl.BlockSpec((EXPERT_TR, d), lambda j, te, seg: (j, 0))),
        out_shape=jax.ShapeDtypeStruct((p, d), F32),
        compiler_params=_cparams("arbitrary"), name="experts",
    )(tile_expert, seg, xs, w_gate, w_up, w_down)


def _combine_kernel(pos_s, route_ref, x1_ref, mod_ref, nw_ref, ys_ref, o_ref, y1buf, y2buf, sem, *, final_norm):
    bufs = (y1buf, y2buf)
    _row_copies(lambda j, k: ys_ref.at[pl.ds(pos_s[k, j], 1)], lambda j, k: bufs[k].at[pl.ds(j, 1)], sem)
    rec = route_ref[...]
    moe = rec[:, R_W1:R_W1 + 1] * y1buf[...] + rec[:, R_W2:R_W2 + 1] * y2buf[...]
    x = x1_ref[...] + mod_ref[5:6, :] * moe
    if final_norm:
        x = x * lax.rsqrt(jnp.mean(x * x, axis=-1, keepdims=True) + EPS) * nw_ref[...]
    o_ref[...] = x


def _combine(ys, pos, route, x1, mod, nw_final, rows_per_mod, final_norm):
    m, d = x1.shape
    row = lambda i: (i, 0)
    return pl.pallas_call(
        functools.partial(_combine_kernel, final_norm=final_norm),
        grid=(m // MOVE_TM,),
        in_specs=[pl.BlockSpec((8, MOVE_TM), lambda i: (0, i), memory_space=pltpu.SMEM),
                  pl.BlockSpec((MOVE_TM, ROUTE_LANES), row), pl.BlockSpec((MOVE_TM, d), row),
                  pl.BlockSpec((None, 8, d), lambda i: ((i * MOVE_TM) // rows_per_mod, 0, 0)),
                  pl.BlockSpec((1, d), lambda i: (0, 0)),
                  pl.BlockSpec(memory_space=pl.ANY)],
        out_specs=pl.BlockSpec((MOVE_TM, d), row),
        out_shape=jax.ShapeDtypeStruct((m, d), F32),
        scratch_shapes=[pltpu.VMEM((MOVE_TM, d), F32), pltpu.VMEM((MOVE_TM, d), F32), pltpu.SemaphoreType.DMA],
        compiler_params=_cparams("arbitrary"), name="combine",
    )(pos, route, x1, mod, nw_final, ys)


def _layer_weights(P, l):
    d = P['w_in'].shape[1]
    cw = P['pool_scale'].shape[1]
    hw = P['hyena_conv'].shape[2]
    o1, o2, o3 = cw, cw + hw, cw + hw + 3 * cw
    na = (P['w_in'].shape[2] - o3) // 3
    w_in = P['w_in'][l]
    segs = [w_in[:, o1:o2], w_in[:, o2:o3], w_in[:, o3:o3 + na], w_in[:, o3 + na:o3 + 2 * na],
            w_in[:, o3 + 2 * na:], w_in[:, :o1]]
    widths = tuple(int(s.shape[1]) for s in segs)
    w_in_r = jnp.concatenate(segs, axis=1).astype(BF16)
    gw = cw // len(POOL_WINDOWS)
    pool_bd = jnp.zeros((cw, cw), F32)
    for g in range(len(POOL_WINDOWS)):
        pool_bd = pool_bd.at[g * gw:(g + 1) * gw, g * gw:(g + 1) * gw].set(P['pool_w'][l, g])
    wr = jnp.concatenate([P['w_route_group'][l],
                          jnp.transpose(P['w_route_exp'][l], (1, 0, 2)).reshape(d, N_EXPERTS)], axis=1)
    wr = jnp.zeros((d, ROUTE_LANES), F32).at[:, :wr.shape[1]].set(wr)
    wr_hi = wr.astype(BF16)
    wr_lo = (wr - wr_hi.astype(F32)).astype(BF16)
    br = jnp.concatenate([P['b_route_group'][l], P['b_route_exp'][l].reshape(-1)])
    br = jnp.zeros((1, ROUTE_LANES), F32).at[0, :br.shape[0]].set(br)
    return dict(
        widths=widths, w_in_r=w_in_r, norm_mix=P['norm_mix'][l][None], norm_ffn=P['norm_ffn'][l][None],
        pool_bd=pool_bd.astype(BF16), pool_scale=P['pool_scale'][l][None], sconv_w=P['sconv_w'][l],
        hyena_conv=P['hyena_conv'][l], hyena_bias=P['hyena_bias'][l],
        merge=(P['w_gate'][l].astype(BF16), P['b_gate'][l][None], P['w_br_a'][l].astype(BF16),
               P['w_br_b'][l].astype(BF16), P['w_br_c'][l].astype(BF16), P['w_br_d'][l].astype(BF16),
               P['w_out'][l].astype(BF16), jnp.stack([wr_hi, wr_lo]), br),
    )


def _run_stream(x3, mods, LW, EW, nw_final, hy, attend, depth):
    b, seq_len, d = x3.shape
    m = b * seq_len
    x = x3.reshape(m, d)
    rows_per_mod = m // mods.shape[1]
    tm = min(512, seq_len)
    tc = min(256, seq_len)
    kvs = []
    for l in range(depth):
        W = LW[l]
        mod = mods[l]
        cw = W['pool_scale'].shape[1]
        h, ph, ps, q, k, v, pp = _inproj(x, mod, W['norm_mix'], W['w_in_r'], W['widths'],
                                         min(INPROJ_TM, rows_per_mod), rows_per_mod)
        kvs.append((k, v))
        ya, yc, u_t, x0_t = _local_mixers(pp, ps, ph, W['pool_bd'], W['pool_scale'], W['sconv_w'],
                                          W['hyena_conv'], seq_len, tc)
        yb_t = _hyena_conv(u_t, x0_t, hy['fwd_bf'], hy['inv_c'], hy['inv_s'], hy['spec'][l],
                           W['hyena_bias'], cw)
        yd = attend(l, q, k, v)
        x1, h2, logits = _merge(x, h, ya, yb_t, yc, yd, mod, W['norm_ffn'], W['merge'], tm,
                                rows_per_mod, seq_len)
        route, counts = _route(logits)
        n_tiles = 2 * m // EXPERT_TR + N_EXPERTS
        seg, tile_tab = _plan(counts, n_tiles)
        pos = _positions(route, seg)
        xs = _dispatch(h2, pos, seg, n_tiles)
        ys = _experts(xs, tile_tab[:, 0], seg, EW[0], EW[1], EW[2], l)
        x = _combine(ys, pos, route, x1, mod, nw_final, rows_per_mod, final_norm=(l == depth - 1))
    return x, kvs


def _hyena_setup(seq_len, P, depth):
    fwd32, inv_c, inv_s = _dft_matrices(seq_len)
    spec = []
    for l in range(depth):
        filt = _hyena_filter(seq_len, P['hyena_f1'][l], P['hyena_fb1'][l], P['hyena_f2'][l],
                             P['hyena_fb2'][l], P['hyena_f3'][l], P['hyena_freq'][l], P['hyena_decay'][l])
        spec.append(_filter_spectrum(fwd32, filt))
    return dict(fwd_bf=fwd32.astype(BF16), inv_c=inv_c, inv_s=inv_s, spec=spec)


def kernel(x_prompt, x_sample, cache_k, cache_v, c, c_ctx, w_ada, b_ada, norm_mix, w_in, w_gate, b_gate, pool_w, pool_scale, hyena_conv, hyena_f1, hyena_fb1, hyena_f2, hyena_fb2, hyena_f3, hyena_freq, hyena_decay, hyena_bias, sconv_w, na_rpb, w_br_a, w_br_b, w_br_c, w_br_d, w_out, norm_ffn, w_route_group, b_route_group, w_route_exp, b_route_exp, w_e_gate, w_e_up, w_e_down, norm_final):
    P = dict(w_in=w_in, w_gate=w_gate, b_gate=b_gate, pool_w=pool_w, pool_scale=pool_scale,
             hyena_conv=hyena_conv, hyena_f1=hyena_f1, hyena_fb1=hyena_fb1, hyena_f2=hyena_f2,
             hyena_fb2=hyena_fb2, hyena_f3=hyena_f3, hyena_freq=hyena_freq, hyena_decay=hyena_decay,
             hyena_bias=hyena_bias, sconv_w=sconv_w, w_br_a=w_br_a, w_br_b=w_br_b, w_br_c=w_br_c,
             w_br_d=w_br_d, w_out=w_out, norm_mix=norm_mix, norm_ffn=norm_ffn,
             w_route_group=w_route_group, b_route_group=b_route_group, w_route_exp=w_route_exp,
             b_route_exp=b_route_exp, w_e_gate=w_e_gate, w_e_up=w_e_up, w_e_down=w_e_down)
    depth, d, _ = w_ada.shape
    bp, lp, _ = x_prompt.shape
    bs, ls, _ = x_sample.shape
    assert (ls // GRID_W) % LAT_QROWS == 0 and ls // GRID_W >= LAT_KROWS

    n_c = 1 + bs
    n_pad = -(-n_c // 8) * 8
    cvecs = jnp.zeros((n_pad, d), F32).at[0].set(c_ctx).at[1:n_c].set(c)
    ada = _ada(cvecs, w_ada, b_ada).reshape(depth, n_pad, 6, d)
    ada = jnp.concatenate([ada, jnp.zeros((depth, n_pad, 2, d), F32)], axis=2)

    LW = [_layer_weights(P, l) for l in range(depth)]
    nw_final = norm_final[None]
    f = w_e_gate.shape[-1]
    EW = (w_e_gate.reshape(depth * N_EXPERTS, d, f), w_e_up.reshape(depth * N_EXPERTS, d, f),
          w_e_down.reshape(depth * N_EXPERTS, f, d))

    hy_p = _hyena_setup(lp, P, depth)
    xp, kv_p = _run_stream(x_prompt, ada[:, 0:1], LW, EW, nw_final, hy_p,
                           lambda l, q, k, v: _context_attention(q, k, v, lp), depth)
    y_prompt = xp.reshape(bp, lp, d)
    new_k = jnp.stack([k.reshape(bp, lp, N_HEADS, HEAD_DIM) for k, _ in kv_p], axis=1)
    new_v = jnp.stack([v.reshape(bp, lp, N_HEADS, HEAD_DIM) for _, v in kv_p], axis=1)

    hy_s = _hyena_setup(ls, P, depth)
    past = cache_k.shape[2]
    ck = cache_k.reshape(bs, depth, past, N_HEADS * HEAD_DIM)
    cv = cache_v.reshape(bs, depth, past, N_HEADS * HEAD_DIM)
    biases = [_latent_bias(na_rpb[l], ls // GRID_W) for l in range(depth)]
    xs, _ = _run_stream(x_sample, ada[:, 1:n_c], LW, EW, nw_final, hy_s,
                        lambda l, q, k, v: _latent_attention(q, k, v, ck, cv, l, biases[l], ls),
                        depth)
    y_sample = xs.reshape(bs, ls, d)
    return (y_prompt, y_sample, new_k, new_v)
```

```python
import functools
import math

import numpy as np
import jax
import jax.numpy as jnp
from jax import lax
from jax.experimental import pallas as pl
from jax.experimental.pallas import tpu as pltpu

F32 = jnp.float32
BF16 = jnp.bfloat16
HIGHEST = lax.Precision.HIGHEST

EPS = 1e-6
GRID_W = 64
NA_ROWS = 8
NA_COLS = 16
N_HEADS = 8
HEAD_DIM = 64
POOL_WINDOWS = (2, 4, 8, 16)
HYENA_BANDS = 16
MOE_GROUPS = 4
MOE_EXPERTS = 8
N_EXPERTS = MOE_GROUPS * MOE_EXPERTS
ROUTE_LANES = 128
EXPERT_LANE0 = MOE_GROUPS
HALO = 8
NEG_BIG = -1e30
VMEM_LIMIT_BYTES = 48 * 1024 * 1024


def _cparams(*sem):
    return pltpu.CompilerParams(dimension_semantics=sem, vmem_limit_bytes=VMEM_LIMIT_BYTES)


def _resident(shape):
    nd = len(shape)
    return pl.BlockSpec(shape, lambda *_: (0,) * nd, pipeline_mode=pl.Buffered(1))


def _dot(a, b):
    return jnp.dot(a, b, preferred_element_type=F32)


def _silu(x):
    return x * jax.nn.sigmoid(x)


def _ada_kernel(cv_ref, w_ref, b_ref, o_ref):
    o_ref[...] = jnp.dot(_silu(cv_ref[...]), w_ref[...], precision=HIGHEST,
                         preferred_element_type=F32) + b_ref[...]


def _ada(cvecs, w_ada, b_ada):
    depth, d, d6 = w_ada.shape
    r = cvecs.shape[0]
    return pl.pallas_call(
        _ada_kernel,
        grid=(depth, d6 // d),
        in_specs=[pl.BlockSpec((r, d), lambda l, j: (0, 0)),
                  pl.BlockSpec((None, d, d), lambda l, j: (l, 0, j)),
                  pl.BlockSpec((None, 1, d), lambda l, j: (l, 0, j))],
        out_specs=pl.BlockSpec((None, r, d), lambda l, j: (l, 0, j)),
        out_shape=jax.ShapeDtypeStruct((depth, r, d6), F32),
        compiler_params=_cparams("parallel", "parallel"),
        name="ada",
    )(cvecs, w_ada, b_ada.reshape(depth, 1, d6))


def _rms_mod(x, nw, shift, scale):
    y = x * lax.rsqrt(jnp.mean(x * x, axis=-1, keepdims=True) + EPS) * nw
    return y * (1.0 + scale) + shift


def _inproj_kernel(x_ref, mod_ref, nw_ref, w_ref, h_ref, ph_ref, ps_ref, q_ref, k_ref, v_ref, pp_ref,
                   *, widths):
    mod = mod_ref[...]
    h = _rms_mod(x_ref[...], nw_ref[...], mod[0:1, :], mod[1:2, :]).astype(BF16)
    h_ref[...] = h
    off = 0
    for ref, wd in zip((ph_ref, ps_ref, q_ref, k_ref, v_ref, pp_ref), widths):
        r = _dot(h, w_ref[:, off:off + wd])
        if ref is q_ref:
            r = r * (HEAD_DIM ** -0.5)
        ref[...] = r.astype(ref.dtype)
        off += wd


def _inproj(x, mod, nw, w_in_r, widths, tm, rows_per_mod):
    m, d = x.shape
    n = w_in_r.shape[1]
    dts = (F32, F32, BF16, F32, F32, F32)
    row = lambda i: (i, 0)
    return pl.pallas_call(
        functools.partial(_inproj_kernel, widths=widths),
        grid=(m // tm,),
        in_specs=[pl.BlockSpec((tm, d), row),
                  pl.BlockSpec((None, 8, d), lambda i: ((i * tm) // rows_per_mod, 0, 0)),
                  pl.BlockSpec((1, d), lambda i: (0, 0)),
                  _resident((d, n))],
        out_specs=[pl.BlockSpec((tm, d), row)] + [pl.BlockSpec((tm, wd), row) for wd in widths],
        out_shape=[jax.ShapeDtypeStruct((m, d), BF16)]
        + [jax.ShapeDtypeStruct((m, wd), dt) for wd, dt in zip(widths, dts)],
        compiler_params=_cparams("parallel"),
        name="inproj",
    )(x, mod, nw, w_in_r)


def _fill_padded(pad_ref, prev_ref, cur_ref, next_ref, first, last, tc):
    zero = jnp.zeros((HALO, cur_ref.shape[1]), F32)
    pad_ref[0:HALO, :] = jnp.where(first, zero, prev_ref[...])
    pad_ref[HALO:HALO + tc, :] = cur_ref[...]
    pad_ref[HALO + tc:2 * HALO + tc, :] = jnp.where(last, zero, next_ref[...])


def _local_kernel(pp_ref, pp_prev, pp_next, ps_ref, ps_prev, ps_next, ph_ref, ph_prev, ph_next,
                  pw_ref, pscale_ref, sw_ref, hw_ref,
                  ya_ref, yc_ref, u_ref, x0_ref,
                  pad_p, pad_s, pad_h, *, seq_len, tc):
    nchunk = seq_len // tc
    j = pl.program_id(0) % nchunk
    first = j == 0
    last = j == nchunk - 1
    cw = pp_ref.shape[1]
    sw = cw

    _fill_padded(pad_p, pp_prev, pp_ref, pp_next, first, last, tc)
    sh = lambda k: pad_p[HALO + k:HALO + k + tc, :]
    u = pp_ref[...]
    sums = {}
    acc = u
    lo_done, hi_done = 0, 0
    for win in POOL_WINDOWS:
        lo, hi = -(win // 2), win // 2 - 1
        for k in range(lo, lo_done):
            acc = acc + sh(k)
        for k in range(hi_done + 1, hi + 1):
            acc = acc + sh(k)
        lo_done, hi_done = lo, hi
        sums[win] = acc
    t = j * tc + lax.broadcasted_iota(jnp.int32, (tc, 1), 0)
    lane = lax.broadcasted_iota(jnp.int32, (1, cw), 1)
    gw = cw // len(POOL_WINDOWS)
    pooled = None
    for g, win in reversed(list(enumerate(POOL_WINDOWS))):
        cnt = jnp.minimum(t - win // 2 + win, seq_len) - jnp.maximum(t - win // 2, 0)
        val = sums[win] * (1.0 / cnt.astype(F32))
        pooled = val if pooled is None else jnp.where(lane < (g + 1) * gw, val, pooled)
    pooled = pooled - u
    ya = _dot(pooled.astype(BF16), pw_ref[...]) * pscale_ref[...]
    ya_ref[...] = ya.astype(BF16)

    _fill_padded(pad_s, ps_prev, ps_ref, ps_next, first, last, tc)
    w3 = sw_ref[...]
    z = lambda k: (pad_s[HALO + k:HALO + k + tc, 2 * sw:3 * sw] * pad_s[HALO + k:HALO + k + tc, 0:sw])
    conv = w3[0:1, :] * z(-1) + w3[1:2, :] * z(0) + w3[2:3, :] * z(1)
    yc_ref[...] = (ps_ref[:, sw:2 * sw] * conv).astype(BF16)

    _fill_padded(pad_h, ph_prev, ph_ref, ph_next, first, last, tc)
    hw = hw_ref[...]
    c3 = (hw[0:1, :] * pad_h[HALO - 1:HALO - 1 + tc, :] + hw[1:2, :] * pad_h[HALO:HALO + tc, :]
          + hw[2:3, :] * pad_h[HALO + 1:HALO + 1 + tc, :])
    x0_ref[...] = c3[:, 0:sw]
    u_ref[...] = c3[:, sw:2 * sw] * c3[:, 2 * sw:3 * sw]


def _local_mixers(pp, ps, ph, pool_bd, pool_scale, sconv_w, hyena_conv, seq_len, tc):
    m, cw = pp.shape
    nb = m // seq_len
    nchunk = seq_len // tc
    hpc = tc // HALO
    nhb = m // HALO
    row = lambda i: (i, 0)
    prev = lambda i: (jnp.maximum(i * hpc - 1, 0), 0)
    nxt = lambda i: (jnp.minimum((i + 1) * hpc, nhb - 1), 0)
    tl = lambda i: (i % nchunk, i // nchunk)
    const = lambda i: (0, 0)

    def trio(width):
        return [pl.BlockSpec((tc, width), row), pl.BlockSpec((HALO, width), prev),
                pl.BlockSpec((HALO, width), nxt)]

    return pl.pallas_call(
        functools.partial(_local_kernel, seq_len=seq_len, tc=tc),
        grid=(m // tc,),
        in_specs=trio(cw) + trio(3 * cw) + trio(3 * cw)
        + [pl.BlockSpec((cw, cw), const), pl.BlockSpec((1, cw), const),
           pl.BlockSpec((3, cw), const), pl.BlockSpec((3, 3 * cw), const)],
        out_specs=[pl.BlockSpec((tc, cw), row), pl.BlockSpec((tc, cw), row),
                   pl.BlockSpec((tc, cw), tl), pl.BlockSpec((tc, cw), tl)],
        out_shape=[jax.ShapeDtypeStruct((m, cw), BF16), jax.ShapeDtypeStruct((m, cw), BF16),
                   jax.ShapeDtypeStruct((seq_len, nb * cw), F32),
                   jax.ShapeDtypeStruct((seq_len, nb * cw), F32)],
        scratch_shapes=[pltpu.VMEM((tc + 2 * HALO, cw), F32), pltpu.VMEM((tc + 2 * HALO, 3 * cw), F32),
                        pltpu.VMEM((tc + 2 * HALO, 3 * cw), F32)],
        compiler_params=_cparams("parallel"),
        name="local_mixers",
    )(pp, pp, pp, ps, ps, ps, ph, ph, ph, pool_bd, pool_scale, sconv_w, hyena_conv)


def _trig(rows, cols, n):
    split = 64
    r = np.asarray(rows, np.int64)[:, None]
    c = np.asarray(cols, np.int64)
    assert c[0] % split == 0 and len(c) % split == 0 and np.all(np.diff(c) == 1)
    c0 = np.arange(split)[None, :]
    c1 = c[::split][None, :]
    ang0 = ((r * c0) % n) * (2.0 * math.pi / n)
    ang1 = ((r * c1) % n) * (2.0 * math.pi / n)
    tab = lambda a: jnp.asarray(a, F32)
    ca, sa = tab(np.cos(ang0))[:, None, :], tab(np.sin(ang0))[:, None, :]
    cb, sb = tab(np.cos(ang1))[:, :, None], tab(np.sin(ang1))[:, :, None]
    shape = (len(rows), len(c))
    return (ca * cb - sa * sb).reshape(shape), (sa * cb + ca * sb).reshape(shape)


def _dft_matrices(seq_len):
    L = seq_len
    n = 2 * L
    k = np.arange(L)
    cos_f, sin_f = _trig(k, k, n)
    alt = jnp.asarray(np.where(k % 2 == 0, 1.0, -1.0), F32)
    im = (-sin_f).at[0, :].set(alt)
    fwd = jnp.concatenate([cos_f, im], axis=0)
    tp = np.arange(L // 2, L // 2 + L)
    cos_i, sin_i = _trig(tp, k, n)
    alt_t = jnp.asarray(np.where(tp % 2 == 0, 1.0, -1.0), F32)
    inv_c = (cos_i * (2.0 / n)).at[:, 0].set(1.0 / n)
    inv_s = (sin_i * (-2.0 / n)).at[:, 0].set(alt_t / n)
    return fwd, inv_c.astype(BF16), inv_s.astype(BF16)


def _hyena_embedding(seq_len, width):
    t = np.arange(seq_len, dtype=np.float64)
    w = (2.0 * math.pi / seq_len) * t
    bands = np.linspace(1e-4, HYENA_BANDS - 1, HYENA_BANDS)
    z = np.concatenate([(t / (seq_len - 1))[:, None], np.cos(w[:, None] * bands),
                        -np.sin(w[:, None] * bands)], axis=-1)
    out = np.zeros((seq_len, width), np.float32)
    out[:, :z.shape[1]] = z
    return jnp.asarray(out)


def _filter_kernel(z_ref, f1_ref, fb1_ref, f2_ref, fb2_ref, f3_ref, freq_ref, decay_ref, o_ref, *, seq_len):
    hdot = lambda a, b: jnp.dot(a, b, precision=HIGHEST, preferred_element_type=F32)
    fr = freq_ref[...]
    hdn = jnp.sin(fr * (hdot(z_ref[...], f1_ref[...]) + fb1_ref[...]))
    hdn = jnp.sin(fr * (hdot(hdn, f2_ref[...]) + fb2_ref[...]))
    filt = hdot(hdn, f3_ref[...])
    t = lax.broadcasted_iota(jnp.int32, (seq_len, 1), 0)
    dist = jnp.abs(t - seq_len // 2).astype(F32) / (seq_len / 2)
    filt = filt * jnp.exp(-dist * decay_ref[...])
    o_ref[...] = filt / jnp.sum(jnp.abs(filt), axis=0, keepdims=True)


def _pad2(a, rows, cols):
    return jnp.zeros((rows, cols), F32).at[:a.shape[0], :a.shape[1]].set(a)


def _hyena_filter(seq_len, f1, fb1, f2, fb2, f3, freq, decay):
    p = 128
    c = f3.shape[1]
    args = (_hyena_embedding(seq_len, p), _pad2(f1, p, p), _pad2(fb1[None], 1, p), _pad2(f2, p, p),
            _pad2(fb2[None], 1, p), _pad2(f3, p, c), _pad2(freq[None], 1, p), decay[None])
    return pl.pallas_call(
        functools.partial(_filter_kernel, seq_len=seq_len),
        out_shape=jax.ShapeDtypeStruct((seq_len, c), F32),
        compiler_params=pltpu.CompilerParams(vmem_limit_bytes=VMEM_LIMIT_BYTES),
        name="hyena_filter",
    )(*args)


def _spectrum_kernel(f_ref, h_ref, o_ref):
    o_ref[...] = jnp.dot(f_ref[...], h_ref[...], precision=HIGHEST, preferred_element_type=F32)


def _filter_spectrum(fwd32, filt):
    n, L = fwd32.shape
    c = filt.shape[1]
    tf = min(n, 512)
    return pl.pallas_call(
        _spectrum_kernel,
        grid=(n // tf,),
        in_specs=[pl.BlockSpec((tf, L), lambda i: (i, 0)), pl.BlockSpec((L, c), lambda i: (0, 0))],
        out_specs=pl.BlockSpec((tf, c), lambda i: (i, 0)),
        out_shape=jax.ShapeDtypeStruct((n, c), F32),
        compiler_params=_cparams("parallel"),
        name="filter_spectrum",
    )(fwd32, filt)


def _hy_fwd_kernel(fc_ref, fs_ref, u_ref, hr_ref, hi_ref, yr_ref, yi_ref, *, cw):
    ub = u_ref[...].astype(BF16)
    ur = _dot(fc_ref[...], ub)
    ui = _dot(fs_ref[...], ub)
    tf = ur.shape[0]
    row0 = (pl.program_id(0) * tf + lax.broadcasted_iota(jnp.int32, (tf, 1), 0)) == 0
    hr, hi = hr_ref[...], hi_ref[...]
    for s in range(ur.shape[1] // cw):
        a, b = ur[:, s * cw:(s + 1) * cw], ui[:, s * cw:(s + 1) * cw]
        yr = jnp.where(row0, a * hr, a * hr - b * hi)
        yi = jnp.where(row0, b * hi, a * hi + b * hr)
        yr_ref[:, s * cw:(s + 1) * cw] = yr.astype(BF16)
        yi_ref[:, s * cw:(s + 1) * cw] = yi.astype(BF16)


def _hy_inv_kernel(ic_ref, is_ref, yr_ref, yi_ref, u_ref, x0_ref, bias_ref, o_ref):
    conv = _dot(ic_ref[...], yr_ref[...]) + _dot(is_ref[...], yi_ref[...])
    o_ref[...] = ((conv + bias_ref[...] * u_ref[...]) * x0_ref[...]).astype(BF16)


def _hyena_conv(u_t, x0_t, fwd_bf, inv_c, inv_s, spec, bias, cw):
    L, ncol = u_t.shape
    tf = min(L, 1024)
    tn = min(ncol, 512)
    nf = L // tf
    yr, yi = pl.pallas_call(
        functools.partial(_hy_fwd_kernel, cw=cw),
        grid=(nf, ncol // tn),
        in_specs=[pl.BlockSpec((tf, L), lambda i, j: (i, 0)),
                  pl.BlockSpec((tf, L), lambda i, j: (nf + i, 0)),
                  pl.BlockSpec((L, tn), lambda i, j: (0, j)),
                  pl.BlockSpec((tf, cw), lambda i, j: (i, 0)),
                  pl.BlockSpec((tf, cw), lambda i, j: (nf + i, 0))],
        out_specs=[pl.BlockSpec((tf, tn), lambda i, j: (i, j))] * 2,
        out_shape=[jax.ShapeDtypeStruct((L, ncol), BF16)] * 2,
        compiler_params=_cparams("parallel", "parallel"),
        name="hyena_dft",
    )(fwd_bf, fwd_bf, u_t, spec, spec)
    bias_t = jnp.tile(bias[None, :], (1, tn // cw))
    return pl.pallas_call(
        _hy_inv_kernel,
        grid=(nf, ncol // tn),
        in_specs=[pl.BlockSpec((tf, L), lambda i, j: (i, 0)),
                  pl.BlockSpec((tf, L), lambda i, j: (i, 0)),
                  pl.BlockSpec((L, tn), lambda i, j: (0, j)),
                  pl.BlockSpec((L, tn), lambda i, j: (0, j)),
                  pl.BlockSpec((tf, tn), lambda i, j: (i, j)),
                  pl.BlockSpec((tf, tn), lambda i, j: (i, j)),
                  pl.BlockSpec((1, tn), lambda i, j: (0, 0))],
        out_specs=pl.BlockSpec((tf, tn), lambda i, j: (i, j)),
        out_shape=jax.ShapeDtypeStruct((L, ncol), BF16),
        compiler_params=_cparams("parallel", "parallel"),
        name="hyena_idft",
    )(inv_c, inv_s, yr, yi, u_t, x0_t, bias_t)


def _qkt(q, k):
    return lax.dot_general(q, k, (((1,), (1,)), ((), ())), preferred_element_type=F32)


PAIR = 2 * HEAD_DIM


def _attend_pair(q_pair, parts, bias_of=None):
    low = lax.broadcasted_iota(jnp.int32, (1, PAIR), 1) < HEAD_DIM
    one = jnp.ones((), BF16)
    accs = []
    for half in (0, 1):
        mine = low if half == 0 else jnp.logical_not(low)
        qh = jnp.where(mine, q_pair, jnp.zeros((), BF16))
        scores = []
        for i, (k, _) in enumerate(parts):
            s = _qkt(qh, k)
            b = None if bias_of is None else bias_of(half, i)
            scores.append(s if b is None else s + b)
        m = None
        for s in scores:
            mi = jnp.max(s, axis=-1, keepdims=True)
            m = mi if m is None else jnp.maximum(m, mi)
        acc = None
        for s, (_, v) in zip(scores, parts):
            p = jnp.exp((s - m).astype(BF16))
            o = _dot(p, jnp.where(mine, v, one))
            acc = o if acc is None else acc + o
        accs.append(acc)
    num = jnp.where(low, accs[0], accs[1])
    den = pltpu.roll(jnp.where(low, accs[1], accs[0]), HEAD_DIM, axis=1)
    return num * (1.0 / den)


def _ctx_attn_kernel(q_ref, k_ref, v_ref, o_ref):
    for j in range(N_HEADS // 2):
        sl = slice(j * PAIR, (j + 1) * PAIR)
        parts = [(k_ref[:, sl].astype(BF16), v_ref[:, sl].astype(BF16))]
        o_ref[:, sl] = _attend_pair(q_ref[:, sl], parts).astype(BF16)


def _context_attention(q, k, v, seq_len):
    m, w = q.shape
    spec = pl.BlockSpec((seq_len, w), lambda b: (b, 0))
    return pl.pallas_call(
        _ctx_attn_kernel,
        grid=(m // seq_len,),
        in_specs=[spec, spec, spec],
        out_specs=spec,
        out_shape=jax.ShapeDtypeStruct((m, w), BF16),
        compiler_params=_cparams("parallel"),
        name="context_attention",
    )(q, k, v)


LAT_QROWS = 4
LAT_KROWS = NA_ROWS + LAT_QROWS
LAT_TQ = LAT_QROWS * GRID_W


def _lat_attn_kernel(q_ref, k0_ref, k1_ref, k2_ref, v0_ref, v1_ref, v2_ref, ck_ref, cv_ref, g_ref, o_ref,
                     *, rows):
    nblk = rows // LAT_QROWS
    rblk = pl.program_id(0)
    kblk = jnp.clip(rblk - 1, 0, nblk - 3)
    slot = {}
    for ri in range(LAT_QROWS):
        r = rblk * LAT_QROWS + ri
        rs = jnp.clip(r - NA_ROWS // 2, 0, rows - NA_ROWS)
        for kj in range(LAT_KROWS):
            krow = kblk * LAT_QROWS + kj
            in_window = (krow >= rs) & (krow < rs + NA_ROWS)
            slot[ri, kj] = jnp.where(in_window, krow - r + NA_ROWS - 1, 2 * NA_ROWS - 1)
    low_half = lax.broadcasted_iota(jnp.int32, (GRID_W, 2 * GRID_W), 1) < GRID_W

    def bias_block(h, i):
        row_blocks = []
        for ri in range(LAT_QROWS):
            tiles = []
            for kj in range(LAT_QROWS * i, LAT_QROWS * (i + 1), 2):
                tiles.append(jnp.where(low_half, g_ref[h, slot[ri, kj]], g_ref[h, slot[ri, kj + 1]]))
            row_blocks.append(jnp.concatenate(tiles, axis=1))
        return jnp.concatenate(row_blocks, axis=0)

    n_local = 3
    for j in range(N_HEADS // 2):
        sl = slice(j * PAIR, (j + 1) * PAIR)
        parts = [(kr[:, sl].astype(BF16), vr[:, sl].astype(BF16))
                 for kr, vr in ((k0_ref, v0_ref), (k1_ref, v1_ref), (k2_ref, v2_ref), (ck_ref, cv_ref))]
        bias_of = lambda half, i, j=j: bias_block(2 * j + half, i) if i < n_local else None
        o_ref[:, sl] = _attend_pair(q_ref[:, sl], parts, bias_of).astype(BF16)


def _latent_bias(rpb, rows):
    del rows
    nh, ndr, ndc = rpb.shape
    qc = np.arange(GRID_W)[:, None]
    kc = (np.arange(2 * GRID_W) % GRID_W)[None, :]
    ws = np.clip(qc - NA_COLS // 2, 0, GRID_W - NA_COLS)
    col_ok = ((kc >= ws) & (kc < ws + NA_COLS)).reshape(-1)
    dc = np.clip(kc - qc + NA_COLS - 1, 0, ndc - 1).reshape(-1)
    onehot = jnp.asarray((dc[None, :] == np.arange(ndc)[:, None]) & col_ok[None, :], F32)
    g = jnp.einsum('hrd,dx->hrx', rpb, onehot, precision=HIGHEST)
    g = g + jnp.asarray(np.where(col_ok, 0.0, NEG_BIG), F32)
    g = jnp.concatenate([g, jnp.full((nh, 2 * NA_ROWS - ndr, g.shape[-1]), NEG_BIG, F32)], axis=1)
    return g.reshape(nh, 2 * NA_ROWS, GRID_W, 2 * GRID_W)


def _latent_attention(q, k, v, ctx_k, ctx_v, layer, bias, seq_len):
    m, w = q.shape
    rows = seq_len // GRID_W
    nblk = rows // LAT_QROWS
    nb = m // seq_len
    tq = LAT_TQ
    nctx = ctx_k.shape[2]

    def kspec(i):
        return pl.BlockSpec((tq, w), lambda r, b: (b * nblk + jnp.clip(r - 1, 0, nblk - 3) + i, 0))

    qspec = pl.BlockSpec((tq, w), lambda r, b: (b * nblk + r, 0))
    cspec = pl.BlockSpec((None, None, nctx, w), lambda r, b: (b, layer, 0, 0))
    return pl.pallas_call(
        functools.partial(_lat_attn_kernel, rows=rows),
        grid=(nblk, nb),
        in_specs=[qspec, kspec(0), kspec(1), kspec(2), kspec(0), kspec(1), kspec(2), cspec, cspec,
                  _resident(bias.shape)],
        out_specs=qspec,
        out_shape=jax.ShapeDtypeStruct((m, w), BF16),
        compiler_params=_cparams("parallel", "parallel"),
        name="latent_attention",
    )(q, k, k, k, v, v, v, ctx_k, ctx_v, bias)


def _merge_kernel(x_ref, h_ref, ya_ref, yb_ref, yc_ref, yd_ref, mod_ref, nw_ref,
                  wg_ref, bg_ref, wa_ref, wb_ref, wc_ref, wd_ref, wo_ref, wr_ref, br_ref,
                  x1_ref, h2_ref, lg_ref):
    d = x_ref.shape[1]
    h = h_ref[...]
    merged = None
    for i, (y_ref, w_ref) in enumerate(((ya_ref, wa_ref), (yb_ref, wb_ref), (yc_ref, wc_ref), (yd_ref, wd_ref))):
        gate = jax.nn.sigmoid(_dot(h, wg_ref[:, i * d:(i + 1) * d]) + bg_ref[:, i * d:(i + 1) * d])
        term = gate * _dot(y_ref[...], w_ref[...])
        merged = term if merged is None else merged + term
    mod = mod_ref[...]
    x1 = x_ref[...] + mod[2:3, :] * _dot(merged.astype(BF16), wo_ref[...])
    x1_ref[...] = x1
    h2 = _rms_mod(x1, nw_ref[...], mod[3:4, :], mod[4:5, :])
    h2_ref[...] = h2
    h2_hi = h2.astype(BF16)
    h2_lo = (h2 - h2_hi.astype(F32)).astype(BF16)
    lg_ref[...] = (_dot(h2_hi, wr_ref[0]) + _dot(h2_lo, wr_ref[0]) + _dot(h2_hi, wr_ref[1])) + br_ref[...]


def _merge(x, h, ya, yb_t, yc, yd, mod, nw, wts, tm, rows_per_mod, seq_len):
    m, d = x.shape
    cw = ya.shape[1]
    nchunk = seq_len // tm
    row = lambda i: (i, 0)
    const2 = lambda i: (0, 0)
    w_specs = [_resident(w.shape) for w in wts]
    return pl.pallas_call(
        _merge_kernel,
        grid=(m // tm,),
        in_specs=[pl.BlockSpec((tm, d), row), pl.BlockSpec((tm, d), row),
                  pl.BlockSpec((tm, cw), row),
                  pl.BlockSpec((tm, cw), lambda i: (i % nchunk, i // nchunk)),
                  pl.BlockSpec((tm, cw), row),
                  pl.BlockSpec((tm, yd.shape[1]), row),
                  pl.BlockSpec((None, 8, d), lambda i: ((i * tm) // rows_per_mod, 0, 0)),
                  pl.BlockSpec((1, d), const2)] + w_specs,
        out_specs=[pl.BlockSpec((tm, d), row), pl.BlockSpec((tm, d), row),
                   pl.BlockSpec((tm, ROUTE_LANES), row)],
        out_shape=[jax.ShapeDtypeStruct((m, d), F32), jax.ShapeDtypeStruct((m, d), F32),
                   jax.ShapeDtypeStruct((m, ROUTE_LANES), F32)],
        compiler_params=_cparams("parallel"),
        name="merge",
    )(x, h, ya, yb_t, yc, yd, mod, nw, *wts)


ROUTE_TM = 512
EXPERT_TR = 256
MOVE_TM = 512
INPROJ_TM = 1024
R_E1, R_E2, R_RANK1, R_RANK2, R_W1, R_W2 = range(6)


def _route_kernel(lg_ref, route_ref, cnt_ref, carry_ref):
    @pl.when(pl.program_id(0) == 0)
    def _():
        carry_ref[...] = jnp.zeros_like(carry_ref)

    lg = lg_ref[...]
    lane = lax.broadcasted_iota(jnp.int32, lg.shape, 1).astype(F32)
    neg = jnp.float32(-jnp.inf)
    big = jnp.float32(ROUTE_LANES)
    gl = jnp.where(lane < MOE_GROUPS, lg, neg)
    gm = jnp.max(gl, axis=-1, keepdims=True)
    g_prob = 1.0 / jnp.sum(jnp.exp(gl - gm), axis=-1, keepdims=True)
    gidx = jnp.min(jnp.where(gl == gm, lane, big), axis=-1, keepdims=True)
    e0 = EXPERT_LANE0 + gidx * MOE_EXPERTS
    el = jnp.where((lane >= e0) & (lane < e0 + MOE_EXPERTS), lg, neg)
    m1 = jnp.max(el, axis=-1, keepdims=True)
    i1 = jnp.min(jnp.where(el == m1, lane, big), axis=-1, keepdims=True)
    el2 = jnp.where(lane == i1, neg, el)
    m2 = jnp.max(el2, axis=-1, keepdims=True)
    i2 = jnp.min(jnp.where(el2 == m2, lane, big), axis=-1, keepdims=True)
    r = jnp.exp(m2 - m1)
    w1 = 1.0 / (1.0 + r)
    w2 = r * w1
    two_hot = jnp.where((lane == i1) | (lane == i2), 1.0, 0.0)
    tm = lg.shape[0]
    ltri = (lax.broadcasted_iota(jnp.int32, (tm, tm), 1) < lax.broadcasted_iota(jnp.int32, (tm, tm), 0))
    rank = carry_ref[0:1, :] + _dot(ltri.astype(BF16), two_hot.astype(BF16))
    pick = lambda idx: jnp.sum(jnp.where(lane == idx, rank, 0.0), axis=-1, keepdims=True)
    rec = jnp.zeros_like(lg)
    for col, val in ((R_E1, i1), (R_E2, i2), (R_RANK1, pick(i1)), (R_RANK2, pick(i2)),
                     (R_W1, w1 * g_prob), (R_W2, w2 * g_prob)):
        rec = jnp.where(lane == col, val, rec)
    route_ref[...] = rec
    carry_ref[...] = carry_ref[...] + jnp.sum(two_hot, axis=0, keepdims=True)
    cnt_ref[...] = carry_ref[...]


def _route(logits):
    m = logits.shape[0]
    spec = pl.BlockSpec((ROUTE_TM, ROUTE_LANES), lambda i: (i, 0))
    return pl.pallas_call(
        _route_kernel, grid=(m // ROUTE_TM,), in_specs=[spec],
        out_specs=[spec, pl.BlockSpec((8, ROUTE_LANES), lambda i: (0, 0))],
        out_shape=[jax.ShapeDtypeStruct((m, ROUTE_LANES), F32), jax.ShapeDtypeStruct((8, ROUTE_LANES), F32)],
        scratch_shapes=[pltpu.VMEM((8, ROUTE_LANES), F32)],
        compiler_params=_cparams("arbitrary"), name="route",
    )(logits)


def _plan_kernel(cnt_ref, seg_ref, tile_ref):
    cnt = cnt_ref[...]
    lane = lax.broadcasted_iota(jnp.int32, cnt.shape, 1)
    is_e = (lane >= EXPERT_LANE0) & (lane < EXPERT_LANE0 + N_EXPERTS)
    size = jnp.where(is_e, jnp.floor((cnt + (EXPERT_TR - 1)) * (1.0 / EXPERT_TR)) * EXPERT_TR, 0.0)
    upper = (lax.broadcasted_iota(jnp.int32, (ROUTE_LANES, ROUTE_LANES), 0)
             < lax.broadcasted_iota(jnp.int32, (ROUTE_LANES, ROUTE_LANES), 1)).astype(F32)
    start = jnp.dot(size, upper, precision=HIGHEST, preferred_element_type=F32)
    end = start + size
    total = jnp.max(end, axis=-1, keepdims=True)
    row = lax.broadcasted_iota(jnp.int32, cnt.shape, 0)
    seg = jnp.where(row == SEG_START, start, jnp.where(row == SEG_SIZE, size, jnp.where(
        row == SEG_END, end, jnp.where(row == SEG_TILES, total * (1.0 / EXPERT_TR), cnt))))
    seg_ref[...] = seg.astype(jnp.int32)
    nt = tile_ref.shape[0]
    t0 = (lax.broadcasted_iota(jnp.int32, (nt, ROUTE_LANES), 0) * EXPERT_TR).astype(F32)
    lane_t = lax.broadcasted_iota(jnp.int32, (nt, ROUTE_LANES), 1)
    done = jnp.where((lane_t >= EXPERT_LANE0) & (lane_t < EXPERT_LANE0 + N_EXPERTS) & (end[0:1, :] <= t0), 1.0, 0.0)
    te = jnp.minimum(jnp.sum(done, axis=-1, keepdims=True), N_EXPERTS - 1.0)
    tile_ref[...] = jnp.broadcast_to(te, (nt, ROUTE_LANES)).astype(jnp.int32)


def _plan(counts, n_tiles):
    nt = -(-n_tiles // 8) * 8
    return pl.pallas_call(
        _plan_kernel,
        out_shape=[jax.ShapeDtypeStruct((8, ROUTE_LANES), jnp.int32),
                   jax.ShapeDtypeStruct((nt, ROUTE_LANES), jnp.int32)],
        name="plan",
    )(counts)


def _positions_kernel(route_ref, seg_ref, pos_ref):
    rec = route_ref[...]
    start = seg_ref[0:1, :].astype(F32)
    lane = lax.broadcasted_iota(jnp.int32, rec.shape, 1).astype(F32)
    seg_start = lambda col: jnp.sum(jnp.where(lane == rec[:, col:col + 1], start, 0.0), axis=-1, keepdims=True)
    p1 = seg_start(R_E1) + rec[:, R_RANK1:R_RANK1 + 1]
    p2 = seg_start(R_E2) + rec[:, R_RANK2:R_RANK2 + 1]
    both = jnp.where(lane == 0.0, p1, jnp.where(lane == 1.0, p2, 0.0))
    pos_ref[...] = jnp.transpose(both)[0:8, :].astype(jnp.int32)


def _positions(route, seg):
    m = route.shape[0]
    return pl.pallas_call(
        _positions_kernel, grid=(m // ROUTE_TM,),
        in_specs=[pl.BlockSpec((ROUTE_TM, ROUTE_LANES), lambda i: (i, 0)),
                  pl.BlockSpec((8, ROUTE_LANES), lambda i: (0, 0))],
        out_specs=pl.BlockSpec((8, ROUTE_TM), lambda i: (0, i)),
        out_shape=jax.ShapeDtypeStruct((8, m), jnp.int32),
        compiler_params=_cparams("parallel"), name="positions",
    )(route, seg)


SEG_START, SEG_SIZE, SEG_END, SEG_TILES, SEG_COUNT = range(5)


def _row_copies(src_of, dst_of, sem):
    copies = [pltpu.make_async_copy(src_of(j, k), dst_of(j, k), sem) for j in range(MOVE_TM) for k in (0, 1)]
    for n, c in enumerate(copies):
        c.start(priority=n % 2)
    for c in copies:
        c.wait()


def _inverse_kernel(seg_s, pos_s, tok_ref, *, n_tiles):
    i = pl.program_id(0)

    @pl.when(i == 0)
    def _():
        def fill(lo, hi):
            def body(p, carry):
                tok_ref[p] = 0
                return carry
            lax.fori_loop(lo, hi, body, 0)

        for e in range(N_EXPERTS):
            lane = EXPERT_LANE0 + e
            fill(seg_s[SEG_START, lane] + seg_s[SEG_COUNT, lane], seg_s[SEG_END, lane])
        fill(seg_s[SEG_TILES, 0] * EXPERT_TR, n_tiles * EXPERT_TR)

    for j in range(MOVE_TM):
        tok_ref[pos_s[0, j]] = i * MOVE_TM + j
        tok_ref[pos_s[1, j]] = i * MOVE_TM + j


def _inverse(pos, seg, n_tiles):
    m = pos.shape[1]
    return pl.pallas_call(
        functools.partial(_inverse_kernel, n_tiles=n_tiles),
        grid_spec=pltpu.PrefetchScalarGridSpec(
            num_scalar_prefetch=1, grid=(m // MOVE_TM,),
            in_specs=[pl.BlockSpec((8, MOVE_TM), lambda i, seg: (0, i), memory_space=pltpu.SMEM)],
            out_specs=pl.BlockSpec(memory_space=pltpu.SMEM)),
        out_shape=jax.ShapeDtypeStruct((n_tiles * EXPERT_TR,), jnp.int32),
        compiler_params=_cparams("arbitrary"), name="inverse",
    )(seg, pos)


def _gather_tile(tok_s, h2_ref, buf, sem, tile, go):
    for p in range(EXPERT_TR):
        go(pltpu.make_async_copy(h2_ref.at[pl.ds(tok_s[tile * EXPERT_TR + p], 1)], buf.at[pl.ds(p, 1)], sem), p)


def _experts_kernel(te_s, seg_s, tok_s, h2_ref, wg_ref, wu_ref, wd_ref, ys_ref, xbuf, sems):
    j = pl.program_id(0)
    n_used = seg_s[SEG_TILES, 0]
    start = lambda c, p: c.start(priority=p % 2)
    wait = lambda c, p: c.wait()

    @pl.when(j == 0)
    def _():
        _gather_tile(tok_s, h2_ref, xbuf.at[0], sems.at[0], 0, start)

    def step(slot):
        @pl.when(j < n_used)
        def _():
            _gather_tile(tok_s, h2_ref, xbuf.at[1 - slot], sems.at[1 - slot], j + 1, start)
            _gather_tile(tok_s, h2_ref, xbuf.at[slot], sems.at[slot], j, wait)
            x = xbuf[slot].astype(BF16)
            a = _dot(x, wg_ref[...].astype(BF16))
            b = _dot(x, wu_ref[...].astype(BF16))
            ys_ref[...] = _dot((_silu(a) * b).astype(BF16), wd_ref[...].astype(BF16))

        @pl.when(j == n_used)
        def _():
            _gather_tile(tok_s, h2_ref, xbuf.at[slot], sems.at[slot], j, wait)

        @pl.when(j >= n_used)
        def _():
            ys_ref[...] = jnp.zeros_like(ys_ref)

    for slot in (0, 1):
        pl.when(j % 2 == slot)(functools.partial(step, slot))


def _experts(h2, tok, tile_expert, seg, w_gate, w_up, w_down, layer):
    d = h2.shape[1]
    p = tok.shape[0]
    f = w_gate.shape[-1]
    wmap = lambda j, te, seg, tok: (layer * N_EXPERTS + te[jnp.minimum(j, seg[SEG_TILES, 0] - 1)], 0, 0)
    return pl.pallas_call(
        _experts_kernel,
        grid_spec=pltpu.PrefetchScalarGridSpec(
            num_scalar_prefetch=3, grid=(p // EXPERT_TR,),
            in_specs=[pl.BlockSpec(memory_space=pl.ANY),
                      pl.BlockSpec((None, d, f), wmap), pl.BlockSpec((None, d, f), wmap),
                      pl.BlockSpec((None, f, d), wmap)],
            out_specs=pl.BlockSpec((EXPERT_TR, d), lambda j, te, seg, tok: (j, 0)),
            scratch_shapes=[pltpu.VMEM((2, EXPERT_TR, d), F32), pltpu.SemaphoreType.DMA((2,))]),
        out_shape=jax.ShapeDtypeStruct((p, d), F32),
        compiler_params=_cparams("arbitrary"), name="experts",
    )(tile_expert, seg, tok, h2, w_gate, w_up, w_down)


def _combine_kernel(pos_s, route_ref, x1_ref, mod_ref, nw_ref, ys_ref, o_ref, y1buf, y2buf, sem, *, final_norm):
    bufs = (y1buf, y2buf)
    _row_copies(lambda j, k: ys_ref.at[pl.ds(pos_s[k, j], 1)], lambda j, k: bufs[k].at[pl.ds(j, 1)], sem)
    rec = route_ref[...]
    moe = rec[:, R_W1:R_W1 + 1] * y1buf[...] + rec[:, R_W2:R_W2 + 1] * y2buf[...]
    x = x1_ref[...] + mod_ref[5:6, :] * moe
    if final_norm:
        x = x * lax.rsqrt(jnp.mean(x * x, axis=-1, keepdims=True) + EPS) * nw_ref[...]
    o_ref[...] = x


def _combine(ys, pos, route, x1, mod, nw_final, rows_per_mod, final_norm):
    m, d = x1.shape
    row = lambda i: (i, 0)
    return pl.pallas_call(
        functools.partial(_combine_kernel, final_norm=final_norm),
        grid=(m // MOVE_TM,),
        in_specs=[pl.BlockSpec((8, MOVE_TM), lambda i: (0, i), memory_space=pltpu.SMEM),
                  pl.BlockSpec((MOVE_TM, ROUTE_LANES), row), pl.BlockSpec((MOVE_TM, d), row),
                  pl.BlockSpec((None, 8, d), lambda i: ((i * MOVE_TM) // rows_per_mod, 0, 0)),
                  pl.BlockSpec((1, d), lambda i: (0, 0)),
                  pl.BlockSpec(memory_space=pl.ANY)],
        out_specs=pl.BlockSpec((MOVE_TM, d), row),
        out_shape=jax.ShapeDtypeStruct((m, d), F32),
        scratch_shapes=[pltpu.VMEM((MOVE_TM, d), F32), pltpu.VMEM((MOVE_TM, d), F32), pltpu.SemaphoreType.DMA],
        compiler_params=_cparams("arbitrary"), name="combine",
    )(pos, route, x1, mod, nw_final, ys)


def _layer_weights(P, l):
    d = P['w_in'].shape[1]
    cw = P['pool_scale'].shape[1]
    hw = P['hyena_conv'].shape[2]
    o1, o2, o3 = cw, cw + hw, cw + hw + 3 * cw
    na = (P['w_in'].shape[2] - o3) // 3
    w_in = P['w_in'][l]
    segs = [w_in[:, o1:o2], w_in[:, o2:o3], w_in[:, o3:o3 + na], w_in[:, o3 + na:o3 + 2 * na],
            w_in[:, o3 + 2 * na:], w_in[:, :o1]]
    widths = tuple(int(s.shape[1]) for s in segs)
    w_in_r = jnp.concatenate(segs, axis=1).astype(BF16)
    gw = cw // len(POOL_WINDOWS)
    pool_bd = jnp.zeros((cw, cw), F32)
    for g in range(len(POOL_WINDOWS)):
        pool_bd = pool_bd.at[g * gw:(g + 1) * gw, g * gw:(g + 1) * gw].set(P['pool_w'][l, g])
    wr = jnp.concatenate([P['w_route_group'][l],
                          jnp.transpose(P['w_route_exp'][l], (1, 0, 2)).reshape(d, N_EXPERTS)], axis=1)
    wr = jnp.zeros((d, ROUTE_LANES), F32).at[:, :wr.shape[1]].set(wr)
    wr_hi = wr.astype(BF16)
    wr_lo = (wr - wr_hi.astype(F32)).astype(BF16)
    br = jnp.concatenate([P['b_route_group'][l], P['b_route_exp'][l].reshape(-1)])
    br = jnp.zeros((1, ROUTE_LANES), F32).at[0, :br.shape[0]].set(br)
    return dict(
        widths=widths, w_in_r=w_in_r, norm_mix=P['norm_mix'][l][None], norm_ffn=P['norm_ffn'][l][None],
        pool_bd=pool_bd.astype(BF16), pool_scale=P['pool_scale'][l][None], sconv_w=P['sconv_w'][l],
        hyena_conv=P['hyena_conv'][l], hyena_bias=P['hyena_bias'][l],
        merge=(P['w_gate'][l].astype(BF16), P['b_gate'][l][None], P['w_br_a'][l].astype(BF16),
               P['w_br_b'][l].astype(BF16), P['w_br_c'][l].astype(BF16), P['w_br_d'][l].astype(BF16),
               P['w_out'][l].astype(BF16), jnp.stack([wr_hi, wr_lo]), br),
    )


def _run_stream(x3, mods, LW, EW, nw_final, hy, attend, depth):
    b, seq_len, d = x3.shape
    m = b * seq_len
    x = x3.reshape(m, d)
    rows_per_mod = m // mods.shape[1]
    tm = min(512, seq_len)
    tc = min(256, seq_len)
    kvs = []
    for l in range(depth):
        W = LW[l]
        mod = mods[l]
        cw = W['pool_scale'].shape[1]
        h, ph, ps, q, k, v, pp = _inproj(x, mod, W['norm_mix'], W['w_in_r'], W['widths'],
                                         min(INPROJ_TM, rows_per_mod), rows_per_mod)
        kvs.append((k, v))
        ya, yc, u_t, x0_t = _local_mixers(pp, ps, ph, W['pool_bd'], W['pool_scale'], W['sconv_w'],
                                          W['hyena_conv'], seq_len, tc)
        yb_t = _hyena_conv(u_t, x0_t, hy['fwd_bf'], hy['inv_c'], hy['inv_s'], hy['spec'][l],
                           W['hyena_bias'], cw)
        yd = attend(l, q, k, v)
        x1, h2, logits = _merge(x, h, ya, yb_t, yc, yd, mod, W['norm_ffn'], W['merge'], tm,
                                rows_per_mod, seq_len)
        route, counts = _route(logits)
        n_tiles = 2 * m // EXPERT_TR + N_EXPERTS
        seg, tile_tab = _plan(counts, n_tiles)
        pos = _positions(route, seg)
        tok = _inverse(pos, seg, n_tiles)
        ys = _experts(h2, tok, tile_tab[:, 0], seg, EW[0], EW[1], EW[2], l)
        x = _combine(ys, pos, route, x1, mod, nw_final, rows_per_mod, final_norm=(l == depth - 1))
    return x, kvs


def _hyena_setup(seq_len, P, depth):
    fwd32, inv_c, inv_s = _dft_matrices(seq_len)
    spec = []
    for l in range(depth):
        filt = _hyena_filter(seq_len, P['hyena_f1'][l], P['hyena_fb1'][l], P['hyena_f2'][l],
                             P['hyena_fb2'][l], P['hyena_f3'][l], P['hyena_freq'][l], P['hyena_decay'][l])
        spec.append(_filter_spectrum(fwd32, filt))
    return dict(fwd_bf=fwd32.astype(BF16), inv_c=inv_c, inv_s=inv_s, spec=spec)


def kernel(x_prompt, x_sample, cache_k, cache_v, c, c_ctx, w_ada, b_ada, norm_mix, w_in, w_gate, b_gate, pool_w, pool_scale, hyena_conv, hyena_f1, hyena_fb1, hyena_f2, hyena_fb2, hyena_f3, hyena_freq, hyena_decay, hyena_bias, sconv_w, na_rpb, w_br_a, w_br_b, w_br_c, w_br_d, w_out, norm_ffn, w_route_group, b_route_group, w_route_exp, b_route_exp, w_e_gate, w_e_up, w_e_down, norm_final):
    P = dict(w_in=w_in, w_gate=w_gate, b_gate=b_gate, pool_w=pool_w, pool_scale=pool_scale,
             hyena_conv=hyena_conv, hyena_f1=hyena_f1, hyena_fb1=hyena_fb1, hyena_f2=hyena_f2,
             hyena_fb2=hyena_fb2, hyena_f3=hyena_f3, hyena_freq=hyena_freq, hyena_decay=hyena_decay,
             hyena_bias=hyena_bias, sconv_w=sconv_w, w_br_a=w_br_a, w_br_b=w_br_b, w_br_c=w_br_c,
             w_br_d=w_br_d, w_out=w_out, norm_mix=norm_mix, norm_ffn=norm_ffn,
             w_route_group=w_route_group, b_route_group=b_route_group, w_route_exp=w_route_exp,
             b_route_exp=b_route_exp, w_e_gate=w_e_gate, w_e_up=w_e_up, w_e_down=w_e_down)
    depth, d, _ = w_ada.shape
    bp, lp, _ = x_prompt.shape
    bs, ls, _ = x_sample.shape
    assert (ls // GRID_W) % LAT_QROWS == 0 and ls // GRID_W >= LAT_KROWS

    n_c = 1 + bs
    n_pad = -(-n_c // 8) * 8
    cvecs = jnp.zeros((n_pad, d), F32).at[0].set(c_ctx).at[1:n_c].set(c)
    ada = _ada(cvecs, w_ada, b_ada).reshape(depth, n_pad, 6, d)
    ada = jnp.concatenate([ada, jnp.zeros((depth, n_pad, 2, d), F32)], axis=2)

    LW = [_layer_weights(P, l) for l in range(depth)]
    nw_final = norm_final[None]
    f = w_e_gate.shape[-1]
    EW = (w_e_gate.reshape(depth * N_EXPERTS, d, f), w_e_up.reshape(depth * N_EXPERTS, d, f),
          w_e_down.reshape(depth * N_EXPERTS, f, d))

    hy_p = _hyena_setup(lp, P, depth)
    xp, kv_p = _run_stream(x_prompt, ada[:, 0:1], LW, EW, nw_final, hy_p,
                           lambda l, q, k, v: _context_attention(q, k, v, lp), depth)
    y_prompt = xp.reshape(bp, lp, d)
    new_k = jnp.stack([k.reshape(bp, lp, N_HEADS, HEAD_DIM) for k, _ in kv_p], axis=1)
    new_v = jnp.stack([v.reshape(bp, lp, N_HEADS, HEAD_DIM) for _, v in kv_p], axis=1)

    hy_s = _hyena_setup(ls, P, depth)
    past = cache_k.shape[2]
    ck = cache_k.reshape(bs, depth, past, N_HEADS * HEAD_DIM)
    cv = cache_v.reshape(bs, depth, past, N_HEADS * HEAD_DIM)
    biases = [_latent_bias(na_rpb[l], ls // GRID_W) for l in range(depth)]
    xs, _ = _run_stream(x_sample, ada[:, 1:n_c], LW, EW, nw_final, hy_s,
                        lambda l, q, k, v: _latent_attention(q, k, v, ck, cv, l, biases[l], ls),
                        depth)
    y_sample = xs.reshape(bs, ls, d)
    return (y_prompt, y_sample, new_k, new_v)
```

```python
import functools
import math

import numpy as np
import jax
import jax.numpy as jnp
from jax import lax
from jax.experimental import pallas as pl
from jax.experimental.pallas import tpu as pltpu

F32 = jnp.float32
BF16 = jnp.bfloat16
HIGHEST = lax.Precision.HIGHEST

EPS = 1e-6
GRID_W = 64
NA_ROWS = 8
NA_COLS = 16
N_HEADS = 8
HEAD_DIM = 64
POOL_WINDOWS = (2, 4, 8, 16)
HYENA_BANDS = 16
MOE_GROUPS = 4
MOE_EXPERTS = 8
N_EXPERTS = MOE_GROUPS * MOE_EXPERTS
ROUTE_LANES = 128
EXPERT_LANE0 = MOE_GROUPS
HALO = 8
NEG_BIG = -1e30
VMEM_LIMIT_BYTES = 48 * 1024 * 1024


def _cparams(*sem):
    return pltpu.CompilerParams(dimension_semantics=sem, vmem_limit_bytes=VMEM_LIMIT_BYTES)


def _resident(shape):
    nd = len(shape)
    return pl.BlockSpec(shape, lambda *_: (0,) * nd, pipeline_mode=pl.Buffered(1))


def _dot(a, b):
    return jnp.dot(a, b, preferred_element_type=F32)


def _silu(x):
    return x * jax.nn.sigmoid(x)


def _ada_kernel(cv_ref, w_ref, b_ref, o_ref):
    o_ref[...] = jnp.dot(_silu(cv_ref[...]), w_ref[...], precision=HIGHEST,
                         preferred_element_type=F32) + b_ref[...]


def _ada(cvecs, w_ada, b_ada):
    depth, d, d6 = w_ada.shape
    r = cvecs.shape[0]
    return pl.pallas_call(
        _ada_kernel,
        grid=(depth, d6 // d),
        in_specs=[pl.BlockSpec((r, d), lambda l, j: (0, 0)),
                  pl.BlockSpec((None, d, d), lambda l, j: (l, 0, j)),
                  pl.BlockSpec((None, 1, d), lambda l, j: (l, 0, j))],
        out_specs=pl.BlockSpec((None, r, d), lambda l, j: (l, 0, j)),
        out_shape=jax.ShapeDtypeStruct((depth, r, d6), F32),
        compiler_params=_cparams("parallel", "parallel"),
        name="ada",
    )(cvecs, w_ada, b_ada.reshape(depth, 1, d6))


def _rms_mod(x, nw, shift, scale):
    y = x * lax.rsqrt(jnp.mean(x * x, axis=-1, keepdims=True) + EPS) * nw
    return y * (1.0 + scale) + shift


def _inproj_kernel(x_ref, mod_ref, nw_ref, w_ref, h_ref, ph_ref, ps_ref, q_ref, k_ref, v_ref, pp_ref,
                   *, widths):
    mod = mod_ref[...]
    h = _rms_mod(x_ref[...], nw_ref[...], mod[0:1, :], mod[1:2, :]).astype(BF16)
    h_ref[...] = h
    off = 0
    for ref, wd in zip((ph_ref, ps_ref, q_ref, k_ref, v_ref, pp_ref), widths):
        r = _dot(h, w_ref[:, off:off + wd])
        if ref is q_ref:
            r = r * (HEAD_DIM ** -0.5)
        ref[...] = r.astype(ref.dtype)
        off += wd


def _inproj(x, mod, nw, w_in_r, widths, tm, rows_per_mod):
    m, d = x.shape
    n = w_in_r.shape[1]
    dts = (F32, F32, BF16, F32, F32, F32)
    row = lambda i: (i, 0)
    return pl.pallas_call(
        functools.partial(_inproj_kernel, widths=widths),
        grid=(m // tm,),
        in_specs=[pl.BlockSpec((tm, d), row),
                  pl.BlockSpec((None, 8, d), lambda i: ((i * tm) // rows_per_mod, 0, 0)),
                  pl.BlockSpec((1, d), lambda i: (0, 0)),
                  _resident((d, n))],
        out_specs=[pl.BlockSpec((tm, d), row)] + [pl.BlockSpec((tm, wd), row) for wd in widths],
        out_shape=[jax.ShapeDtypeStruct((m, d), BF16)]
        + [jax.ShapeDtypeStruct((m, wd), dt) for wd, dt in zip(widths, dts)],
        compiler_params=_cparams("parallel"),
        name="inproj",
    )(x, mod, nw, w_in_r)


def _fill_padded(pad_ref, prev_ref, cur_ref, next_ref, first, last, tc):
    zero = jnp.zeros((HALO, cur_ref.shape[1]), F32)
    pad_ref[0:HALO, :] = jnp.where(first, zero, prev_ref[...])
    pad_ref[HALO:HALO + tc, :] = cur_ref[...]
    pad_ref[HALO + tc:2 * HALO + tc, :] = jnp.where(last, zero, next_ref[...])


def _local_kernel(pp_ref, pp_prev, pp_next, ps_ref, ps_prev, ps_next, ph_ref, ph_prev, ph_next,
                  pw_ref, pscale_ref, sw_ref, hw_ref,
                  ya_ref, yc_ref, u_ref, x0_ref,
                  pad_p, pad_s, pad_h, *, seq_len, tc):
    nchunk = seq_len // tc
    j = pl.program_id(0) % nchunk
    first = j == 0
    last = j == nchunk - 1
    cw = pp_ref.shape[1]
    sw = cw

    _fill_padded(pad_p, pp_prev, pp_ref, pp_next, first, last, tc)
    sh = lambda k: pad_p[HALO + k:HALO + k + tc, :]
    u = pp_ref[...]
    sums = {}
    acc = u
    lo_done, hi_done = 0, 0
    for win in POOL_WINDOWS:
        lo, hi = -(win // 2), win // 2 - 1
        for k in range(lo, lo_done):
            acc = acc + sh(k)
        for k in range(hi_done + 1, hi + 1):
            acc = acc + sh(k)
        lo_done, hi_done = lo, hi
        sums[win] = acc
    t = j * tc + lax.broadcasted_iota(jnp.int32, (tc, 1), 0)
    lane = lax.broadcasted_iota(jnp.int32, (1, cw), 1)
    gw = cw // len(POOL_WINDOWS)
    pooled = None
    for g, win in reversed(list(enumerate(POOL_WINDOWS))):
        cnt = jnp.minimum(t - win // 2 + win, seq_len) - jnp.maximum(t - win // 2, 0)
        val = sums[win] * (1.0 / cnt.astype(F32))
        pooled = val if pooled is None else jnp.where(lane < (g + 1) * gw, val, pooled)
    pooled = pooled - u
    ya = _dot(pooled.astype(BF16), pw_ref[...]) * pscale_ref[...]
    ya_ref[...] = ya.astype(BF16)

    _fill_padded(pad_s, ps_prev, ps_ref, ps_next, first, last, tc)
    w3 = sw_ref[...]
    z = lambda k: (pad_s[HALO + k:HALO + k + tc, 2 * sw:3 * sw] * pad_s[HALO + k:HALO + k + tc, 0:sw])
    conv = w3[0:1, :] * z(-1) + w3[1:2, :] * z(0) + w3[2:3, :] * z(1)
    yc_ref[...] = (ps_ref[:, sw:2 * sw] * conv).astype(BF16)

    _fill_padded(pad_h, ph_prev, ph_ref, ph_next, first, last, tc)
    hw = hw_ref[...]
    c3 = (hw[0:1, :] * pad_h[HALO - 1:HALO - 1 + tc, :] + hw[1:2, :] * pad_h[HALO:HALO + tc, :]
          + hw[2:3, :] * pad_h[HALO + 1:HALO + 1 + tc, :])
    x0_ref[...] = c3[:, 0:sw]
    u_ref[...] = c3[:, sw:2 * sw] * c3[:, 2 * sw:3 * sw]


def _local_mixers(pp, ps, ph, pool_bd, pool_scale, sconv_w, hyena_conv, seq_len, tc):
    m, cw = pp.shape
    nb = m // seq_len
    nchunk = seq_len // tc
    hpc = tc // HALO
    nhb = m // HALO
    row = lambda i: (i, 0)
    prev = lambda i: (jnp.maximum(i * hpc - 1, 0), 0)
    nxt = lambda i: (jnp.minimum((i + 1) * hpc, nhb - 1), 0)
    tl = lambda i: (i % nchunk, i // nchunk)
    const = lambda i: (0, 0)

    def trio(width):
        return [pl.BlockSpec((tc, width), row), pl.BlockSpec((HALO, width), prev),
                pl.BlockSpec((HALO, width), nxt)]

    return pl.pallas_call(
        functools.partial(_local_kernel, seq_len=seq_len, tc=tc),
        grid=(m // tc,),
        in_specs=trio(cw) + trio(3 * cw) + trio(3 * cw)
        + [pl.BlockSpec((cw, cw), const), pl.BlockSpec((1, cw), const),
           pl.BlockSpec((3, cw), const), pl.BlockSpec((3, 3 * cw), const)],
        out_specs=[pl.BlockSpec((tc, cw), row), pl.BlockSpec((tc, cw), row),
                   pl.BlockSpec((tc, cw), tl), pl.BlockSpec((tc, cw), tl)],
        out_shape=[jax.ShapeDtypeStruct((m, cw), BF16), jax.ShapeDtypeStruct((m, cw), BF16),
                   jax.ShapeDtypeStruct((seq_len, nb * cw), F32),
                   jax.ShapeDtypeStruct((seq_len, nb * cw), F32)],
        scratch_shapes=[pltpu.VMEM((tc + 2 * HALO, cw), F32), pltpu.VMEM((tc + 2 * HALO, 3 * cw), F32),
                        pltpu.VMEM((tc + 2 * HALO, 3 * cw), F32)],
        compiler_params=_cparams("parallel"),
        name="local_mixers",
    )(pp, pp, pp, ps, ps, ps, ph, ph, ph, pool_bd, pool_scale, sconv_w, hyena_conv)


def _trig(rows, cols, n):
    split = 64
    r = np.asarray(rows, np.int64)[:, None]
    c = np.asarray(cols, np.int64)
    assert c[0] % split == 0 and len(c) % split == 0 and np.all(np.diff(c) == 1)
    c0 = np.arange(split)[None, :]
    c1 = c[::split][None, :]
    ang0 = ((r * c0) % n) * (2.0 * math.pi / n)
    ang1 = ((r * c1) % n) * (2.0 * math.pi / n)
    tab = lambda a: jnp.asarray(a, F32)
    ca, sa = tab(np.cos(ang0))[:, None, :], tab(np.sin(ang0))[:, None, :]
    cb, sb = tab(np.cos(ang1))[:, :, None], tab(np.sin(ang1))[:, :, None]
    shape = (len(rows), len(c))
    return (ca * cb - sa * sb).reshape(shape), (sa * cb + ca * sb).reshape(shape)


def _dft_matrices(seq_len):
    L = seq_len
    n = 2 * L
    k = np.arange(L)
    cos_f, sin_f = _trig(k, k, n)
    first = jnp.asarray(k == 0)
    alt = jnp.asarray(np.where(k % 2 == 0, 1.0, -1.0), F32)
    im = jnp.where(first[:, None], alt[None, :], -sin_f)
    fwd = jnp.concatenate([cos_f, im], axis=0)
    tp = np.arange(L // 2, L // 2 + L)
    cos_i, sin_i = _trig(tp, k, n)
    alt_t = jnp.asarray(np.where(tp % 2 == 0, 1.0, -1.0) / n, F32)
    inv_c = jnp.where(first[None, :], 1.0 / n, cos_i * (2.0 / n))
    inv_s = jnp.where(first[None, :], alt_t[:, None], sin_i * (-2.0 / n))
    return fwd, inv_c.astype(BF16), inv_s.astype(BF16)


def _hyena_embedding(seq_len, width):
    t = np.arange(seq_len, dtype=np.float64)
    w = (2.0 * math.pi / seq_len) * t
    bands = np.linspace(1e-4, HYENA_BANDS - 1, HYENA_BANDS)
    z = np.concatenate([(t / (seq_len - 1))[:, None], np.cos(w[:, None] * bands),
                        -np.sin(w[:, None] * bands)], axis=-1)
    out = np.zeros((seq_len, width), np.float32)
    out[:, :z.shape[1]] = z
    return jnp.asarray(out)


def _filter_kernel(z_ref, f1_ref, fb1_ref, f2_ref, fb2_ref, f3_ref, freq_ref, decay_ref, o_ref, *, seq_len):
    hdot = lambda a, b: jnp.dot(a, b, precision=HIGHEST, preferred_element_type=F32)
    fr = freq_ref[...]
    hdn = jnp.sin(fr * (hdot(z_ref[...], f1_ref[...]) + fb1_ref[...]))
    hdn = jnp.sin(fr * (hdot(hdn, f2_ref[...]) + fb2_ref[...]))
    filt = hdot(hdn, f3_ref[...])
    t = lax.broadcasted_iota(jnp.int32, (seq_len, 1), 0)
    dist = jnp.abs(t - seq_len // 2).astype(F32) / (seq_len / 2)
    filt = filt * jnp.exp(-dist * decay_ref[...])
    o_ref[...] = filt / jnp.sum(jnp.abs(filt), axis=0, keepdims=True)


def _pad2(a, rows, cols):
    return jnp.zeros((rows, cols), F32).at[:a.shape[0], :a.shape[1]].set(a)


def _hyena_filter(seq_len, f1, fb1, f2, fb2, f3, freq, decay):
    p = 128
    c = f3.shape[1]
    args = (_hyena_embedding(seq_len, p), _pad2(f1, p, p), _pad2(fb1[None], 1, p), _pad2(f2, p, p),
            _pad2(fb2[None], 1, p), _pad2(f3, p, c), _pad2(freq[None], 1, p), decay[None])
    return pl.pallas_call(
        functools.partial(_filter_kernel, seq_len=seq_len),
        out_shape=jax.ShapeDtypeStruct((seq_len, c), F32),
        compiler_params=pltpu.CompilerParams(vmem_limit_bytes=VMEM_LIMIT_BYTES),
        name="hyena_filter",
    )(*args)


def _spectrum_kernel(f_ref, h_ref, o_ref):
    o_ref[...] = jnp.dot(f_ref[...], h_ref[...], precision=HIGHEST, preferred_element_type=F32)


def _filter_spectrum(fwd32, filt):
    n, L = fwd32.shape
    c = filt.shape[1]
    tf = min(n, 512)
    return pl.pallas_call(
        _spectrum_kernel,
        grid=(n // tf,),
        in_specs=[pl.BlockSpec((tf, L), lambda i: (i, 0)), pl.BlockSpec((L, c), lambda i: (0, 0))],
        out_specs=pl.BlockSpec((tf, c), lambda i: (i, 0)),
        out_shape=jax.ShapeDtypeStruct((n, c), F32),
        compiler_params=_cparams("parallel"),
        name="filter_spectrum",
    )(fwd32, filt)


def _hy_fwd_kernel(fc_ref, fs_ref, u_ref, hr_ref, hi_ref, yr_ref, yi_ref, *, cw):
    ub = u_ref[...].astype(BF16)
    ur = _dot(fc_ref[...], ub)
    ui = _dot(fs_ref[...], ub)
    tf = ur.shape[0]
    row0 = (pl.program_id(0) * tf + lax.broadcasted_iota(jnp.int32, (tf, 1), 0)) == 0
    hr, hi = hr_ref[...], hi_ref[...]
    for s in range(ur.shape[1] // cw):
        a, b = ur[:, s * cw:(s + 1) * cw], ui[:, s * cw:(s + 1) * cw]
        yr = jnp.where(row0, a * hr, a * hr - b * hi)
        yi = jnp.where(row0, b * hi, a * hi + b * hr)
        yr_ref[:, s * cw:(s + 1) * cw] = yr.astype(BF16)
        yi_ref[:, s * cw:(s + 1) * cw] = yi.astype(BF16)


def _hy_inv_kernel(ic_ref, is_ref, yr_ref, yi_ref, u_ref, x0_ref, bias_ref, o_ref):
    conv = _dot(ic_ref[...], yr_ref[...]) + _dot(is_ref[...], yi_ref[...])
    o_ref[...] = ((conv + bias_ref[...] * u_ref[...]) * x0_ref[...]).astype(BF16)


def _hyena_conv(u_t, x0_t, fwd_bf, inv_c, inv_s, spec, bias, cw):
    L, ncol = u_t.shape
    tf = min(L, 1024)
    tn = min(ncol, 512)
    nf = L // tf
    yr, yi = pl.pallas_call(
        functools.partial(_hy_fwd_kernel, cw=cw),
        grid=(nf, ncol // tn),
        in_specs=[pl.BlockSpec((tf, L), lambda i, j: (i, 0)),
                  pl.BlockSpec((tf, L), lambda i, j: (nf + i, 0)),
                  pl.BlockSpec((L, tn), lambda i, j: (0, j)),
                  pl.BlockSpec((tf, cw), lambda i, j: (i, 0)),
                  pl.BlockSpec((tf, cw), lambda i, j: (nf + i, 0))],
        out_specs=[pl.BlockSpec((tf, tn), lambda i, j: (i, j))] * 2,
        out_shape=[jax.ShapeDtypeStruct((L, ncol), BF16)] * 2,
        compiler_params=_cparams("parallel", "parallel"),
        name="hyena_dft",
    )(fwd_bf, fwd_bf, u_t, spec, spec)
    bias_t = jnp.tile(bias[None, :], (1, tn // cw))
    return pl.pallas_call(
        _hy_inv_kernel,
        grid=(nf, ncol // tn),
        in_specs=[pl.BlockSpec((tf, L), lambda i, j: (i, 0)),
                  pl.BlockSpec((tf, L), lambda i, j: (i, 0)),
                  pl.BlockSpec((L, tn), lambda i, j: (0, j)),
                  pl.BlockSpec((L, tn), lambda i, j: (0, j)),
                  pl.BlockSpec((tf, tn), lambda i, j: (i, j)),
                  pl.BlockSpec((tf, tn), lambda i, j: (i, j)),
                  pl.BlockSpec((1, tn), lambda i, j: (0, 0))],
        out_specs=pl.BlockSpec((tf, tn), lambda i, j: (i, j)),
        out_shape=jax.ShapeDtypeStruct((L, ncol), BF16),
        compiler_params=_cparams("parallel", "parallel"),
        name="hyena_idft",
    )(inv_c, inv_s, yr, yi, u_t, x0_t, bias_t)


def _qkt(q, k):
    return lax.dot_general(q, k, (((1,), (1,)), ((), ())), preferred_element_type=F32)


PAIR = 2 * HEAD_DIM


def _attend_pair(q_pair, parts, bias_of=None):
    low = lax.broadcasted_iota(jnp.int32, (1, PAIR), 1) < HEAD_DIM
    one = jnp.ones((), BF16)
    accs = []
    for half in (0, 1):
        mine = low if half == 0 else jnp.logical_not(low)
        qh = jnp.where(mine, q_pair, jnp.zeros((), BF16))
        scores = []
        for i, (k, _) in enumerate(parts):
            s = _qkt(qh, k)
            b = None if bias_of is None else bias_of(half, i)
            scores.append(s if b is None else s + b)
        m = None
        for s in scores:
            mi = jnp.max(s, axis=-1, keepdims=True)
            m = mi if m is None else jnp.maximum(m, mi)
        acc = None
        for s, (_, v) in zip(scores, parts):
            p = jnp.exp((s - m).astype(BF16))
            o = _dot(p, jnp.where(mine, v, one))
            acc = o if acc is None else acc + o
        accs.append(acc)
    num = jnp.where(low, accs[0], accs[1])
    den = pltpu.roll(jnp.where(low, accs[1], accs[0]), HEAD_DIM, axis=1)
    return num * (1.0 / den)


def _ctx_attn_kernel(q_ref, k_ref, v_ref, o_ref):
    for j in range(N_HEADS // 2):
        sl = slice(j * PAIR, (j + 1) * PAIR)
        parts = [(k_ref[:, sl].astype(BF16), v_ref[:, sl].astype(BF16))]
        o_ref[:, sl] = _attend_pair(q_ref[:, sl], parts).astype(BF16)


def _context_attention(q, k, v, seq_len):
    m, w = q.shape
    spec = pl.BlockSpec((seq_len, w), lambda b: (b, 0))
    return pl.pallas_call(
        _ctx_attn_kernel,
        grid=(m // seq_len,),
        in_specs=[spec, spec, spec],
        out_specs=spec,
        out_shape=jax.ShapeDtypeStruct((m, w), BF16),
        compiler_params=_cparams("parallel"),
        name="context_attention",
    )(q, k, v)


LAT_QROWS = 4
LAT_KROWS = NA_ROWS + LAT_QROWS
LAT_TQ = LAT_QROWS * GRID_W


def _lat_attn_kernel(q_ref, k0_ref, k1_ref, k2_ref, v0_ref, v1_ref, v2_ref, ck_ref, cv_ref, g_ref, o_ref,
                     *, rows):
    nblk = rows // LAT_QROWS
    rblk = pl.program_id(0)
    kblk = jnp.clip(rblk - 1, 0, nblk - 3)
    slot = {}
    for ri in range(LAT_QROWS):
        r = rblk * LAT_QROWS + ri
        rs = jnp.clip(r - NA_ROWS // 2, 0, rows - NA_ROWS)
        for kj in range(LAT_KROWS):
            krow = kblk * LAT_QROWS + kj
            in_window = (krow >= rs) & (krow < rs + NA_ROWS)
            slot[ri, kj] = jnp.where(in_window, krow - r + NA_ROWS - 1, 2 * NA_ROWS - 1)
    low_half = lax.broadcasted_iota(jnp.int32, (GRID_W, 2 * GRID_W), 1) < GRID_W

    def bias_block(h, i):
        row_blocks = []
        for ri in range(LAT_QROWS):
            tiles = []
            for kj in range(LAT_QROWS * i, LAT_QROWS * (i + 1), 2):
                tiles.append(jnp.where(low_half, g_ref[h, slot[ri, kj]], g_ref[h, slot[ri, kj + 1]]))
            row_blocks.append(jnp.concatenate(tiles, axis=1))
        return jnp.concatenate(row_blocks, axis=0)

    n_local = 3
    for j in range(N_HEADS // 2):
        sl = slice(j * PAIR, (j + 1) * PAIR)
        parts = [(kr[:, sl].astype(BF16), vr[:, sl].astype(BF16))
                 for kr, vr in ((k0_ref, v0_ref), (k1_ref, v1_ref), (k2_ref, v2_ref), (ck_ref, cv_ref))]
        bias_of = lambda half, i, j=j: bias_block(2 * j + half, i) if i < n_local else None
        o_ref[:, sl] = _attend_pair(q_ref[:, sl], parts, bias_of).astype(BF16)


def _latent_bias(rpb, rows):
    del rows
    nh, ndr, ndc = rpb.shape
    qc = np.arange(GRID_W)[:, None]
    kc = (np.arange(2 * GRID_W) % GRID_W)[None, :]
    ws = np.clip(qc - NA_COLS // 2, 0, GRID_W - NA_COLS)
    col_ok = ((kc >= ws) & (kc < ws + NA_COLS)).reshape(-1)
    dc = np.clip(kc - qc + NA_COLS - 1, 0, ndc - 1).reshape(-1)
    onehot = jnp.asarray((dc[None, :] == np.arange(ndc)[:, None]) & col_ok[None, :], F32)
    g = jnp.einsum('hrd,dx->hrx', rpb, onehot, precision=HIGHEST)
    g = g + jnp.asarray(np.where(col_ok, 0.0, NEG_BIG), F32)
    g = jnp.concatenate([g, jnp.full((nh, 2 * NA_ROWS - ndr, g.shape[-1]), NEG_BIG, F32)], axis=1)
    return g.reshape(nh, 2 * NA_ROWS, GRID_W, 2 * GRID_W)


def _latent_attention(q, k, v, ctx_k, ctx_v, layer, bias, seq_len):
    m, w = q.shape
    rows = seq_len // GRID_W
    nblk = rows // LAT_QROWS
    nb = m // seq_len
    tq = LAT_TQ
    nctx = ctx_k.shape[2]

    def kspec(i):
        return pl.BlockSpec((tq, w), lambda r, b: (b * nblk + jnp.clip(r - 1, 0, nblk - 3) + i, 0))

    qspec = pl.BlockSpec((tq, w), lambda r, b: (b * nblk + r, 0))
    cspec = pl.BlockSpec((None, None, nctx, w), lambda r, b: (b, layer, 0, 0))
    return pl.pallas_call(
        functools.partial(_lat_attn_kernel, rows=rows),
        grid=(nblk, nb),
        in_specs=[qspec, kspec(0), kspec(1), kspec(2), kspec(0), kspec(1), kspec(2), cspec, cspec,
                  _resident(bias.shape)],
        out_specs=qspec,
        out_shape=jax.ShapeDtypeStruct((m, w), BF16),
        compiler_params=_cparams("parallel", "parallel"),
        name="latent_attention",
    )(q, k, k, k, v, v, v, ctx_k, ctx_v, bias)


def _merge_kernel(x_ref, h_ref, ya_ref, yb_ref, yc_ref, yd_ref, mod_ref, nw_ref,
                  wg_ref, bg_ref, wa_ref, wb_ref, wc_ref, wd_ref, wo_ref, wr_ref, br_ref,
                  x1_ref, h2_ref, route_ref, cnt_ref, carry_ref):
    @pl.when(pl.program_id(0) == 0)
    def _():
        carry_ref[...] = jnp.zeros_like(carry_ref)

    d = x_ref.shape[1]
    h = h_ref[...]
    merged = None
    for i, (y_ref, w_ref) in enumerate(((ya_ref, wa_ref), (yb_ref, wb_ref), (yc_ref, wc_ref), (yd_ref, wd_ref))):
        gate = jax.nn.sigmoid(_dot(h, wg_ref[:, i * d:(i + 1) * d]) + bg_ref[:, i * d:(i + 1) * d])
        term = gate * _dot(y_ref[...], w_ref[...])
        merged = term if merged is None else merged + term
    mod = mod_ref[...]
    x1 = x_ref[...] + mod[2:3, :] * _dot(merged.astype(BF16), wo_ref[...])
    x1_ref[...] = x1
    h2 = _rms_mod(x1, nw_ref[...], mod[3:4, :], mod[4:5, :])
    h2_ref[...] = h2
    h2_hi = h2.astype(BF16)
    h2_lo = (h2 - h2_hi.astype(F32)).astype(BF16)
    logits = (_dot(h2_hi, wr_ref[0]) + _dot(h2_lo, wr_ref[0]) + _dot(h2_hi, wr_ref[1])) + br_ref[...]
    route_ref[...] = _route_tile(logits, carry_ref)
    cnt_ref[...] = carry_ref[...]


def _merge(x, h, ya, yb_t, yc, yd, mod, nw, wts, tm, rows_per_mod, seq_len):
    m, d = x.shape
    cw = ya.shape[1]
    nchunk = seq_len // tm
    row = lambda i: (i, 0)
    const2 = lambda i: (0, 0)
    w_specs = [_resident(w.shape) for w in wts]
    return pl.pallas_call(
        _merge_kernel,
        grid=(m // tm,),
        in_specs=[pl.BlockSpec((tm, d), row), pl.BlockSpec((tm, d), row),
                  pl.BlockSpec((tm, cw), row),
                  pl.BlockSpec((tm, cw), lambda i: (i % nchunk, i // nchunk)),
                  pl.BlockSpec((tm, cw), row),
                  pl.BlockSpec((tm, yd.shape[1]), row),
                  pl.BlockSpec((None, 8, d), lambda i: ((i * tm) // rows_per_mod, 0, 0)),
                  pl.BlockSpec((1, d), const2)] + w_specs,
        out_specs=[pl.BlockSpec((tm, d), row), pl.BlockSpec((tm, d), row),
                   pl.BlockSpec((tm, ROUTE_LANES), row), pl.BlockSpec((8, ROUTE_LANES), const2)],
        out_shape=[jax.ShapeDtypeStruct((m, d), F32), jax.ShapeDtypeStruct((m, d), F32),
                   jax.ShapeDtypeStruct((m, ROUTE_LANES), F32), jax.ShapeDtypeStruct((8, ROUTE_LANES), F32)],
        scratch_shapes=[pltpu.VMEM((8, ROUTE_LANES), F32)],
        compiler_params=_cparams("arbitrary"),
        name="merge",
    )(x, h, ya, yb_t, yc, yd, mod, nw, *wts)


ROUTE_TM = 512
EXPERT_TR = 256
MOVE_TM = 512
INPROJ_TM = 1024
R_E1, R_E2, R_RANK1, R_RANK2, R_W1, R_W2 = range(6)


def _route_tile(lg, carry_ref):
    lane = lax.broadcasted_iota(jnp.int32, lg.shape, 1).astype(F32)
    neg = jnp.float32(-jnp.inf)
    big = jnp.float32(ROUTE_LANES)
    gl = jnp.where(lane < MOE_GROUPS, lg, neg)
    gm = jnp.max(gl, axis=-1, keepdims=True)
    g_prob = 1.0 / jnp.sum(jnp.exp(gl - gm), axis=-1, keepdims=True)
    gidx = jnp.min(jnp.where(gl == gm, lane, big), axis=-1, keepdims=True)
    e0 = EXPERT_LANE0 + gidx * MOE_EXPERTS
    el = jnp.where((lane >= e0) & (lane < e0 + MOE_EXPERTS), lg, neg)
    m1 = jnp.max(el, axis=-1, keepdims=True)
    i1 = jnp.min(jnp.where(el == m1, lane, big), axis=-1, keepdims=True)
    el2 = jnp.where(lane == i1, neg, el)
    m2 = jnp.max(el2, axis=-1, keepdims=True)
    i2 = jnp.min(jnp.where(el2 == m2, lane, big), axis=-1, keepdims=True)
    r = jnp.exp(m2 - m1)
    w1 = 1.0 / (1.0 + r)
    w2 = r * w1
    two_hot = jnp.where((lane == i1) | (lane == i2), 1.0, 0.0)
    tm = lg.shape[0]
    ltri = (lax.broadcasted_iota(jnp.int32, (tm, tm), 1) < lax.broadcasted_iota(jnp.int32, (tm, tm), 0))
    rank = carry_ref[0:1, :] + _dot(ltri.astype(BF16), two_hot.astype(BF16))
    pick = lambda idx: jnp.sum(jnp.where(lane == idx, rank, 0.0), axis=-1, keepdims=True)
    rec = jnp.zeros_like(lg)
    for col, val in ((R_E1, i1), (R_E2, i2), (R_RANK1, pick(i1)), (R_RANK2, pick(i2)),
                     (R_W1, w1 * g_prob), (R_W2, w2 * g_prob)):
        rec = jnp.where(lane == col, val, rec)
    carry_ref[...] = carry_ref[...] + jnp.sum(two_hot, axis=0, keepdims=True)
    return rec


def _plan_kernel(cnt_ref, seg_ref, tile_ref):
    cnt = cnt_ref[...]
    lane = lax.broadcasted_iota(jnp.int32, cnt.shape, 1)
    is_e = (lane >= EXPERT_LANE0) & (lane < EXPERT_LANE0 + N_EXPERTS)
    size = jnp.where(is_e, jnp.floor((cnt + (EXPERT_TR - 1)) * (1.0 / EXPERT_TR)) * EXPERT_TR, 0.0)
    upper = (lax.broadcasted_iota(jnp.int32, (ROUTE_LANES, ROUTE_LANES), 0)
             < lax.broadcasted_iota(jnp.int32, (ROUTE_LANES, ROUTE_LANES), 1)).astype(F32)
    start = jnp.dot(size, upper, precision=HIGHEST, preferred_element_type=F32)
    end = start + size
    total = jnp.max(end, axis=-1, keepdims=True)
    row = lax.broadcasted_iota(jnp.int32, cnt.shape, 0)
    seg = jnp.where(row == SEG_START, start, jnp.where(row == SEG_SIZE, size, jnp.where(
        row == SEG_END, end, jnp.where(row == SEG_TILES, total * (1.0 / EXPERT_TR), cnt))))
    seg_ref[...] = seg.astype(jnp.int32)
    nt = tile_ref.shape[0]
    t0 = (lax.broadcasted_iota(jnp.int32, (nt, ROUTE_LANES), 0) * EXPERT_TR).astype(F32)
    lane_t = lax.broadcasted_iota(jnp.int32, (nt, ROUTE_LANES), 1)
    done = jnp.where((lane_t >= EXPERT_LANE0) & (lane_t < EXPERT_LANE0 + N_EXPERTS) & (end[0:1, :] <= t0), 1.0, 0.0)
    te = jnp.minimum(jnp.sum(done, axis=-1, keepdims=True), N_EXPERTS - 1.0)
    tile_ref[...] = jnp.broadcast_to(te, (nt, ROUTE_LANES)).astype(jnp.int32)


def _plan(counts, n_tiles):
    nt = -(-n_tiles // 8) * 8
    return pl.pallas_call(
        _plan_kernel,
        out_shape=[jax.ShapeDtypeStruct((8, ROUTE_LANES), jnp.int32),
                   jax.ShapeDtypeStruct((nt, ROUTE_LANES), jnp.int32)],
        name="plan",
    )(counts)


def _positions_kernel(route_ref, seg_ref, pos_ref):
    rec = route_ref[...]
    start = seg_ref[0:1, :].astype(F32)
    lane = lax.broadcasted_iota(jnp.int32, rec.shape, 1).astype(F32)
    seg_start = lambda col: jnp.sum(jnp.where(lane == rec[:, col:col + 1], start, 0.0), axis=-1, keepdims=True)
    p1 = seg_start(R_E1) + rec[:, R_RANK1:R_RANK1 + 1]
    p2 = seg_start(R_E2) + rec[:, R_RANK2:R_RANK2 + 1]
    both = jnp.where(lane == 0.0, p1, jnp.where(lane == 1.0, p2, 0.0))
    pos_ref[...] = jnp.transpose(both)[0:8, :].astype(jnp.int32)


def _positions(route, seg):
    m = route.shape[0]
    return pl.pallas_call(
        _positions_kernel, grid=(m // ROUTE_TM,),
        in_specs=[pl.BlockSpec((ROUTE_TM, ROUTE_LANES), lambda i: (i, 0)),
                  pl.BlockSpec((8, ROUTE_LANES), lambda i: (0, 0))],
        out_specs=pl.BlockSpec((8, ROUTE_TM), lambda i: (0, i)),
        out_shape=jax.ShapeDtypeStruct((8, m), jnp.int32),
        compiler_params=_cparams("parallel"), name="positions",
    )(route, seg)


SEG_START, SEG_SIZE, SEG_END, SEG_TILES, SEG_COUNT = range(5)


def _row_copies(src_of, dst_of, sem):
    copies = [pltpu.make_async_copy(src_of(j, k), dst_of(j, k), sem) for j in range(MOVE_TM) for k in (0, 1)]
    for n, c in enumerate(copies):
        c.start(priority=n % 2)
    for c in copies:
        c.wait()


def _dispatch_kernel(seg_s, pos_s, h2_ref, xs_ref, zbuf, zsem, sem, *, n_tiles):
    @pl.when(pl.program_id(0) == 0)
    def _():
        zbuf[...] = jnp.zeros_like(zbuf)

        def zero_tiles(go):
            for e in range(N_EXPERTS):
                lane = EXPERT_LANE0 + e

                @pl.when(seg_s[SEG_SIZE, lane] > 0)
                def _():
                    start = pl.multiple_of(seg_s[SEG_END, lane] - EXPERT_TR, EXPERT_TR)
                    go(pltpu.make_async_copy(zbuf, xs_ref.at[pl.ds(start, EXPERT_TR)], zsem.at[e]))

                tile = n_tiles - N_EXPERTS + e

                @pl.when(tile >= seg_s[SEG_TILES, 0])
                def _():
                    dst = xs_ref.at[pl.ds(tile * EXPERT_TR, EXPERT_TR)]
                    go(pltpu.make_async_copy(zbuf, dst, zsem.at[N_EXPERTS + e]))

        zero_tiles(lambda c: c.start())
        zero_tiles(lambda c: c.wait())

    _row_copies(lambda j, k: h2_ref.at[pl.ds(j, 1)], lambda j, k: xs_ref.at[pl.ds(pos_s[k, j], 1)], sem)


def _dispatch(h2, pos, seg, n_tiles):
    m, d = h2.shape
    return pl.pallas_call(
        functools.partial(_dispatch_kernel, n_tiles=n_tiles),
        grid_spec=pltpu.PrefetchScalarGridSpec(
            num_scalar_prefetch=1, grid=(m // MOVE_TM,),
            in_specs=[pl.BlockSpec((8, MOVE_TM), lambda i, seg: (0, i), memory_space=pltpu.SMEM),
                      pl.BlockSpec((MOVE_TM, d), lambda i, seg: (i, 0))],
            out_specs=pl.BlockSpec(memory_space=pl.ANY),
            scratch_shapes=[pltpu.VMEM((EXPERT_TR, d), F32), pltpu.SemaphoreType.DMA((2 * N_EXPERTS,)),
                            pltpu.SemaphoreType.DMA]),
        out_shape=jax.ShapeDtypeStruct((n_tiles * EXPERT_TR, d), F32),
        compiler_params=_cparams("arbitrary"), name="dispatch",
    )(seg, pos, h2)


def _experts_kernel(te_s, seg_s, xs_ref, wg_ref, wu_ref, wd_ref, ys_ref):
    used = pl.program_id(0) < seg_s[SEG_TILES, 0]

    @pl.when(used)
    def _():
        x = xs_ref[...].astype(BF16)
        a = _dot(x, wg_ref[...].astype(BF16))
        b = _dot(x, wu_ref[...].astype(BF16))
        ys_ref[...] = _dot((_silu(a) * b).astype(BF16), wd_ref[...].astype(BF16))

    @pl.when(jnp.logical_not(used))
    def _():
        ys_ref[...] = jnp.zeros_like(ys_ref)


def _experts(xs, tile_expert, seg, w_gate, w_up, w_down, layer):
    p, d = xs.shape
    f = w_gate.shape[-1]
    last = lambda j, seg: jnp.minimum(j, seg[SEG_TILES, 0] - 1)
    wmap = lambda j, te, seg: (layer * N_EXPERTS + te[last(j, seg)], 0, 0)
    wspec = lambda shape: pl.BlockSpec(shape, wmap)
    return pl.pallas_call(
        _experts_kernel,
        grid_spec=pltpu.PrefetchScalarGridSpec(
            num_scalar_prefetch=2, grid=(p // EXPERT_TR,),
            in_specs=[pl.BlockSpec((EXPERT_TR, d), lambda j, te, seg: (last(j, seg), 0)),
                      wspec((None, d, f)), wspec((None, d, f)), wspec((None, f, d))],
            out_specs=pl.BlockSpec((EXPERT_TR, d), lambda j, te, seg: (j, 0))),
        out_shape=jax.ShapeDtypeStruct((p, d), F32),
        compiler_params=_cparams("arbitrary"), name="experts",
    )(tile_expert, seg, xs, w_gate, w_up, w_down)


def _combine_kernel(pos_s, route_ref, x1_ref, mod_ref, nw_ref, ys_ref, o_ref, y1buf, y2buf, sem, *, final_norm):
    bufs = (y1buf, y2buf)
    _row_copies(lambda j, k: ys_ref.at[pl.ds(pos_s[k, j], 1)], lambda j, k: bufs[k].at[pl.ds(j, 1)], sem)
    rec = route_ref[...]
    moe = rec[:, R_W1:R_W1 + 1] * y1buf[...] + rec[:, R_W2:R_W2 + 1] * y2buf[...]
    x = x1_ref[...] + mod_ref[5:6, :] * moe
    if final_norm:
        x = x * lax.rsqrt(jnp.mean(x * x, axis=-1, keepdims=True) + EPS) * nw_ref[...]
    o_ref[...] = x


def _combine(ys, pos, route, x1, mod, nw_final, rows_per_mod, final_norm):
    m, d = x1.shape
    row = lambda i: (i, 0)
    return pl.pallas_call(
        functools.partial(_combine_kernel, final_norm=final_norm),
        grid=(m // MOVE_TM,),
        in_specs=[pl.BlockSpec((8, MOVE_TM), lambda i: (0, i), memory_space=pltpu.SMEM),
                  pl.BlockSpec((MOVE_TM, ROUTE_LANES), row), pl.BlockSpec((MOVE_TM, d), row),
                  pl.BlockSpec((None, 8, d), lambda i: ((i * MOVE_TM) // rows_per_mod, 0, 0)),
                  pl.BlockSpec((1, d), lambda i: (0, 0)),
                  pl.BlockSpec(memory_space=pl.ANY)],
        out_specs=pl.BlockSpec((MOVE_TM, d), row),
        out_shape=jax.ShapeDtypeStruct((m, d), F32),
        scratch_shapes=[pltpu.VMEM((MOVE_TM, d), F32), pltpu.VMEM((MOVE_TM, d), F32), pltpu.SemaphoreType.DMA],
        compiler_params=_cparams("arbitrary"), name="combine",
    )(pos, route, x1, mod, nw_final, ys)


def _layer_weights(P, l):
    d = P['w_in'].shape[1]
    cw = P['pool_scale'].shape[1]
    hw = P['hyena_conv'].shape[2]
    o1, o2, o3 = cw, cw + hw, cw + hw + 3 * cw
    na = (P['w_in'].shape[2] - o3) // 3
    w_in = P['w_in'][l]
    segs = [w_in[:, o1:o2], w_in[:, o2:o3], w_in[:, o3:o3 + na], w_in[:, o3 + na:o3 + 2 * na],
            w_in[:, o3 + 2 * na:], w_in[:, :o1]]
    widths = tuple(int(s.shape[1]) for s in segs)
    w_in_r = jnp.concatenate(segs, axis=1).astype(BF16)
    gw = cw // len(POOL_WINDOWS)
    pool_bd = jnp.zeros((cw, cw), F32)
    for g in range(len(POOL_WINDOWS)):
        pool_bd = pool_bd.at[g * gw:(g + 1) * gw, g * gw:(g + 1) * gw].set(P['pool_w'][l, g])
    wr = jnp.concatenate([P['w_route_group'][l],
                          jnp.transpose(P['w_route_exp'][l], (1, 0, 2)).reshape(d, N_EXPERTS)], axis=1)
    wr = jnp.zeros((d, ROUTE_LANES), F32).at[:, :wr.shape[1]].set(wr)
    wr_hi = wr.astype(BF16)
    wr_lo = (wr - wr_hi.astype(F32)).astype(BF16)
    br = jnp.concatenate([P['b_route_group'][l], P['b_route_exp'][l].reshape(-1)])
    br = jnp.zeros((1, ROUTE_LANES), F32).at[0, :br.shape[0]].set(br)
    return dict(
        widths=widths, w_in_r=w_in_r, norm_mix=P['norm_mix'][l][None], norm_ffn=P['norm_ffn'][l][None],
        pool_bd=pool_bd.astype(BF16), pool_scale=P['pool_scale'][l][None], sconv_w=P['sconv_w'][l],
        hyena_conv=P['hyena_conv'][l], hyena_bias=P['hyena_bias'][l],
        merge=(P['w_gate'][l].astype(BF16), P['b_gate'][l][None], P['w_br_a'][l].astype(BF16),
               P['w_br_b'][l].astype(BF16), P['w_br_c'][l].astype(BF16), P['w_br_d'][l].astype(BF16),
               P['w_out'][l].astype(BF16), jnp.stack([wr_hi, wr_lo]), br),
    )


def _run_stream(x3, mods, LW, EW, nw_final, hy, attend, depth):
    b, seq_len, d = x3.shape
    m = b * seq_len
    x = x3.reshape(m, d)
    rows_per_mod = m // mods.shape[1]
    tm = min(512, seq_len)
    tc = min(256, seq_len)
    kvs = []
    for l in range(depth):
        W = LW[l]
        mod = mods[l]
        cw = W['pool_scale'].shape[1]
        h, ph, ps, q, k, v, pp = _inproj(x, mod, W['norm_mix'], W['w_in_r'], W['widths'],
                                         min(INPROJ_TM, rows_per_mod), rows_per_mod)
        kvs.append((k, v))
        ya, yc, u_t, x0_t = _local_mixers(pp, ps, ph, W['pool_bd'], W['pool_scale'], W['sconv_w'],
                                          W['hyena_conv'], seq_len, tc)
        yb_t = _hyena_conv(u_t, x0_t, hy['fwd_bf'], hy['inv_c'], hy['inv_s'], hy['spec'][l],
                           W['hyena_bias'], cw)
        yd = attend(l, q, k, v)
        x1, h2, route, counts = _merge(x, h, ya, yb_t, yc, yd, mod, W['norm_ffn'], W['merge'], tm,
                                       rows_per_mod, seq_len)
        n_tiles = 2 * m // EXPERT_TR + N_EXPERTS
        seg, tile_tab = _plan(counts, n_tiles)
        pos = _positions(route, seg)
        xs = _dispatch(h2, pos, seg, n_tiles)
        ys = _experts(xs, tile_tab[:, 0], seg, EW[0], EW[1], EW[2], l)
        x = _combine(ys, pos, route, x1, mod, nw_final, rows_per_mod, final_norm=(l == depth - 1))
    return x, kvs


def _hyena_setup(seq_len, P, depth):
    fwd32, inv_c, inv_s = _dft_matrices(seq_len)
    spec = []
    for l in range(depth):
        filt = _hyena_filter(seq_len, P['hyena_f1'][l], P['hyena_fb1'][l], P['hyena_f2'][l],
                             P['hyena_fb2'][l], P['hyena_f3'][l], P['hyena_freq'][l], P['hyena_decay'][l])
        spec.append(_filter_spectrum(fwd32, filt))
    return dict(fwd_bf=fwd32.astype(BF16), inv_c=inv_c, inv_s=inv_s, spec=spec)


def kernel(x_prompt, x_sample, cache_k, cache_v, c, c_ctx, w_ada, b_ada, norm_mix, w_in, w_gate, b_gate, pool_w, pool_scale, hyena_conv, hyena_f1, hyena_fb1, hyena_f2, hyena_fb2, hyena_f3, hyena_freq, hyena_decay, hyena_bias, sconv_w, na_rpb, w_br_a, w_br_b, w_br_c, w_br_d, w_out, norm_ffn, w_route_group, b_route_group, w_route_exp, b_route_exp, w_e_gate, w_e_up, w_e_down, norm_final):
    P = dict(w_in=w_in, w_gate=w_gate, b_gate=b_gate, pool_w=pool_w, pool_scale=pool_scale,
             hyena_conv=hyena_conv, hyena_f1=hyena_f1, hyena_fb1=hyena_fb1, hyena_f2=hyena_f2,
             hyena_fb2=hyena_fb2, hyena_f3=hyena_f3, hyena_freq=hyena_freq, hyena_decay=hyena_decay,
             hyena_bias=hyena_bias, sconv_w=sconv_w, w_br_a=w_br_a, w_br_b=w_br_b, w_br_c=w_br_c,
             w_br_d=w_br_d, w_out=w_out, norm_mix=norm_mix, norm_ffn=norm_ffn,
             w_route_group=w_route_group, b_route_group=b_route_group, w_route_exp=w_route_exp,
             b_route_exp=b_route_exp, w_e_gate=w_e_gate, w_e_up=w_e_up, w_e_down=w_e_down)
    depth, d, _ = w_ada.shape
    bp, lp, _ = x_prompt.shape
    bs, ls, _ = x_sample.shape
    assert (ls // GRID_W) % LAT_QROWS == 0 and ls // GRID_W >= LAT_KROWS

    n_c = 1 + bs
    n_pad = -(-n_c // 8) * 8
    cvecs = jnp.zeros((n_pad, d), F32).at[0].set(c_ctx).at[1:n_c].set(c)
    ada = _ada(cvecs, w_ada, b_ada).reshape(depth, n_pad, 6, d)
    ada = jnp.concatenate([ada, jnp.zeros((depth, n_pad, 2, d), F32)], axis=2)

    LW = [_layer_weights(P, l) for l in range(depth)]
    nw_final = norm_final[None]
    f = w_e_gate.shape[-1]
    EW = (w_e_gate.reshape(depth * N_EXPERTS, d, f), w_e_up.reshape(depth * N_EXPERTS, d, f),
          w_e_down.reshape(depth * N_EXPERTS, f, d))

    hy_p = _hyena_setup(lp, P, depth)
    xp, kv_p = _run_stream(x_prompt, ada[:, 0:1], LW, EW, nw_final, hy_p,
                           lambda l, q, k, v: _context_attention(q, k, v, lp), depth)
    y_prompt = xp.reshape(bp, lp, d)
    new_k = jnp.stack([k.reshape(bp, lp, N_HEADS, HEAD_DIM) for k, _ in kv_p], axis=1)
    new_v = jnp.stack([v.reshape(bp, lp, N_HEADS, HEAD_DIM) for _, v in kv_p], axis=1)

    hy_s = _hyena_setup(ls, P, depth)
    past = cache_k.shape[2]
    ck = cache_k.reshape(bs, depth, past, N_HEADS * HEAD_DIM)
    cv = cache_v.reshape(bs, depth, past, N_HEADS * HEAD_DIM)
    biases = [_latent_bias(na_rpb[l], ls // GRID_W) for l in range(depth)]
    xs, _ = _run_stream(x_sample, ada[:, 1:n_c], LW, EW, nw_final, hy_s,
                        lambda l, q, k, v: _latent_attention(q, k, v, ck, cv, l, biases[l], ls),
                        depth)
    y_sample = xs.reshape(bs, ls, d)
    return (y_prompt, y_sample, new_k, new_v)
```

```python
import functools
import math

import numpy as np
import jax
import jax.numpy as jnp
from jax import lax
from jax.experimental import pallas as pl
from jax.experimental.pallas import tpu as pltpu

F32 = jnp.float32
BF16 = jnp.bfloat16
HIGHEST = lax.Precision.HIGHEST

EPS = 1e-6
GRID_W = 64
NA_ROWS = 8
NA_COLS = 16
N_HEADS = 8
HEAD_DIM = 64
POOL_WINDOWS = (2, 4, 8, 16)
HYENA_BANDS = 16
MOE_GROUPS = 4
MOE_EXPERTS = 8
N_EXPERTS = MOE_GROUPS * MOE_EXPERTS
ROUTE_LANES = 128
EXPERT_LANE0 = MOE_GROUPS
HALO = 8
NEG_BIG = -1e30
VMEM_LIMIT_BYTES = 48 * 1024 * 1024


def _cparams(*sem):
    return pltpu.CompilerParams(dimension_semantics=sem, vmem_limit_bytes=VMEM_LIMIT_BYTES)


def _resident(shape):
    nd = len(shape)
    return pl.BlockSpec(shape, lambda *_: (0,) * nd, pipeline_mode=pl.Buffered(1))


def _dot(a, b):
    return jnp.dot(a, b, preferred_element_type=F32)


def _silu(x):
    return x * jax.nn.sigmoid(x)


def _ada_kernel(cv_ref, w_ref, b_ref, o_ref):
    o_ref[...] = jnp.dot(_silu(cv_ref[...]), w_ref[...], precision=HIGHEST,
                         preferred_element_type=F32) + b_ref[...]


def _ada(cvecs, w_ada, b_ada):
    depth, d, d6 = w_ada.shape
    r = cvecs.shape[0]
    return pl.pallas_call(
        _ada_kernel,
        grid=(depth, d6 // d),
        in_specs=[pl.BlockSpec((r, d), lambda l, j: (0, 0)),
                  pl.BlockSpec((None, d, d), lambda l, j: (l, 0, j)),
                  pl.BlockSpec((None, 1, d), lambda l, j: (l, 0, j))],
        out_specs=pl.BlockSpec((None, r, d), lambda l, j: (l, 0, j)),
        out_shape=jax.ShapeDtypeStruct((depth, r, d6), F32),
        compiler_params=_cparams("parallel", "parallel"),
        name="ada",
    )(cvecs, w_ada, b_ada.reshape(depth, 1, d6))


def _rms_mod(x, nw, shift, scale):
    y = x * lax.rsqrt(jnp.mean(x * x, axis=-1, keepdims=True) + EPS) * nw
    return y * (1.0 + scale) + shift


def _inproj_kernel(x_ref, mod_ref, nw_ref, w_ref, h_ref, ph_ref, ps_ref, q_ref, k_ref, v_ref, pp_ref,
                   *, cols):
    mod = mod_ref[...]
    h = _rms_mod(x_ref[...], nw_ref[...], mod[0:1, :], mod[1:2, :]).astype(BF16)
    h_ref[...] = h
    for ref, (off, wd) in zip((ph_ref, ps_ref, q_ref, k_ref, v_ref, pp_ref), cols):
        r = _dot(h, w_ref[:, off:off + wd])
        if ref is q_ref:
            r = r * (HEAD_DIM ** -0.5)
        ref[...] = r.astype(ref.dtype)


def _inproj(x, mod, nw, w_in, cols, tm, rows_per_mod):
    m, d = x.shape
    n = w_in.shape[1]
    widths = [wd for _, wd in cols]
    dts = (F32, F32, BF16, F32, F32, F32)
    row = lambda i: (i, 0)
    return pl.pallas_call(
        functools.partial(_inproj_kernel, cols=cols),
        grid=(m // tm,),
        in_specs=[pl.BlockSpec((tm, d), row),
                  pl.BlockSpec((None, 8, d), lambda i: ((i * tm) // rows_per_mod, 0, 0)),
                  pl.BlockSpec((1, d), lambda i: (0, 0)),
                  _resident((d, n))],
        out_specs=[pl.BlockSpec((tm, d), row)] + [pl.BlockSpec((tm, wd), row) for wd in widths],
        out_shape=[jax.ShapeDtypeStruct((m, d), BF16)]
        + [jax.ShapeDtypeStruct((m, wd), dt) for wd, dt in zip(widths, dts)],
        compiler_params=_cparams("parallel"),
        name="inproj",
    )(x, mod, nw, w_in)


def _fill_padded(pad_ref, prev_ref, cur_ref, next_ref, first, last, tc):
    zero = jnp.zeros((HALO, cur_ref.shape[1]), F32)
    pad_ref[0:HALO, :] = jnp.where(first, zero, prev_ref[...])
    pad_ref[HALO:HALO + tc, :] = cur_ref[...]
    pad_ref[HALO + tc:2 * HALO + tc, :] = jnp.where(last, zero, next_ref[...])


def _local_kernel(pp_ref, pp_prev, pp_next, ps_ref, ps_prev, ps_next, ph_ref, ph_prev, ph_next,
                  pw_ref, pscale_ref, sw_ref, hw_ref,
                  ya_ref, yc_ref, u_ref, x0_ref,
                  pad_p, pad_s, pad_h, *, seq_len, tc):
    nchunk = seq_len // tc
    j = pl.program_id(0) % nchunk
    first = j == 0
    last = j == nchunk - 1
    cw = pp_ref.shape[1]
    sw = cw

    _fill_padded(pad_p, pp_prev, pp_ref, pp_next, first, last, tc)
    sh = lambda k: pad_p[HALO + k:HALO + k + tc, :]
    u = pp_ref[...]
    sums = {}
    acc = u
    lo_done, hi_done = 0, 0
    for win in POOL_WINDOWS:
        lo, hi = -(win // 2), win // 2 - 1
        for k in range(lo, lo_done):
            acc = acc + sh(k)
        for k in range(hi_done + 1, hi + 1):
            acc = acc + sh(k)
        lo_done, hi_done = lo, hi
        sums[win] = acc
    t = j * tc + lax.broadcasted_iota(jnp.int32, (tc, 1), 0)
    lane = lax.broadcasted_iota(jnp.int32, (1, cw), 1)
    gw = cw // len(POOL_WINDOWS)
    pooled = None
    for g, win in reversed(list(enumerate(POOL_WINDOWS))):
        cnt = jnp.minimum(t - win // 2 + win, seq_len) - jnp.maximum(t - win // 2, 0)
        val = sums[win] * (1.0 / cnt.astype(F32))
        pooled = val if pooled is None else jnp.where(lane < (g + 1) * gw, val, pooled)
    pooled = pooled - u
    ya = _dot(pooled.astype(BF16), pw_ref[...]) * pscale_ref[...]
    ya_ref[...] = ya.astype(BF16)

    _fill_padded(pad_s, ps_prev, ps_ref, ps_next, first, last, tc)
    w3 = sw_ref[...]
    z = lambda k: (pad_s[HALO + k:HALO + k + tc, 2 * sw:3 * sw] * pad_s[HALO + k:HALO + k + tc, 0:sw])
    conv = w3[0:1, :] * z(-1) + w3[1:2, :] * z(0) + w3[2:3, :] * z(1)
    yc_ref[...] = (ps_ref[:, sw:2 * sw] * conv).astype(BF16)

    _fill_padded(pad_h, ph_prev, ph_ref, ph_next, first, last, tc)
    hw = hw_ref[...]
    c3 = (hw[0:1, :] * pad_h[HALO - 1:HALO - 1 + tc, :] + hw[1:2, :] * pad_h[HALO:HALO + tc, :]
          + hw[2:3, :] * pad_h[HALO + 1:HALO + 1 + tc, :])
    x0_ref[...] = c3[:, 0:sw]
    u_ref[...] = c3[:, sw:2 * sw] * c3[:, 2 * sw:3 * sw]


def _local_mixers(pp, ps, ph, pool_bd, pool_scale, sconv_w, hyena_conv, seq_len, tc):
    m, cw = pp.shape
    nb = m // seq_len
    nchunk = seq_len // tc
    hpc = tc // HALO
    nhb = m // HALO
    row = lambda i: (i, 0)
    prev = lambda i: (jnp.maximum(i * hpc - 1, 0), 0)
    nxt = lambda i: (jnp.minimum((i + 1) * hpc, nhb - 1), 0)
    tl = lambda i: (i % nchunk, i // nchunk)
    const = lambda i: (0, 0)

    def trio(width):
        return [pl.BlockSpec((tc, width), row), pl.BlockSpec((HALO, width), prev),
                pl.BlockSpec((HALO, width), nxt)]

    return pl.pallas_call(
        functools.partial(_local_kernel, seq_len=seq_len, tc=tc),
        grid=(m // tc,),
        in_specs=trio(cw) + trio(3 * cw) + trio(3 * cw)
        + [pl.BlockSpec((cw, cw), const), pl.BlockSpec((1, cw), const),
           pl.BlockSpec((3, cw), const), pl.BlockSpec((3, 3 * cw), const)],
        out_specs=[pl.BlockSpec((tc, cw), row), pl.BlockSpec((tc, cw), row),
                   pl.BlockSpec((tc, cw), tl), pl.BlockSpec((tc, cw), tl)],
        out_shape=[jax.ShapeDtypeStruct((m, cw), BF16), jax.ShapeDtypeStruct((m, cw), BF16),
                   jax.ShapeDtypeStruct((seq_len, nb * cw), F32),
                   jax.ShapeDtypeStruct((seq_len, nb * cw), F32)],
        scratch_shapes=[pltpu.VMEM((tc + 2 * HALO, cw), F32), pltpu.VMEM((tc + 2 * HALO, 3 * cw), F32),
                        pltpu.VMEM((tc + 2 * HALO, 3 * cw), F32)],
        compiler_params=_cparams("parallel"),
        name="local_mixers",
    )(pp, pp, pp, ps, ps, ps, ph, ph, ph, pool_bd, pool_scale, sconv_w, hyena_conv)


def _trig(rows, cols, n):
    split = 64
    r = np.asarray(rows, np.int64)[:, None]
    c = np.asarray(cols, np.int64)
    assert c[0] % split == 0 and len(c) % split == 0 and np.all(np.diff(c) == 1)
    c0 = np.arange(split)[None, :]
    c1 = c[::split][None, :]
    ang0 = ((r * c0) % n) * (2.0 * math.pi / n)
    ang1 = ((r * c1) % n) * (2.0 * math.pi / n)
    tab = lambda a: jnp.asarray(a, F32)
    ca, sa = tab(np.cos(ang0))[:, None, :], tab(np.sin(ang0))[:, None, :]
    cb, sb = tab(np.cos(ang1))[:, :, None], tab(np.sin(ang1))[:, :, None]
    shape = (len(rows), len(c))
    return (ca * cb - sa * sb).reshape(shape), (sa * cb + ca * sb).reshape(shape)


def _dft_matrices(seq_len):
    L = seq_len
    n = 2 * L
    k = np.arange(L)
    cos_f, sin_f = _trig(k, k, n)
    first = jnp.asarray(k == 0)
    alt = jnp.asarray(np.where(k % 2 == 0, 1.0, -1.0), F32)
    im = jnp.where(first[:, None], alt[None, :], -sin_f)
    fwd = jnp.concatenate([cos_f, im], axis=0)
    tp = np.arange(L // 2, L // 2 + L)
    cos_i, sin_i = _trig(tp, k, n)
    alt_t = jnp.asarray(np.where(tp % 2 == 0, 1.0, -1.0) / n, F32)
    inv_c = jnp.where(first[None, :], 1.0 / n, cos_i * (2.0 / n))
    inv_s = jnp.where(first[None, :], alt_t[:, None], sin_i * (-2.0 / n))
    return fwd, inv_c.astype(BF16), inv_s.astype(BF16)


def _hyena_embedding(seq_len, width):
    t = np.arange(seq_len, dtype=np.float64)
    w = (2.0 * math.pi / seq_len) * t
    bands = np.linspace(1e-4, HYENA_BANDS - 1, HYENA_BANDS)
    z = np.concatenate([(t / (seq_len - 1))[:, None], np.cos(w[:, None] * bands),
                        -np.sin(w[:, None] * bands)], axis=-1)
    out = np.zeros((seq_len, width), np.float32)
    out[:, :z.shape[1]] = z
    return jnp.asarray(out)


def _filter_kernel(z_ref, f1_ref, fb1_ref, f2_ref, fb2_ref, f3_ref, freq_ref, decay_ref, o_ref, *, seq_len):
    hdot = lambda a, b: jnp.dot(a, b, precision=HIGHEST, preferred_element_type=F32)
    fr = freq_ref[...]
    hdn = jnp.sin(fr * (hdot(z_ref[...], f1_ref[...]) + fb1_ref[...]))
    hdn = jnp.sin(fr * (hdot(hdn, f2_ref[...]) + fb2_ref[...]))
    filt = hdot(hdn, f3_ref[...])
    t = lax.broadcasted_iota(jnp.int32, (seq_len, 1), 0)
    dist = jnp.abs(t - seq_len // 2).astype(F32) / (seq_len / 2)
    filt = filt * jnp.exp(-dist * decay_ref[...])
    o_ref[...] = filt / jnp.sum(jnp.abs(filt), axis=0, keepdims=True)


def _pad2(a, rows, cols):
    return jnp.zeros((rows, cols), F32).at[:a.shape[0], :a.shape[1]].set(a)


def _hyena_filter(seq_len, f1, fb1, f2, fb2, f3, freq, decay):
    p = 128
    c = f3.shape[1]
    args = (_hyena_embedding(seq_len, p), _pad2(f1, p, p), _pad2(fb1[None], 1, p), _pad2(f2, p, p),
            _pad2(fb2[None], 1, p), _pad2(f3, p, c), _pad2(freq[None], 1, p), decay[None])
    return pl.pallas_call(
        functools.partial(_filter_kernel, seq_len=seq_len),
        out_shape=jax.ShapeDtypeStruct((seq_len, c), F32),
        compiler_params=pltpu.CompilerParams(vmem_limit_bytes=VMEM_LIMIT_BYTES),
        name="hyena_filter",
    )(*args)


def _spectrum_kernel(f_ref, h_ref, o_ref):
    o_ref[...] = jnp.dot(f_ref[...], h_ref[...], precision=HIGHEST, preferred_element_type=F32)


def _filter_spectrum(fwd32, filt):
    n, L = fwd32.shape
    c = filt.shape[1]
    tf = min(n, 512)
    return pl.pallas_call(
        _spectrum_kernel,
        grid=(n // tf,),
        in_specs=[pl.BlockSpec((tf, L), lambda i: (i, 0)), pl.BlockSpec((L, c), lambda i: (0, 0))],
        out_specs=pl.BlockSpec((tf, c), lambda i: (i, 0)),
        out_shape=jax.ShapeDtypeStruct((n, c), F32),
        compiler_params=_cparams("parallel"),
        name="filter_spectrum",
    )(fwd32, filt)


def _hy_fwd_kernel(fc_ref, fs_ref, u_ref, hr_ref, hi_ref, yr_ref, yi_ref, *, cw):
    ub = u_ref[...].astype(BF16)
    ur = _dot(fc_ref[...], ub)
    ui = _dot(fs_ref[...], ub)
    tf = ur.shape[0]
    row0 = (pl.program_id(0) * tf + lax.broadcasted_iota(jnp.int32, (tf, 1), 0)) == 0
    hr, hi = hr_ref[...], hi_ref[...]
    for s in range(ur.shape[1] // cw):
        a, b = ur[:, s * cw:(s + 1) * cw], ui[:, s * cw:(s + 1) * cw]
        yr = jnp.where(row0, a * hr, a * hr - b * hi)
        yi = jnp.where(row0, b * hi, a * hi + b * hr)
        yr_ref[:, s * cw:(s + 1) * cw] = yr.astype(BF16)
        yi_ref[:, s * cw:(s + 1) * cw] = yi.astype(BF16)


def _hy_inv_kernel(ic_ref, is_ref, yr_ref, yi_ref, u_ref, x0_ref, bias_ref, o_ref):
    conv = _dot(ic_ref[...], yr_ref[...]) + _dot(is_ref[...], yi_ref[...])
    o_ref[...] = ((conv + bias_ref[...] * u_ref[...]) * x0_ref[...]).astype(BF16)


def _hyena_conv(u_t, x0_t, fwd_bf, inv_c, inv_s, spec, bias, cw):
    L, ncol = u_t.shape
    tf = min(L, 1024)
    tn = min(ncol, 512)
    nf = L // tf
    yr, yi = pl.pallas_call(
        functools.partial(_hy_fwd_kernel, cw=cw),
        grid=(nf, ncol // tn),
        in_specs=[pl.BlockSpec((tf, L), lambda i, j: (i, 0)),
                  pl.BlockSpec((tf, L), lambda i, j: (nf + i, 0)),
                  pl.BlockSpec((L, tn), lambda i, j: (0, j)),
                  pl.BlockSpec((tf, cw), lambda i, j: (i, 0)),
                  pl.BlockSpec((tf, cw), lambda i, j: (nf + i, 0))],
        out_specs=[pl.BlockSpec((tf, tn), lambda i, j: (i, j))] * 2,
        out_shape=[jax.ShapeDtypeStruct((L, ncol), BF16)] * 2,
        compiler_params=_cparams("parallel", "parallel"),
        name="hyena_dft",
    )(fwd_bf, fwd_bf, u_t, spec, spec)
    bias_t = jnp.tile(bias[None, :], (1, tn // cw))
    return pl.pallas_call(
        _hy_inv_kernel,
        grid=(nf, ncol // tn),
        in_specs=[pl.BlockSpec((tf, L), lambda i, j: (i, 0)),
                  pl.BlockSpec((tf, L), lambda i, j: (i, 0)),
                  pl.BlockSpec((L, tn), lambda i, j: (0, j)),
                  pl.BlockSpec((L, tn), lambda i, j: (0, j)),
                  pl.BlockSpec((tf, tn), lambda i, j: (i, j)),
                  pl.BlockSpec((tf, tn), lambda i, j: (i, j)),
                  pl.BlockSpec((1, tn), lambda i, j: (0, 0))],
        out_specs=pl.BlockSpec((tf, tn), lambda i, j: (i, j)),
        out_shape=jax.ShapeDtypeStruct((L, ncol), BF16),
        compiler_params=_cparams("parallel", "parallel"),
        name="hyena_idft",
    )(inv_c, inv_s, yr, yi, u_t, x0_t, bias_t)


def _qkt(q, k):
    return lax.dot_general(q, k, (((1,), (1,)), ((), ())), preferred_element_type=F32)


PAIR = 2 * HEAD_DIM


def _attend_pair(q_pair, parts, bias_of=None):
    low = lax.broadcasted_iota(jnp.int32, (1, PAIR), 1) < HEAD_DIM
    one = jnp.ones((), BF16)
    accs = []
    for half in (0, 1):
        mine = low if half == 0 else jnp.logical_not(low)
        qh = jnp.where(mine, q_pair, jnp.zeros((), BF16))
        scores = []
        for i, (k, _) in enumerate(parts):
            s = _qkt(qh, k)
            b = None if bias_of is None else bias_of(half, i)
            scores.append(s if b is None else s + b)
        m = None
        for s in scores:
            mi = jnp.max(s, axis=-1, keepdims=True)
            m = mi if m is None else jnp.maximum(m, mi)
        acc = None
        for s, (_, v) in zip(scores, parts):
            p = jnp.exp((s - m).astype(BF16))
            o = _dot(p, jnp.where(mine, v, one))
            acc = o if acc is None else acc + o
        accs.append(acc)
    num = jnp.where(low, accs[0], accs[1])
    den = pltpu.roll(jnp.where(low, accs[1], accs[0]), HEAD_DIM, axis=1)
    return num * (1.0 / den)


def _ctx_attn_kernel(q_ref, k_ref, v_ref, o_ref):
    for j in range(N_HEADS // 2):
        sl = slice(j * PAIR, (j + 1) * PAIR)
        parts = [(k_ref[:, sl].astype(BF16), v_ref[:, sl].astype(BF16))]
        o_ref[:, sl] = _attend_pair(q_ref[:, sl], parts).astype(BF16)


def _context_attention(q, k, v, seq_len):
    m, w = q.shape
    spec = pl.BlockSpec((seq_len, w), lambda b: (b, 0))
    return pl.pallas_call(
        _ctx_attn_kernel,
        grid=(m // seq_len,),
        in_specs=[spec, spec, spec],
        out_specs=spec,
        out_shape=jax.ShapeDtypeStruct((m, w), BF16),
        compiler_params=_cparams("parallel"),
        name="context_attention",
    )(q, k, v)


LAT_QROWS = 4
LAT_KROWS = NA_ROWS + LAT_QROWS
LAT_TQ = LAT_QROWS * GRID_W


def _lat_attn_kernel(q_ref, k0_ref, k1_ref, k2_ref, v0_ref, v1_ref, v2_ref, ck_ref, cv_ref, g_ref, o_ref,
                     *, rows):
    nblk = rows // LAT_QROWS
    rblk = pl.program_id(0)
    kblk = jnp.clip(rblk - 1, 0, nblk - 3)
    slot = {}
    for ri in range(LAT_QROWS):
        r = rblk * LAT_QROWS + ri
        rs = jnp.clip(r - NA_ROWS // 2, 0, rows - NA_ROWS)
        for kj in range(LAT_KROWS):
            krow = kblk * LAT_QROWS + kj
            in_window = (krow >= rs) & (krow < rs + NA_ROWS)
            slot[ri, kj] = jnp.where(in_window, krow - r + NA_ROWS - 1, 2 * NA_ROWS - 1)
    low_half = lax.broadcasted_iota(jnp.int32, (GRID_W, 2 * GRID_W), 1) < GRID_W

    def bias_block(h, i):
        row_blocks = []
        for ri in range(LAT_QROWS):
            tiles = []
            for kj in range(LAT_QROWS * i, LAT_QROWS * (i + 1), 2):
                tiles.append(jnp.where(low_half, g_ref[h, slot[ri, kj]], g_ref[h, slot[ri, kj + 1]]))
            row_blocks.append(jnp.concatenate(tiles, axis=1))
        return jnp.concatenate(row_blocks, axis=0)

    n_local = 3
    for j in range(N_HEADS // 2):
        sl = slice(j * PAIR, (j + 1) * PAIR)
        parts = [(kr[:, sl].astype(BF16), vr[:, sl].astype(BF16))
                 for kr, vr in ((k0_ref, v0_ref), (k1_ref, v1_ref), (k2_ref, v2_ref), (ck_ref, cv_ref))]
        bias_of = lambda half, i, j=j: bias_block(2 * j + half, i) if i < n_local else None
        o_ref[:, sl] = _attend_pair(q_ref[:, sl], parts, bias_of).astype(BF16)


def _latent_bias(rpb, rows):
    del rows
    nh, ndr, ndc = rpb.shape
    qc = np.arange(GRID_W)[:, None]
    kc = (np.arange(2 * GRID_W) % GRID_W)[None, :]
    ws = np.clip(qc - NA_COLS // 2, 0, GRID_W - NA_COLS)
    col_ok = ((kc >= ws) & (kc < ws + NA_COLS)).reshape(-1)
    dc = np.clip(kc - qc + NA_COLS - 1, 0, ndc - 1).reshape(-1)
    onehot = jnp.asarray((dc[None, :] == np.arange(ndc)[:, None]) & col_ok[None, :], F32)
    g = jnp.einsum('hrd,dx->hrx', rpb, onehot, precision=HIGHEST)
    g = g + jnp.asarray(np.where(col_ok, 0.0, NEG_BIG), F32)
    g = jnp.concatenate([g, jnp.full((nh, 2 * NA_ROWS - ndr, g.shape[-1]), NEG_BIG, F32)], axis=1)
    return g.reshape(nh, 2 * NA_ROWS, GRID_W, 2 * GRID_W)


def _latent_attention(q, k, v, ctx_k, ctx_v, layer, bias, seq_len):
    m, w = q.shape
    rows = seq_len // GRID_W
    nblk = rows // LAT_QROWS
    nb = m // seq_len
    tq = LAT_TQ
    nctx = ctx_k.shape[2]

    def kspec(i):
        return pl.BlockSpec((tq, w), lambda r, b: (b * nblk + jnp.clip(r - 1, 0, nblk - 3) + i, 0))

    qspec = pl.BlockSpec((tq, w), lambda r, b: (b * nblk + r, 0))
    cspec = pl.BlockSpec((None, None, nctx, w), lambda r, b: (b, layer, 0, 0))
    return pl.pallas_call(
        functools.partial(_lat_attn_kernel, rows=rows),
        grid=(nblk, nb),
        in_specs=[qspec, kspec(0), kspec(1), kspec(2), kspec(0), kspec(1), kspec(2), cspec, cspec,
                  _resident(bias.shape)],
        out_specs=qspec,
        out_shape=jax.ShapeDtypeStruct((m, w), BF16),
        compiler_params=_cparams("parallel", "parallel"),
        name="latent_attention",
    )(q, k, k, k, v, v, v, ctx_k, ctx_v, bias)


def _merge_kernel(x_ref, h_ref, ya_ref, yb_ref, yc_ref, yd_ref, mod_ref, nw_ref,
                  wg_ref, bg_ref, wa_ref, wb_ref, wc_ref, wd_ref, wo_ref, wr_ref, br_ref,
                  x1_ref, h2_ref, route_ref, cnt_ref, carry_ref):
    @pl.when(pl.program_id(0) == 0)
    def _():
        carry_ref[...] = jnp.zeros_like(carry_ref)

    d = x_ref.shape[1]
    h = h_ref[...]
    merged = None
    for i, (y_ref, w_ref) in enumerate(((ya_ref, wa_ref), (yb_ref, wb_ref), (yc_ref, wc_ref), (yd_ref, wd_ref))):
        gate = jax.nn.sigmoid(_dot(h, wg_ref[:, i * d:(i + 1) * d]) + bg_ref[:, i * d:(i + 1) * d])
        term = gate * _dot(y_ref[...], w_ref[...])
        merged = term if merged is None else merged + term
    mod = mod_ref[...]
    x1 = x_ref[...] + mod[2:3, :] * _dot(merged.astype(BF16), wo_ref[...])
    x1_ref[...] = x1
    h2 = _rms_mod(x1, nw_ref[...], mod[3:4, :], mod[4:5, :])
    h2_ref[...] = h2
    h2_hi = h2.astype(BF16)
    h2_lo = (h2 - h2_hi.astype(F32)).astype(BF16)
    logits = (_dot(h2_hi, wr_ref[0]) + _dot(h2_lo, wr_ref[0]) + _dot(h2_hi, wr_ref[1])) + br_ref[...]
    route_ref[...] = _route_tile(logits, carry_ref)
    cnt_ref[...] = carry_ref[...]


def _merge(x, h, ya, yb_t, yc, yd, mod, nw, wts, tm, rows_per_mod, seq_len):
    m, d = x.shape
    cw = ya.shape[1]
    nchunk = seq_len // tm
    row = lambda i: (i, 0)
    const2 = lambda i: (0, 0)
    w_specs = [_resident(w.shape) for w in wts]
    return pl.pallas_call(
        _merge_kernel,
        grid=(m // tm,),
        in_specs=[pl.BlockSpec((tm, d), row), pl.BlockSpec((tm, d), row),
                  pl.BlockSpec((tm, cw), row),
                  pl.BlockSpec((tm, cw), lambda i: (i % nchunk, i // nchunk)),
                  pl.BlockSpec((tm, cw), row),
                  pl.BlockSpec((tm, yd.shape[1]), row),
                  pl.BlockSpec((None, 8, d), lambda i: ((i * tm) // rows_per_mod, 0, 0)),
                  pl.BlockSpec((1, d), const2)] + w_specs,
        out_specs=[pl.BlockSpec((tm, d), row), pl.BlockSpec((tm, d), row),
                   pl.BlockSpec((tm, ROUTE_LANES), row), pl.BlockSpec((8, ROUTE_LANES), const2)],
        out_shape=[jax.ShapeDtypeStruct((m, d), F32), jax.ShapeDtypeStruct((m, d), F32),
                   jax.ShapeDtypeStruct((m, ROUTE_LANES), F32), jax.ShapeDtypeStruct((8, ROUTE_LANES), F32)],
        scratch_shapes=[pltpu.VMEM((8, ROUTE_LANES), F32)],
        compiler_params=_cparams("arbitrary"),
        name="merge",
    )(x, h, ya, yb_t, yc, yd, mod, nw, *wts)


ROUTE_TM = 512
EXPERT_TR = 256
MOVE_TM = 512
INPROJ_TM = 1024
R_E1, R_E2, R_RANK1, R_RANK2, R_W1, R_W2 = range(6)


def _route_tile(lg, carry_ref):
    lane = lax.broadcasted_iota(jnp.int32, lg.shape, 1).astype(F32)
    neg = jnp.float32(-jnp.inf)
    big = jnp.float32(ROUTE_LANES)
    gl = jnp.where(lane < MOE_GROUPS, lg, neg)
    gm = jnp.max(gl, axis=-1, keepdims=True)
    g_prob = 1.0 / jnp.sum(jnp.exp(gl - gm), axis=-1, keepdims=True)
    gidx = jnp.min(jnp.where(gl == gm, lane, big), axis=-1, keepdims=True)
    e0 = EXPERT_LANE0 + gidx * MOE_EXPERTS
    el = jnp.where((lane >= e0) & (lane < e0 + MOE_EXPERTS), lg, neg)
    m1 = jnp.max(el, axis=-1, keepdims=True)
    i1 = jnp.min(jnp.where(el == m1, lane, big), axis=-1, keepdims=True)
    el2 = jnp.where(lane == i1, neg, el)
    m2 = jnp.max(el2, axis=-1, keepdims=True)
    i2 = jnp.min(jnp.where(el2 == m2, lane, big), axis=-1, keepdims=True)
    r = jnp.exp(m2 - m1)
    w1 = 1.0 / (1.0 + r)
    w2 = r * w1
    two_hot = jnp.where((lane == i1) | (lane == i2), 1.0, 0.0)
    tm = lg.shape[0]
    ltri = (lax.broadcasted_iota(jnp.int32, (tm, tm), 1) < lax.broadcasted_iota(jnp.int32, (tm, tm), 0))
    rank = carry_ref[0:1, :] + _dot(ltri.astype(BF16), two_hot.astype(BF16))
    pick = lambda idx: jnp.sum(jnp.where(lane == idx, rank, 0.0), axis=-1, keepdims=True)
    rec = jnp.zeros_like(lg)
    for col, val in ((R_E1, i1), (R_E2, i2), (R_RANK1, pick(i1)), (R_RANK2, pick(i2)),
                     (R_W1, w1 * g_prob), (R_W2, w2 * g_prob)):
        rec = jnp.where(lane == col, val, rec)
    carry_ref[...] = carry_ref[...] + jnp.sum(two_hot, axis=0, keepdims=True)
    return rec


def _plan_kernel(cnt_ref, seg_ref):
    cnt = cnt_ref[...]
    lane = lax.broadcasted_iota(jnp.int32, cnt.shape, 1)
    is_e = (lane >= EXPERT_LANE0) & (lane < EXPERT_LANE0 + N_EXPERTS)
    size = jnp.where(is_e, jnp.floor((cnt + (EXPERT_TR - 1)) * (1.0 / EXPERT_TR)) * EXPERT_TR, 0.0)
    upper = (lax.broadcasted_iota(jnp.int32, (ROUTE_LANES, ROUTE_LANES), 0)
             < lax.broadcasted_iota(jnp.int32, (ROUTE_LANES, ROUTE_LANES), 1)).astype(F32)
    start = jnp.dot(size, upper, precision=HIGHEST, preferred_element_type=F32)
    end = start + size
    total = jnp.max(end, axis=-1, keepdims=True)
    row = lax.broadcasted_iota(jnp.int32, cnt.shape, 0)
    seg = jnp.where(row == SEG_START, start, jnp.where(row == SEG_SIZE, size, jnp.where(
        row == SEG_END, end, jnp.where(row == SEG_TILES, total * (1.0 / EXPERT_TR), cnt))))
    seg_ref[...] = seg.astype(jnp.int32)


def _plan(counts):
    return pl.pallas_call(
        _plan_kernel, out_shape=jax.ShapeDtypeStruct((8, ROUTE_LANES), jnp.int32), name="plan",
    )(counts)


def _positions_kernel(route_ref, seg_ref, pos_ref):
    rec = route_ref[...]
    start = seg_ref[0:1, :].astype(F32)
    lane = lax.broadcasted_iota(jnp.int32, rec.shape, 1).astype(F32)
    seg_start = lambda col: jnp.sum(jnp.where(lane == rec[:, col:col + 1], start, 0.0), axis=-1, keepdims=True)
    p1 = seg_start(R_E1) + rec[:, R_RANK1:R_RANK1 + 1]
    p2 = seg_start(R_E2) + rec[:, R_RANK2:R_RANK2 + 1]
    both = jnp.where(lane == 0.0, p1, jnp.where(lane == 1.0, p2, 0.0))
    pos_ref[...] = jnp.transpose(both)[0:8, :].astype(jnp.int32)


def _positions(route, seg):
    m = route.shape[0]
    return pl.pallas_call(
        _positions_kernel, grid=(m // ROUTE_TM,),
        in_specs=[pl.BlockSpec((ROUTE_TM, ROUTE_LANES), lambda i: (i, 0)),
                  pl.BlockSpec((8, ROUTE_LANES), lambda i: (0, 0))],
        out_specs=pl.BlockSpec((8, ROUTE_TM), lambda i: (0, i)),
        out_shape=jax.ShapeDtypeStruct((8, m), jnp.int32),
        compiler_params=_cparams("parallel"), name="positions",
    )(route, seg)


SEG_START, SEG_SIZE, SEG_END, SEG_TILES, SEG_COUNT = range(5)


def _row_copies(src_of, dst_of, sem):
    copies = [pltpu.make_async_copy(src_of(j, k), dst_of(j, k), sem) for j in range(MOVE_TM) for k in (0, 1)]
    for n, c in enumerate(copies):
        c.start(priority=n % 2)
    for c in copies:
        c.wait()


def _dispatch_kernel(seg_s, pos_s, h2_ref, xs_ref, zbuf, zsem, sem, *, n_tiles):
    @pl.when(pl.program_id(0) == 0)
    def _():
        zbuf[...] = jnp.zeros_like(zbuf)

        def zero_tiles(go):
            for e in range(N_EXPERTS):
                lane = EXPERT_LANE0 + e

                @pl.when(seg_s[SEG_SIZE, lane] > 0)
                def _():
                    start = pl.multiple_of(seg_s[SEG_END, lane] - EXPERT_TR, EXPERT_TR)
                    go(pltpu.make_async_copy(zbuf, xs_ref.at[pl.ds(start, EXPERT_TR)], zsem.at[e]))

                tile = n_tiles - N_EXPERTS + e

                @pl.when(tile >= seg_s[SEG_TILES, 0])
                def _():
                    dst = xs_ref.at[pl.ds(tile * EXPERT_TR, EXPERT_TR)]
                    go(pltpu.make_async_copy(zbuf, dst, zsem.at[N_EXPERTS + e]))

        zero_tiles(lambda c: c.start())
        zero_tiles(lambda c: c.wait())

    _row_copies(lambda j, k: h2_ref.at[pl.ds(j, 1)], lambda j, k: xs_ref.at[pl.ds(pos_s[k, j], 1)], sem)


def _dispatch(h2, pos, seg, n_tiles):
    m, d = h2.shape
    return pl.pallas_call(
        functools.partial(_dispatch_kernel, n_tiles=n_tiles),
        grid_spec=pltpu.PrefetchScalarGridSpec(
            num_scalar_prefetch=1, grid=(m // MOVE_TM,),
            in_specs=[pl.BlockSpec((8, MOVE_TM), lambda i, seg: (0, i), memory_space=pltpu.SMEM),
                      pl.BlockSpec((MOVE_TM, d), lambda i, seg: (i, 0))],
            out_specs=pl.BlockSpec(memory_space=pl.ANY),
            scratch_shapes=[pltpu.VMEM((EXPERT_TR, d), F32), pltpu.SemaphoreType.DMA((2 * N_EXPERTS,)),
                            pltpu.SemaphoreType.DMA]),
        out_shape=jax.ShapeDtypeStruct((n_tiles * EXPERT_TR, d), F32),
        compiler_params=_cparams("arbitrary"), name="dispatch",
    )(seg, pos, h2)


def _experts_kernel(seg_s, xs_ref, wg_ref, wu_ref, wd_ref, ys_ref,
                    xbuf, ybuf, wgb, wub, wdb, sem_in, sem_out, zsem, *, n_tiles):
    e = pl.program_id(0)
    lane = EXPERT_LANE0 + e
    first = seg_s[SEG_START, lane]
    ntile = lax.shift_right_logical(seg_s[SEG_SIZE, lane], EXPERT_TR.bit_length() - 1)

    def rows(t):
        return pl.ds(pl.multiple_of(first + t * EXPERT_TR, EXPERT_TR), EXPERT_TR)

    def load(t, slot):
        return pltpu.make_async_copy(xs_ref.at[rows(t)], xbuf.at[slot], sem_in.at[slot])

    def store(t, slot):
        return pltpu.make_async_copy(ybuf.at[slot], ys_ref.at[rows(t)], sem_out.at[slot])

    @pl.when(ntile > 0)
    def _():
        wgb[...] = wg_ref[...].astype(BF16)
        wub[...] = wu_ref[...].astype(BF16)
        wdb[...] = wd_ref[...].astype(BF16)
        load(0, 0).start()

        def body(t, carry):
            slot = lax.rem(t, 2)
            load(t, slot).wait()

            @pl.when(t + 1 < ntile)
            def _():
                load(t + 1, 1 - slot).start()

            @pl.when(t >= 2)
            def _():
                store(t - 2, slot).wait()

            x = xbuf[slot].astype(BF16)
            hid = _silu(_dot(x, wgb[...])) * _dot(x, wub[...])
            ybuf[slot] = _dot(hid.astype(BF16), wdb[...])
            store(t, slot).start()
            return carry

        lax.fori_loop(0, ntile, body, 0)

        @pl.when(ntile >= 2)
        def _():
            store(ntile - 2, lax.rem(ntile, 2)).wait()
        store(ntile - 1, lax.rem(ntile - 1, 2)).wait()

    @pl.when(e == N_EXPERTS - 1)
    def _():
        xbuf[0] = jnp.zeros(xbuf.shape[1:], F32)

        def zero_tail(go):
            for k in range(N_EXPERTS):
                tile = n_tiles - N_EXPERTS + k

                @pl.when(tile >= seg_s[SEG_TILES, 0])
                def _():
                    go(pltpu.make_async_copy(xbuf.at[0], ys_ref.at[pl.ds(tile * EXPERT_TR, EXPERT_TR)], zsem.at[k]))

        zero_tail(lambda c: c.start())
        zero_tail(lambda c: c.wait())


def _experts(xs, seg, w_gate, w_up, w_down, layer):
    p, d = xs.shape
    f = w_gate.shape[-1]
    wmap = lambda e, seg: (layer * N_EXPERTS + e, 0, 0)
    return pl.pallas_call(
        functools.partial(_experts_kernel, n_tiles=p // EXPERT_TR),
        grid_spec=pltpu.PrefetchScalarGridSpec(
            num_scalar_prefetch=1, grid=(N_EXPERTS,),
            in_specs=[pl.BlockSpec(memory_space=pl.ANY),
                      pl.BlockSpec((None, d, f), wmap), pl.BlockSpec((None, d, f), wmap),
                      pl.BlockSpec((None, f, d), wmap)],
            out_specs=pl.BlockSpec(memory_space=pl.ANY),
            scratch_shapes=[pltpu.VMEM((2, EXPERT_TR, d), F32), pltpu.VMEM((2, EXPERT_TR, d), F32),
                            pltpu.VMEM((d, f), BF16), pltpu.VMEM((d, f), BF16), pltpu.VMEM((f, d), BF16),
                            pltpu.SemaphoreType.DMA((2,)), pltpu.SemaphoreType.DMA((2,)),
                            pltpu.SemaphoreType.DMA((N_EXPERTS,))]),
        out_shape=jax.ShapeDtypeStruct((p, d), F32),
        compiler_params=_cparams("arbitrary"), name="experts",
    )(seg, xs, w_gate, w_up, w_down)


def _combine_kernel(pos_s, route_ref, x1_ref, mod_ref, nw_ref, ys_ref, o_ref, y1buf, y2buf, sem, *, final_norm):
    bufs = (y1buf, y2buf)
    _row_copies(lambda j, k: ys_ref.at[pl.ds(pos_s[k, j], 1)], lambda j, k: bufs[k].at[pl.ds(j, 1)], sem)
    rec = route_ref[...]
    moe = rec[:, R_W1:R_W1 + 1] * y1buf[...] + rec[:, R_W2:R_W2 + 1] * y2buf[...]
    x = x1_ref[...] + mod_ref[5:6, :] * moe
    if final_norm:
        x = x * lax.rsqrt(jnp.mean(x * x, axis=-1, keepdims=True) + EPS) * nw_ref[...]
    o_ref[...] = x


def _combine(ys, pos, route, x1, mod, nw_final, rows_per_mod, final_norm):
    m, d = x1.shape
    row = lambda i: (i, 0)
    return pl.pallas_call(
        functools.partial(_combine_kernel, final_norm=final_norm),
        grid=(m // MOVE_TM,),
        in_specs=[pl.BlockSpec((8, MOVE_TM), lambda i: (0, i), memory_space=pltpu.SMEM),
                  pl.BlockSpec((MOVE_TM, ROUTE_LANES), row), pl.BlockSpec((MOVE_TM, d), row),
                  pl.BlockSpec((None, 8, d), lambda i: ((i * MOVE_TM) // rows_per_mod, 0, 0)),
                  pl.BlockSpec((1, d), lambda i: (0, 0)),
                  pl.BlockSpec(memory_space=pl.ANY)],
        out_specs=pl.BlockSpec((MOVE_TM, d), row),
        out_shape=jax.ShapeDtypeStruct((m, d), F32),
        scratch_shapes=[pltpu.VMEM((MOVE_TM, d), F32), pltpu.VMEM((MOVE_TM, d), F32), pltpu.SemaphoreType.DMA],
        compiler_params=_cparams("arbitrary"), name="combine",
    )(pos, route, x1, mod, nw_final, ys)


def _layer_weights(P, l):
    d = P['w_in'].shape[1]
    cw = P['pool_scale'].shape[1]
    hw = P['hyena_conv'].shape[2]
    o1, o2, o3 = cw, cw + hw, cw + hw + 3 * cw
    na = (P['w_in'].shape[2] - o3) // 3
    cols = ((o1, hw), (o2, 3 * cw), (o3, na), (o3 + na, na), (o3 + 2 * na, na), (0, cw))
    gw = cw // len(POOL_WINDOWS)
    pool_bd = jnp.zeros((cw, cw), F32)
    for g in range(len(POOL_WINDOWS)):
        pool_bd = pool_bd.at[g * gw:(g + 1) * gw, g * gw:(g + 1) * gw].set(P['pool_w'][l, g])
    wr = jnp.concatenate([P['w_route_group'][l],
                          jnp.transpose(P['w_route_exp'][l], (1, 0, 2)).reshape(d, N_EXPERTS)], axis=1)
    wr = jnp.zeros((d, ROUTE_LANES), F32).at[:, :wr.shape[1]].set(wr)
    wr_hi = wr.astype(BF16)
    wr_lo = (wr - wr_hi.astype(F32)).astype(BF16)
    br = jnp.concatenate([P['b_route_group'][l], P['b_route_exp'][l].reshape(-1)])
    br = jnp.zeros((1, ROUTE_LANES), F32).at[0, :br.shape[0]].set(br)
    return dict(
        cols=cols, w_in=P['w_in'][l].astype(BF16), norm_mix=P['norm_mix'][l][None], norm_ffn=P['norm_ffn'][l][None],
        pool_bd=pool_bd.astype(BF16), pool_scale=P['pool_scale'][l][None], sconv_w=P['sconv_w'][l],
        hyena_conv=P['hyena_conv'][l], hyena_bias=P['hyena_bias'][l],
        merge=(P['w_gate'][l].astype(BF16), P['b_gate'][l][None], P['w_br_a'][l].astype(BF16),
               P['w_br_b'][l].astype(BF16), P['w_br_c'][l].astype(BF16), P['w_br_d'][l].astype(BF16),
               P['w_out'][l].astype(BF16), jnp.stack([wr_hi, wr_lo]), br),
    )


def _run_stream(x3, mods, LW, EW, nw_final, hy, attend, depth):
    b, seq_len, d = x3.shape
    m = b * seq_len
    x = x3.reshape(m, d)
    rows_per_mod = m // mods.shape[1]
    tm = min(512, seq_len)
    tc = min(256, seq_len)
    kvs = []
    for l in range(depth):
        W = LW[l]
        mod = mods[l]
        cw = W['pool_scale'].shape[1]
        h, ph, ps, q, k, v, pp = _inproj(x, mod, W['norm_mix'], W['w_in'], W['cols'],
                                         min(INPROJ_TM, rows_per_mod), rows_per_mod)
        kvs.append((k, v))
        ya, yc, u_t, x0_t = _local_mixers(pp, ps, ph, W['pool_bd'], W['pool_scale'], W['sconv_w'],
                                          W['hyena_conv'], seq_len, tc)
        yb_t = _hyena_conv(u_t, x0_t, hy['fwd_bf'], hy['inv_c'], hy['inv_s'], hy['spec'][l],
                           W['hyena_bias'], cw)
        yd = attend(l, q, k, v)
        x1, h2, route, counts = _merge(x, h, ya, yb_t, yc, yd, mod, W['norm_ffn'], W['merge'], tm,
                                       rows_per_mod, seq_len)
        n_tiles = 2 * m // EXPERT_TR + N_EXPERTS
        seg = _plan(counts)
        pos = _positions(route, seg)
        xs = _dispatch(h2, pos, seg, n_tiles)
        ys = _experts(xs, seg, EW[0], EW[1], EW[2], l)
        x = _combine(ys, pos, route, x1, mod, nw_final, rows_per_mod, final_norm=(l == depth - 1))
    return x, kvs


def _hyena_setup(seq_len, P, depth):
    fwd32, inv_c, inv_s = _dft_matrices(seq_len)
    spec = []
    for l in range(depth):
        filt = _hyena_filter(seq_len, P['hyena_f1'][l], P['hyena_fb1'][l], P['hyena_f2'][l],
                             P['hyena_fb2'][l], P['hyena_f3'][l], P['hyena_freq'][l], P['hyena_decay'][l])
        spec.append(_filter_spectrum(fwd32, filt))
    return dict(fwd_bf=fwd32.astype(BF16), inv_c=inv_c, inv_s=inv_s, spec=spec)


def kernel(x_prompt, x_sample, cache_k, cache_v, c, c_ctx, w_ada, b_ada, norm_mix, w_in, w_gate, b_gate, pool_w, pool_scale, hyena_conv, hyena_f1, hyena_fb1, hyena_f2, hyena_fb2, hyena_f3, hyena_freq, hyena_decay, hyena_bias, sconv_w, na_rpb, w_br_a, w_br_b, w_br_c, w_br_d, w_out, norm_ffn, w_route_group, b_route_group, w_route_exp, b_route_exp, w_e_gate, w_e_up, w_e_down, norm_final):
    P = dict(w_in=w_in, w_gate=w_gate, b_gate=b_gate, pool_w=pool_w, pool_scale=pool_scale,
             hyena_conv=hyena_conv, hyena_f1=hyena_f1, hyena_fb1=hyena_fb1, hyena_f2=hyena_f2,
             hyena_fb2=hyena_fb2, hyena_f3=hyena_f3, hyena_freq=hyena_freq, hyena_decay=hyena_decay,
             hyena_bias=hyena_bias, sconv_w=sconv_w, w_br_a=w_br_a, w_br_b=w_br_b, w_br_c=w_br_c,
             w_br_d=w_br_d, w_out=w_out, norm_mix=norm_mix, norm_ffn=norm_ffn,
             w_route_group=w_route_group, b_route_group=b_route_group, w_route_exp=w_route_exp,
             b_route_exp=b_route_exp, w_e_gate=w_e_gate, w_e_up=w_e_up, w_e_down=w_e_down)
    depth, d, _ = w_ada.shape
    bp, lp, _ = x_prompt.shape
    bs, ls, _ = x_sample.shape
    assert (ls // GRID_W) % LAT_QROWS == 0 and ls // GRID_W >= LAT_KROWS

    n_c = 1 + bs
    n_pad = -(-n_c // 8) * 8
    cvecs = jnp.zeros((n_pad, d), F32).at[0].set(c_ctx).at[1:n_c].set(c)
    ada = _ada(cvecs, w_ada, b_ada).reshape(depth, n_pad, 6, d)
    ada = jnp.concatenate([ada, jnp.zeros((depth, n_pad, 2, d), F32)], axis=2)

    LW = [_layer_weights(P, l) for l in range(depth)]
    nw_final = norm_final[None]
    f = w_e_gate.shape[-1]
    EW = (w_e_gate.reshape(depth * N_EXPERTS, d, f), w_e_up.reshape(depth * N_EXPERTS, d, f),
          w_e_down.reshape(depth * N_EXPERTS, f, d))

    hy_p = _hyena_setup(lp, P, depth)
    xp, kv_p = _run_stream(x_prompt, ada[:, 0:1], LW, EW, nw_final, hy_p,
                           lambda l, q, k, v: _context_attention(q, k, v, lp), depth)
    y_prompt = xp.reshape(bp, lp, d)
    new_k = jnp.stack([k.reshape(bp, lp, N_HEADS, HEAD_DIM) for k, _ in kv_p], axis=1)
    new_v = jnp.stack([v.reshape(bp, lp, N_HEADS, HEAD_DIM) for _, v in kv_p], axis=1)

    hy_s = _hyena_setup(ls, P, depth)
    past = cache_k.shape[2]
    ck = cache_k.reshape(bs, depth, past, N_HEADS * HEAD_DIM)
    cv = cache_v.reshape(bs, depth, past, N_HEADS * HEAD_DIM)
    biases = [_latent_bias(na_rpb[l], ls // GRID_W) for l in range(depth)]
    xs, _ = _run_stream(x_sample, ada[:, 1:n_c], LW, EW, nw_final, hy_s,
                        lambda l, q, k, v: _latent_attention(q, k, v, ck, cv, l, biases[l], ls),
                        depth)
    y_sample = xs.reshape(bs, ls, d)
    return (y_prompt, y_sample, new_k, new_v)
```

```python
import functools
import math

import numpy as np
import jax
import jax.numpy as jnp
from jax import lax
from jax.experimental import pallas as pl
from jax.experimental.pallas import tpu as pltpu

F32 = jnp.float32
BF16 = jnp.bfloat16
HIGHEST = lax.Precision.HIGHEST

EPS = 1e-6
GRID_W = 64
NA_ROWS = 8
NA_COLS = 16
N_HEADS = 8
HEAD_DIM = 64
POOL_WINDOWS = (2, 4, 8, 16)
HYENA_BANDS = 16
MOE_GROUPS = 4
MOE_EXPERTS = 8
N_EXPERTS = MOE_GROUPS * MOE_EXPERTS
ROUTE_LANES = 128
EXPERT_LANE0 = MOE_GROUPS
HALO = 8
NEG_BIG = -1e30
VMEM_LIMIT_BYTES = 48 * 1024 * 1024


def _cparams(*sem):
    return pltpu.CompilerParams(dimension_semantics=sem, vmem_limit_bytes=VMEM_LIMIT_BYTES)


def _resident(shape):
    nd = len(shape)
    return pl.BlockSpec(shape, lambda *_: (0,) * nd, pipeline_mode=pl.Buffered(1))


def _dot(a, b):
    return jnp.dot(a, b, preferred_element_type=F32)


def _silu(x):
    return x * jax.nn.sigmoid(x)


def _ada_kernel(cv_ref, w_ref, b_ref, o_ref):
    o_ref[...] = jnp.dot(_silu(cv_ref[...]), w_ref[...], precision=HIGHEST,
                         preferred_element_type=F32) + b_ref[...]


def _ada(cvecs, w_ada, b_ada):
    depth, d, d6 = w_ada.shape
    r = cvecs.shape[0]
    return pl.pallas_call(
        _ada_kernel,
        grid=(depth, d6 // d),
        in_specs=[pl.BlockSpec((r, d), lambda l, j: (0, 0)),
                  pl.BlockSpec((None, d, d), lambda l, j: (l, 0, j)),
                  pl.BlockSpec((None, 1, d), lambda l, j: (l, 0, j))],
        out_specs=pl.BlockSpec((None, r, d), lambda l, j: (l, 0, j)),
        out_shape=jax.ShapeDtypeStruct((depth, r, d6), F32),
        compiler_params=_cparams("parallel", "parallel"),
        name="ada",
    )(cvecs, w_ada, b_ada.reshape(depth, 1, d6))


def _rms_mod(x, nw, shift, scale):
    y = x * lax.rsqrt(jnp.mean(x * x, axis=-1, keepdims=True) + EPS) * nw
    return y * (1.0 + scale) + shift


def _inproj_kernel(x_ref, mod_ref, nw_ref, w_ref, h_ref, ph_ref, ps_ref, q_ref, k_ref, v_ref, pp_ref,
                   *, cols):
    mod = mod_ref[...]
    h = _rms_mod(x_ref[...], nw_ref[...], mod[0:1, :], mod[1:2, :]).astype(BF16)
    h_ref[...] = h
    for ref, (off, wd) in zip((ph_ref, ps_ref, q_ref, k_ref, v_ref, pp_ref), cols):
        r = _dot(h, w_ref[:, off:off + wd])
        if ref is q_ref:
            r = r * (HEAD_DIM ** -0.5)
        ref[...] = r.astype(ref.dtype)


def _inproj(x, mod, nw, w_in, cols, tm, rows_per_mod):
    m, d = x.shape
    n = w_in.shape[1]
    widths = [wd for _, wd in cols]
    dts = (F32, F32, BF16, F32, F32, F32)
    row = lambda i: (i, 0)
    return pl.pallas_call(
        functools.partial(_inproj_kernel, cols=cols),
        grid=(m // tm,),
        in_specs=[pl.BlockSpec((tm, d), row),
                  pl.BlockSpec((None, 8, d), lambda i: ((i * tm) // rows_per_mod, 0, 0)),
                  pl.BlockSpec((1, d), lambda i: (0, 0)),
                  _resident((d, n))],
        out_specs=[pl.BlockSpec((tm, d), row)] + [pl.BlockSpec((tm, wd), row) for wd in widths],
        out_shape=[jax.ShapeDtypeStruct((m, d), BF16)]
        + [jax.ShapeDtypeStruct((m, wd), dt) for wd, dt in zip(widths, dts)],
        compiler_params=_cparams("parallel"),
        name="inproj",
    )(x, mod, nw, w_in)


def _fill_padded(pad_ref, prev_ref, cur_ref, next_ref, first, last, tc):
    zero = jnp.zeros((HALO, cur_ref.shape[1]), F32)
    pad_ref[0:HALO, :] = jnp.where(first, zero, prev_ref[...])
    pad_ref[HALO:HALO + tc, :] = cur_ref[...]
    pad_ref[HALO + tc:2 * HALO + tc, :] = jnp.where(last, zero, next_ref[...])


def _local_kernel(pp_ref, pp_prev, pp_next, ps_ref, ps_prev, ps_next, ph_ref, ph_prev, ph_next,
                  pw_ref, pscale_ref, sw_ref, hw_ref,
                  ya_ref, yc_ref, u_ref, x0_ref,
                  pad_p, pad_s, pad_h, *, seq_len, tc):
    nchunk = seq_len // tc
    j = pl.program_id(0) % nchunk
    first = j == 0
    last = j == nchunk - 1
    cw = pp_ref.shape[1]
    sw = cw

    _fill_padded(pad_p, pp_prev, pp_ref, pp_next, first, last, tc)
    sh = lambda k: pad_p[HALO + k:HALO + k + tc, :]
    u = pp_ref[...]
    sums = {}
    acc = u
    lo_done, hi_done = 0, 0
    for win in POOL_WINDOWS:
        lo, hi = -(win // 2), win // 2 - 1
        for k in range(lo, lo_done):
            acc = acc + sh(k)
        for k in range(hi_done + 1, hi + 1):
            acc = acc + sh(k)
        lo_done, hi_done = lo, hi
        sums[win] = acc
    t = j * tc + lax.broadcasted_iota(jnp.int32, (tc, 1), 0)
    lane = lax.broadcasted_iota(jnp.int32, (1, cw), 1)
    gw = cw // len(POOL_WINDOWS)
    pooled = None
    for g, win in reversed(list(enumerate(POOL_WINDOWS))):
        cnt = jnp.minimum(t - win // 2 + win, seq_len) - jnp.maximum(t - win // 2, 0)
        val = sums[win] * (1.0 / cnt.astype(F32))
        pooled = val if pooled is None else jnp.where(lane < (g + 1) * gw, val, pooled)
    pooled = pooled - u
    ya = _dot(pooled.astype(BF16), pw_ref[...]) * pscale_ref[...]
    ya_ref[...] = ya.astype(BF16)

    _fill_padded(pad_s, ps_prev, ps_ref, ps_next, first, last, tc)
    w3 = sw_ref[...]
    z = lambda k: (pad_s[HALO + k:HALO + k + tc, 2 * sw:3 * sw] * pad_s[HALO + k:HALO + k + tc, 0:sw])
    conv = w3[0:1, :] * z(-1) + w3[1:2, :] * z(0) + w3[2:3, :] * z(1)
    yc_ref[...] = (ps_ref[:, sw:2 * sw] * conv).astype(BF16)

    _fill_padded(pad_h, ph_prev, ph_ref, ph_next, first, last, tc)
    hw = hw_ref[...]
    c3 = (hw[0:1, :] * pad_h[HALO - 1:HALO - 1 + tc, :] + hw[1:2, :] * pad_h[HALO:HALO + tc, :]
          + hw[2:3, :] * pad_h[HALO + 1:HALO + 1 + tc, :])
    x0_ref[...] = c3[:, 0:sw]
    u_ref[...] = c3[:, sw:2 * sw] * c3[:, 2 * sw:3 * sw]


def _local_mixers(pp, ps, ph, pool_bd, pool_scale, sconv_w, hyena_conv, seq_len, tc):
    m, cw = pp.shape
    nb = m // seq_len
    nchunk = seq_len // tc
    hpc = tc // HALO
    nhb = m // HALO
    row = lambda i: (i, 0)
    prev = lambda i: (jnp.maximum(i * hpc - 1, 0), 0)
    nxt = lambda i: (jnp.minimum((i + 1) * hpc, nhb - 1), 0)
    tl = lambda i: (i % nchunk, i // nchunk)
    const = lambda i: (0, 0)

    def trio(width):
        return [pl.BlockSpec((tc, width), row), pl.BlockSpec((HALO, width), prev),
                pl.BlockSpec((HALO, width), nxt)]

    return pl.pallas_call(
        functools.partial(_local_kernel, seq_len=seq_len, tc=tc),
        grid=(m // tc,),
        in_specs=trio(cw) + trio(3 * cw) + trio(3 * cw)
        + [pl.BlockSpec((cw, cw), const), pl.BlockSpec((1, cw), const),
           pl.BlockSpec((3, cw), const), pl.BlockSpec((3, 3 * cw), const)],
        out_specs=[pl.BlockSpec((tc, cw), row), pl.BlockSpec((tc, cw), row),
                   pl.BlockSpec((tc, cw), tl), pl.BlockSpec((tc, cw), tl)],
        out_shape=[jax.ShapeDtypeStruct((m, cw), BF16), jax.ShapeDtypeStruct((m, cw), BF16),
                   jax.ShapeDtypeStruct((seq_len, nb * cw), F32),
                   jax.ShapeDtypeStruct((seq_len, nb * cw), F32)],
        scratch_shapes=[pltpu.VMEM((tc + 2 * HALO, cw), F32), pltpu.VMEM((tc + 2 * HALO, 3 * cw), F32),
                        pltpu.VMEM((tc + 2 * HALO, 3 * cw), F32)],
        compiler_params=_cparams("parallel"),
        name="local_mixers",
    )(pp, pp, pp, ps, ps, ps, ph, ph, ph, pool_bd, pool_scale, sconv_w, hyena_conv)


def _trig(rows, cols, n):
    split = 64
    r = np.asarray(rows, np.int64)[:, None]
    c = np.asarray(cols, np.int64)
    assert c[0] % split == 0 and len(c) % split == 0 and np.all(np.diff(c) == 1)
    c0 = np.arange(split)[None, :]
    c1 = c[::split][None, :]
    ang0 = ((r * c0) % n) * (2.0 * math.pi / n)
    ang1 = ((r * c1) % n) * (2.0 * math.pi / n)
    tab = lambda a: jnp.asarray(a, F32)
    ca, sa = tab(np.cos(ang0))[:, None, :], tab(np.sin(ang0))[:, None, :]
    cb, sb = tab(np.cos(ang1))[:, :, None], tab(np.sin(ang1))[:, :, None]
    shape = (len(rows), len(c))
    return (ca * cb - sa * sb).reshape(shape), (sa * cb + ca * sb).reshape(shape)


def _dft_matrices(seq_len):
    L = seq_len
    n = 2 * L
    k = np.arange(L)
    first = jnp.asarray(k == 0)
    alt = jnp.asarray(np.where(k % 2 == 0, 1.0, -1.0), F32)
    cos2, sin2 = _trig(np.concatenate([k, k]), k, n)
    re_rows = jnp.asarray(np.arange(n) < L)[:, None]
    nyq_row = jnp.asarray(np.arange(n) == L)[:, None]
    fwd = jnp.where(re_rows, cos2, jnp.where(nyq_row, alt[None, :], -sin2))
    tp = np.arange(L // 2, L // 2 + L)
    cos_i, sin_i = _trig(tp, k, n)
    alt_t = jnp.asarray(np.where(tp % 2 == 0, 1.0, -1.0) / n, F32)
    inv_c = jnp.where(first[None, :], 1.0 / n, cos_i * (2.0 / n))
    inv_s = jnp.where(first[None, :], alt_t[:, None], sin_i * (-2.0 / n))
    return fwd, inv_c.astype(BF16), inv_s.astype(BF16)


def _hyena_embedding(seq_len, width):
    t = np.arange(seq_len, dtype=np.float64)
    w = (2.0 * math.pi / seq_len) * t
    bands = np.linspace(1e-4, HYENA_BANDS - 1, HYENA_BANDS)
    z = np.concatenate([(t / (seq_len - 1))[:, None], np.cos(w[:, None] * bands),
                        -np.sin(w[:, None] * bands)], axis=-1)
    out = np.zeros((seq_len, width), np.float32)
    out[:, :z.shape[1]] = z
    return jnp.asarray(out)


def _filter_kernel(z_ref, f1_ref, fb1_ref, f2_ref, fb2_ref, f3_ref, freq_ref, decay_ref, o_ref, *, seq_len):
    hdot = lambda a, b: jnp.dot(a, b, precision=HIGHEST, preferred_element_type=F32)
    fr = freq_ref[...]
    hdn = jnp.sin(fr * (hdot(z_ref[...], f1_ref[...]) + fb1_ref[...]))
    hdn = jnp.sin(fr * (hdot(hdn, f2_ref[...]) + fb2_ref[...]))
    filt = hdot(hdn, f3_ref[...])
    t = lax.broadcasted_iota(jnp.int32, (seq_len, 1), 0)
    dist = jnp.abs(t - seq_len // 2).astype(F32) / (seq_len / 2)
    filt = filt * jnp.exp(-dist * decay_ref[...])
    o_ref[...] = filt / jnp.sum(jnp.abs(filt), axis=0, keepdims=True)


def _pad2(a, rows, cols):
    return jnp.zeros((rows, cols), F32).at[:a.shape[0], :a.shape[1]].set(a)


def _hyena_filter(seq_len, f1, fb1, f2, fb2, f3, freq, decay):
    p = 128
    c = f3.shape[1]
    args = (_hyena_embedding(seq_len, p), _pad2(f1, p, p), _pad2(fb1[None], 1, p), _pad2(f2, p, p),
            _pad2(fb2[None], 1, p), _pad2(f3, p, c), _pad2(freq[None], 1, p), decay[None])
    return pl.pallas_call(
        functools.partial(_filter_kernel, seq_len=seq_len),
        out_shape=jax.ShapeDtypeStruct((seq_len, c), F32),
        compiler_params=pltpu.CompilerParams(vmem_limit_bytes=VMEM_LIMIT_BYTES),
        name="hyena_filter",
    )(*args)


def _spectrum_kernel(f_ref, h_ref, o_ref):
    o_ref[...] = jnp.dot(f_ref[...], h_ref[...], precision=HIGHEST, preferred_element_type=F32)


def _filter_spectrum(fwd32, filt):
    n, L = fwd32.shape
    c = filt.shape[1]
    tf = min(n, 512)
    return pl.pallas_call(
        _spectrum_kernel,
        grid=(n // tf,),
        in_specs=[pl.BlockSpec((tf, L), lambda i: (i, 0)), pl.BlockSpec((L, c), lambda i: (0, 0))],
        out_specs=pl.BlockSpec((tf, c), lambda i: (i, 0)),
        out_shape=jax.ShapeDtypeStruct((n, c), F32),
        compiler_params=_cparams("parallel"),
        name="filter_spectrum",
    )(fwd32, filt)


def _hy_fwd_kernel(fc_ref, fs_ref, u_ref, hr_ref, hi_ref, yr_ref, yi_ref, *, cw):
    ub = u_ref[...].astype(BF16)
    ur = _dot(fc_ref[...], ub)
    ui = _dot(fs_ref[...], ub)
    tf = ur.shape[0]
    row0 = (pl.program_id(0) * tf + lax.broadcasted_iota(jnp.int32, (tf, 1), 0)) == 0
    hr, hi = hr_ref[...], hi_ref[...]
    for s in range(ur.shape[1] // cw):
        a, b = ur[:, s * cw:(s + 1) * cw], ui[:, s * cw:(s + 1) * cw]
        yr = jnp.where(row0, a * hr, a * hr - b * hi)
        yi = jnp.where(row0, b * hi, a * hi + b * hr)
        yr_ref[:, s * cw:(s + 1) * cw] = yr.astype(BF16)
        yi_ref[:, s * cw:(s + 1) * cw] = yi.astype(BF16)


def _hy_inv_kernel(ic_ref, is_ref, yr_ref, yi_ref, u_ref, x0_ref, bias_ref, o_ref):
    conv = _dot(ic_ref[...], yr_ref[...]) + _dot(is_ref[...], yi_ref[...])
    o_ref[...] = ((conv + bias_ref[...] * u_ref[...]) * x0_ref[...]).astype(BF16)


def _hyena_conv(u_t, x0_t, fwd_bf, inv_c, inv_s, spec, bias, cw):
    L, ncol = u_t.shape
    tf = min(L, 1024)
    tn = min(ncol, 512)
    nf = L // tf
    yr, yi = pl.pallas_call(
        functools.partial(_hy_fwd_kernel, cw=cw),
        grid=(nf, ncol // tn),
        in_specs=[pl.BlockSpec((tf, L), lambda i, j: (i, 0)),
                  pl.BlockSpec((tf, L), lambda i, j: (nf + i, 0)),
                  pl.BlockSpec((L, tn), lambda i, j: (0, j)),
                  pl.BlockSpec((tf, cw), lambda i, j: (i, 0)),
                  pl.BlockSpec((tf, cw), lambda i, j: (nf + i, 0))],
        out_specs=[pl.BlockSpec((tf, tn), lambda i, j: (i, j))] * 2,
        out_shape=[jax.ShapeDtypeStruct((L, ncol), BF16)] * 2,
        compiler_params=_cparams("parallel", "parallel"),
        name="hyena_dft",
    )(fwd_bf, fwd_bf, u_t, spec, spec)
    bias_t = jnp.tile(bias[None, :], (1, tn // cw))
    return pl.pallas_call(
        _hy_inv_kernel,
        grid=(nf, ncol // tn),
        in_specs=[pl.BlockSpec((tf, L), lambda i, j: (i, 0)),
                  pl.BlockSpec((tf, L), lambda i, j: (i, 0)),
                  pl.BlockSpec((L, tn), lambda i, j: (0, j)),
                  pl.BlockSpec((L, tn), lambda i, j: (0, j)),
                  pl.BlockSpec((tf, tn), lambda i, j: (i, j)),
                  pl.BlockSpec((tf, tn), lambda i, j: (i, j)),
                  pl.BlockSpec((1, tn), lambda i, j: (0, 0))],
        out_specs=pl.BlockSpec((tf, tn), lambda i, j: (i, j)),
        out_shape=jax.ShapeDtypeStruct((L, ncol), BF16),
        compiler_params=_cparams("parallel", "parallel"),
        name="hyena_idft",
    )(inv_c, inv_s, yr, yi, u_t, x0_t, bias_t)


def _qkt(q, k):
    return lax.dot_general(q, k, (((1,), (1,)), ((), ())), preferred_element_type=F32)


PAIR = 2 * HEAD_DIM


def _attend_pair(q_pair, parts, bias_of=None):
    low = lax.broadcasted_iota(jnp.int32, (1, PAIR), 1) < HEAD_DIM
    one = jnp.ones((), BF16)
    accs = []
    for half in (0, 1):
        mine = low if half == 0 else jnp.logical_not(low)
        qh = jnp.where(mine, q_pair, jnp.zeros((), BF16))
        scores = []
        for i, (k, _) in enumerate(parts):
            s = _qkt(qh, k)
            b = None if bias_of is None else bias_of(half, i)
            scores.append(s if b is None else s + b)
        m = None
        for s in scores:
            mi = jnp.max(s, axis=-1, keepdims=True)
            m = mi if m is None else jnp.maximum(m, mi)
        acc = None
        for s, (_, v) in zip(scores, parts):
            p = jnp.exp((s - m).astype(BF16))
            o = _dot(p, jnp.where(mine, v, one))
            acc = o if acc is None else acc + o
        accs.append(acc)
    num = jnp.where(low, accs[0], accs[1])
    den = pltpu.roll(jnp.where(low, accs[1], accs[0]), HEAD_DIM, axis=1)
    return num * (1.0 / den)


def _ctx_attn_kernel(q_ref, k_ref, v_ref, o_ref):
    for j in range(N_HEADS // 2):
        sl = slice(j * PAIR, (j + 1) * PAIR)
        parts = [(k_ref[:, sl].astype(BF16), v_ref[:, sl].astype(BF16))]
        o_ref[:, sl] = _attend_pair(q_ref[:, sl], parts).astype(BF16)


def _context_attention(q, k, v, seq_len):
    m, w = q.shape
    spec = pl.BlockSpec((seq_len, w), lambda b: (b, 0))
    return pl.pallas_call(
        _ctx_attn_kernel,
        grid=(m // seq_len,),
        in_specs=[spec, spec, spec],
        out_specs=spec,
        out_shape=jax.ShapeDtypeStruct((m, w), BF16),
        compiler_params=_cparams("parallel"),
        name="context_attention",
    )(q, k, v)


LAT_QROWS = 4
LAT_KROWS = NA_ROWS + LAT_QROWS
LAT_TQ = LAT_QROWS * GRID_W


def _lat_attn_kernel(q_ref, k0_ref, k1_ref, k2_ref, v0_ref, v1_ref, v2_ref, ck_ref, cv_ref, g_ref, o_ref,
                     *, rows):
    nblk = rows // LAT_QROWS
    rblk = pl.program_id(0)
    kblk = jnp.clip(rblk - 1, 0, nblk - 3)
    slot = {}
    for ri in range(LAT_QROWS):
        r = rblk * LAT_QROWS + ri
        rs = jnp.clip(r - NA_ROWS // 2, 0, rows - NA_ROWS)
        for kj in range(LAT_KROWS):
            krow = kblk * LAT_QROWS + kj
            in_window = (krow >= rs) & (krow < rs + NA_ROWS)
            slot[ri, kj] = jnp.where(in_window, krow - r + NA_ROWS - 1, 2 * NA_ROWS - 1)
    low_half = lax.broadcasted_iota(jnp.int32, (GRID_W, 2 * GRID_W), 1) < GRID_W

    def bias_block(h, i):
        row_blocks = []
        for ri in range(LAT_QROWS):
            tiles = []
            for kj in range(LAT_QROWS * i, LAT_QROWS * (i + 1), 2):
                tiles.append(jnp.where(low_half, g_ref[h, slot[ri, kj]], g_ref[h, slot[ri, kj + 1]]))
            row_blocks.append(jnp.concatenate(tiles, axis=1))
        return jnp.concatenate(row_blocks, axis=0)

    n_local = 3
    for j in range(N_HEADS // 2):
        sl = slice(j * PAIR, (j + 1) * PAIR)
        parts = [(kr[:, sl].astype(BF16), vr[:, sl].astype(BF16))
                 for kr, vr in ((k0_ref, v0_ref), (k1_ref, v1_ref), (k2_ref, v2_ref), (ck_ref, cv_ref))]
        bias_of = lambda half, i, j=j: bias_block(2 * j + half, i) if i < n_local else None
        o_ref[:, sl] = _attend_pair(q_ref[:, sl], parts, bias_of).astype(BF16)


def _latent_bias(rpb, rows):
    del rows
    nh, ndr, ndc = rpb.shape
    qc = np.arange(GRID_W)[:, None]
    kc = (np.arange(2 * GRID_W) % GRID_W)[None, :]
    ws = np.clip(qc - NA_COLS // 2, 0, GRID_W - NA_COLS)
    col_ok = ((kc >= ws) & (kc < ws + NA_COLS)).reshape(-1)
    dc = np.clip(kc - qc + NA_COLS - 1, 0, ndc - 1).reshape(-1)
    onehot = jnp.asarray((dc[None, :] == np.arange(ndc)[:, None]) & col_ok[None, :], F32)
    rpb_ext = jnp.concatenate([rpb, jnp.zeros((nh, 2 * NA_ROWS - ndr, ndc), F32)], axis=1)
    g = jnp.einsum('hrd,dx->hrx', rpb_ext, onehot, precision=HIGHEST)
    mask = np.where(col_ok[None, :] & (np.arange(2 * NA_ROWS) < ndr)[:, None], 0.0, NEG_BIG)
    g = g + jnp.asarray(mask, F32)
    return g.reshape(nh, 2 * NA_ROWS, GRID_W, 2 * GRID_W)


def _latent_attention(q, k, v, ctx_k, ctx_v, layer, bias, seq_len):
    m, w = q.shape
    rows = seq_len // GRID_W
    nblk = rows // LAT_QROWS
    nb = m // seq_len
    tq = LAT_TQ
    nctx = ctx_k.shape[2]

    def kspec(i):
        return pl.BlockSpec((tq, w), lambda r, b: (b * nblk + jnp.clip(r - 1, 0, nblk - 3) + i, 0))

    qspec = pl.BlockSpec((tq, w), lambda r, b: (b * nblk + r, 0))
    cspec = pl.BlockSpec((None, None, nctx, w), lambda r, b: (b, layer, 0, 0))
    return pl.pallas_call(
        functools.partial(_lat_attn_kernel, rows=rows),
        grid=(nblk, nb),
        in_specs=[qspec, kspec(0), kspec(1), kspec(2), kspec(0), kspec(1), kspec(2), cspec, cspec,
                  _resident(bias.shape)],
        out_specs=qspec,
        out_shape=jax.ShapeDtypeStruct((m, w), BF16),
        compiler_params=_cparams("parallel", "parallel"),
        name="latent_attention",
    )(q, k, k, k, v, v, v, ctx_k, ctx_v, bias)


def _merge_kernel(x_ref, h_ref, ya_ref, yb_ref, yc_ref, yd_ref, mod_ref, nw_ref,
                  wg_ref, bg_ref, wa_ref, wb_ref, wc_ref, wd_ref, wo_ref, wr_ref, br_ref,
                  x1_ref, h2_ref, route_ref, cnt_ref, carry_ref):
    @pl.when(pl.program_id(0) == 0)
    def _():
        carry_ref[...] = jnp.zeros_like(carry_ref)

    d = x_ref.shape[1]
    h = h_ref[...]
    merged = None
    for i, (y_ref, w_ref) in enumerate(((ya_ref, wa_ref), (yb_ref, wb_ref), (yc_ref, wc_ref), (yd_ref, wd_ref))):
        gate = jax.nn.sigmoid(_dot(h, wg_ref[:, i * d:(i + 1) * d]) + bg_ref[:, i * d:(i + 1) * d])
        term = gate * _dot(y_ref[...], w_ref[...])
        merged = term if merged is None else merged + term
    mod = mod_ref[...]
    x1 = x_ref[...] + mod[2:3, :] * _dot(merged.astype(BF16), wo_ref[...])
    x1_ref[...] = x1
    h2 = _rms_mod(x1, nw_ref[...], mod[3:4, :], mod[4:5, :])
    h2_ref[...] = h2
    h2_hi = h2.astype(BF16)
    h2_lo = (h2 - h2_hi.astype(F32)).astype(BF16)
    logits = (_dot(h2_hi, wr_ref[0]) + _dot(h2_lo, wr_ref[0]) + _dot(h2_hi, wr_ref[1])) + br_ref[...]
    route_ref[...] = _route_tile(logits, carry_ref)
    cnt_ref[...] = carry_ref[...]


def _merge(x, h, ya, yb_t, yc, yd, mod, nw, wts, tm, rows_per_mod, seq_len):
    m, d = x.shape
    cw = ya.shape[1]
    nchunk = seq_len // tm
    row = lambda i: (i, 0)
    const2 = lambda i: (0, 0)
    w_specs = [_resident(w.shape) for w in wts]
    return pl.pallas_call(
        _merge_kernel,
        grid=(m // tm,),
        in_specs=[pl.BlockSpec((tm, d), row), pl.BlockSpec((tm, d), row),
                  pl.BlockSpec((tm, cw), row),
                  pl.BlockSpec((tm, cw), lambda i: (i % nchunk, i // nchunk)),
                  pl.BlockSpec((tm, cw), row),
                  pl.BlockSpec((tm, yd.shape[1]), row),
                  pl.BlockSpec((None, 8, d), lambda i: ((i * tm) // rows_per_mod, 0, 0)),
                  pl.BlockSpec((1, d), const2)] + w_specs,
        out_specs=[pl.BlockSpec((tm, d), row), pl.BlockSpec((tm, d), row),
                   pl.BlockSpec((tm, ROUTE_LANES), row), pl.BlockSpec((8, ROUTE_LANES), const2)],
        out_shape=[jax.ShapeDtypeStruct((m, d), F32), jax.ShapeDtypeStruct((m, d), F32),
                   jax.ShapeDtypeStruct((m, ROUTE_LANES), F32), jax.ShapeDtypeStruct((8, ROUTE_LANES), F32)],
        scratch_shapes=[pltpu.VMEM((8, ROUTE_LANES), F32)],
        compiler_params=_cparams("arbitrary"),
        name="merge",
    )(x, h, ya, yb_t, yc, yd, mod, nw, *wts)


ROUTE_TM = 512
EXPERT_TR = 256
MOVE_TM = 1024
INPROJ_TM = 1024
R_E1, R_E2, R_RANK1, R_RANK2, R_W1, R_W2 = range(6)


def _route_tile(lg, carry_ref):
    lane = lax.broadcasted_iota(jnp.int32, lg.shape, 1).astype(F32)
    neg = jnp.float32(-jnp.inf)
    big = jnp.float32(ROUTE_LANES)
    gl = jnp.where(lane < MOE_GROUPS, lg, neg)
    gm = jnp.max(gl, axis=-1, keepdims=True)
    g_prob = 1.0 / jnp.sum(jnp.exp(gl - gm), axis=-1, keepdims=True)
    gidx = jnp.min(jnp.where(gl == gm, lane, big), axis=-1, keepdims=True)
    e0 = EXPERT_LANE0 + gidx * MOE_EXPERTS
    el = jnp.where((lane >= e0) & (lane < e0 + MOE_EXPERTS), lg, neg)
    m1 = jnp.max(el, axis=-1, keepdims=True)
    i1 = jnp.min(jnp.where(el == m1, lane, big), axis=-1, keepdims=True)
    el2 = jnp.where(lane == i1, neg, el)
    m2 = jnp.max(el2, axis=-1, keepdims=True)
    i2 = jnp.min(jnp.where(el2 == m2, lane, big), axis=-1, keepdims=True)
    r = jnp.exp(m2 - m1)
    w1 = 1.0 / (1.0 + r)
    w2 = r * w1
    two_hot = jnp.where((lane == i1) | (lane == i2), 1.0, 0.0)
    tm = lg.shape[0]
    ltri = (lax.broadcasted_iota(jnp.int32, (tm, tm), 1) < lax.broadcasted_iota(jnp.int32, (tm, tm), 0))
    rank = carry_ref[0:1, :] + _dot(ltri.astype(BF16), two_hot.astype(BF16))
    pick = lambda idx: jnp.sum(jnp.where(lane == idx, rank, 0.0), axis=-1, keepdims=True)
    rec = jnp.zeros_like(lg)
    for col, val in ((R_E1, i1), (R_E2, i2), (R_RANK1, pick(i1)), (R_RANK2, pick(i2)),
                     (R_W1, w1 * g_prob), (R_W2, w2 * g_prob)):
        rec = jnp.where(lane == col, val, rec)
    carry_ref[...] = carry_ref[...] + jnp.sum(two_hot, axis=0, keepdims=True)
    return rec


def _plan_kernel(cnt_ref, seg_ref, tile_ref):
    cnt = cnt_ref[...]
    lane = lax.broadcasted_iota(jnp.int32, cnt.shape, 1)
    is_e = (lane >= EXPERT_LANE0) & (lane < EXPERT_LANE0 + N_EXPERTS)
    size = jnp.where(is_e, jnp.floor((cnt + (EXPERT_TR - 1)) * (1.0 / EXPERT_TR)) * EXPERT_TR, 0.0)
    upper = (lax.broadcasted_iota(jnp.int32, (ROUTE_LANES, ROUTE_LANES), 0)
             < lax.broadcasted_iota(jnp.int32, (ROUTE_LANES, ROUTE_LANES), 1)).astype(F32)
    start = jnp.dot(size, upper, precision=HIGHEST, preferred_element_type=F32)
    end = start + size
    total = jnp.max(end, axis=-1, keepdims=True)
    row = lax.broadcasted_iota(jnp.int32, cnt.shape, 0)
    seg = jnp.where(row == SEG_START, start, jnp.where(row == SEG_SIZE, size, jnp.where(
        row == SEG_END, end, jnp.where(row == SEG_TILES, total * (1.0 / EXPERT_TR), cnt))))
    seg_ref[...] = seg.astype(jnp.int32)
    nt = tile_ref.shape[0]
    t0 = (lax.broadcasted_iota(jnp.int32, (nt, ROUTE_LANES), 0) * EXPERT_TR).astype(F32)
    lane_t = lax.broadcasted_iota(jnp.int32, (nt, ROUTE_LANES), 1)
    done = jnp.where((lane_t >= EXPERT_LANE0) & (lane_t < EXPERT_LANE0 + N_EXPERTS) & (end[0:1, :] <= t0), 1.0, 0.0)
    te = jnp.minimum(jnp.sum(done, axis=-1, keepdims=True), N_EXPERTS - 1.0)
    tile_ref[...] = jnp.broadcast_to(te, (nt, ROUTE_LANES)).astype(jnp.int32)


def _plan(counts, n_tiles):
    nt = -(-n_tiles // 8) * 8
    return pl.pallas_call(
        _plan_kernel,
        out_shape=[jax.ShapeDtypeStruct((8, ROUTE_LANES), jnp.int32),
                   jax.ShapeDtypeStruct((nt, ROUTE_LANES), jnp.int32)],
        name="plan",
    )(counts)


def _positions_kernel(route_ref, seg_ref, pos_ref):
    rec = route_ref[...]
    start = seg_ref[0:1, :].astype(F32)
    lane = lax.broadcasted_iota(jnp.int32, rec.shape, 1).astype(F32)
    seg_start = lambda col: jnp.sum(jnp.where(lane == rec[:, col:col + 1], start, 0.0), axis=-1, keepdims=True)
    p1 = seg_start(R_E1) + rec[:, R_RANK1:R_RANK1 + 1]
    p2 = seg_start(R_E2) + rec[:, R_RANK2:R_RANK2 + 1]
    both = jnp.where(lane == 0.0, p1, jnp.where(lane == 1.0, p2, 0.0))
    pos_ref[...] = jnp.transpose(both)[0:8, :].astype(jnp.int32)


def _positions(route, seg):
    m = route.shape[0]
    return pl.pallas_call(
        _positions_kernel, grid=(m // ROUTE_TM,),
        in_specs=[pl.BlockSpec((ROUTE_TM, ROUTE_LANES), lambda i: (i, 0)),
                  pl.BlockSpec((8, ROUTE_LANES), lambda i: (0, 0))],
        out_specs=pl.BlockSpec((8, ROUTE_TM), lambda i: (0, i)),
        out_shape=jax.ShapeDtypeStruct((8, m), jnp.int32),
        compiler_params=_cparams("parallel"), name="positions",
    )(route, seg)


SEG_START, SEG_SIZE, SEG_END, SEG_TILES, SEG_COUNT = range(5)


def _row_copies(src_of, dst_of, sem):
    copies = [pltpu.make_async_copy(src_of(j, k), dst_of(j, k), sem) for j in range(MOVE_TM) for k in (0, 1)]
    for n, c in enumerate(copies):
        c.start(priority=n % 2)
    for c in copies:
        c.wait()


def _dispatch_kernel(seg_s, pos_s, h2_ref, xs_ref, zbuf, zsem, sem, *, n_tiles):
    @pl.when(pl.program_id(0) == 0)
    def _():
        zbuf[...] = jnp.zeros_like(zbuf)

        def zero_tiles(go):
            for e in range(N_EXPERTS):
                lane = EXPERT_LANE0 + e

                @pl.when(seg_s[SEG_SIZE, lane] > 0)
                def _():
                    start = pl.multiple_of(seg_s[SEG_END, lane] - EXPERT_TR, EXPERT_TR)
                    go(pltpu.make_async_copy(zbuf, xs_ref.at[pl.ds(start, EXPERT_TR)], zsem.at[e]))

                tile = n_tiles - N_EXPERTS + e

                @pl.when(tile >= seg_s[SEG_TILES, 0])
                def _():
                    dst = xs_ref.at[pl.ds(tile * EXPERT_TR, EXPERT_TR)]
                    go(pltpu.make_async_copy(zbuf, dst, zsem.at[N_EXPERTS + e]))

        zero_tiles(lambda c: c.start())
        zero_tiles(lambda c: c.wait())

    _row_copies(lambda j, k: h2_ref.at[pl.ds(j, 1)], lambda j, k: xs_ref.at[pl.ds(pos_s[k, j], 1)], sem)


def _dispatch(h2, pos, seg, n_tiles):
    m, d = h2.shape
    return pl.pallas_call(
        functools.partial(_dispatch_kernel, n_tiles=n_tiles),
        grid_spec=pltpu.PrefetchScalarGridSpec(
            num_scalar_prefetch=1, grid=(m // MOVE_TM,),
            in_specs=[pl.BlockSpec((8, MOVE_TM), lambda i, seg: (0, i), memory_space=pltpu.SMEM),
                      pl.BlockSpec((MOVE_TM, d), lambda i, seg: (i, 0))],
            out_specs=pl.BlockSpec(memory_space=pl.ANY),
            scratch_shapes=[pltpu.VMEM((EXPERT_TR, d), F32), pltpu.SemaphoreType.DMA((2 * N_EXPERTS,)),
                            pltpu.SemaphoreType.DMA]),
        out_shape=jax.ShapeDtypeStruct((n_tiles * EXPERT_TR, d), F32),
        compiler_params=_cparams("arbitrary"), name="dispatch",
    )(seg, pos, h2)


def _experts_kernel(te_s, seg_s, xs_ref, wg_ref, wu_ref, wd_ref, ys_ref):
    used = pl.program_id(0) < seg_s[SEG_TILES, 0]

    @pl.when(used)
    def _():
        x = xs_ref[...].astype(BF16)
        a = _dot(x, wg_ref[...].astype(BF16))
        b = _dot(x, wu_ref[...].astype(BF16))
        ys_ref[...] = _dot((_silu(a) * b).astype(BF16), wd_ref[...].astype(BF16))

    @pl.when(jnp.logical_not(used))
    def _():
        ys_ref[...] = jnp.zeros_like(ys_ref)


def _experts(xs, tile_expert, seg, w_gate, w_up, w_down, layer):
    p, d = xs.shape
    f = w_gate.shape[-1]
    last = lambda j, seg: jnp.minimum(j, seg[SEG_TILES, 0] - 1)
    wmap = lambda j, te, seg: (layer * N_EXPERTS + te[last(j, seg)], 0, 0)
    wspec = lambda shape: pl.BlockSpec(shape, wmap)
    return pl.pallas_call(
        _experts_kernel,
        grid_spec=pltpu.PrefetchScalarGridSpec(
            num_scalar_prefetch=2, grid=(p // EXPERT_TR,),
            in_specs=[pl.BlockSpec((EXPERT_TR, d), lambda j, te, seg: (last(j, seg), 0)),
                      wspec((None, d, f)), wspec((None, d, f)), wspec((None, f, d))],
            out_specs=pl.BlockSpec((EXPERT_TR, d), lambda j, te, seg: (j, 0))),
        out_shape=jax.ShapeDtypeStruct((p, d), F32),
        compiler_params=_cparams("arbitrary"), name="experts",
    )(tile_expert, seg, xs, w_gate, w_up, w_down)


def _combine_kernel(pos_s, route_ref, x1_ref, mod_ref, nw_ref, ys_ref, o_ref, y1buf, y2buf, sem, *, final_norm):
    bufs = (y1buf, y2buf)
    _row_copies(lambda j, k: ys_ref.at[pl.ds(pos_s[k, j], 1)], lambda j, k: bufs[k].at[pl.ds(j, 1)], sem)
    rec = route_ref[...]
    moe = rec[:, R_W1:R_W1 + 1] * y1buf[...] + rec[:, R_W2:R_W2 + 1] * y2buf[...]
    x = x1_ref[...] + mod_ref[5:6, :] * moe
    if final_norm:
        x = x * lax.rsqrt(jnp.mean(x * x, axis=-1, keepdims=True) + EPS) * nw_ref[...]
    o_ref[...] = x


def _combine(ys, pos, route, x1, mod, nw_final, rows_per_mod, final_norm):
    m, d = x1.shape
    row = lambda i: (i, 0)
    return pl.pallas_call(
        functools.partial(_combine_kernel, final_norm=final_norm),
        grid=(m // MOVE_TM,),
        in_specs=[pl.BlockSpec((8, MOVE_TM), lambda i: (0, i), memory_space=pltpu.SMEM),
                  pl.BlockSpec((MOVE_TM, ROUTE_LANES), row), pl.BlockSpec((MOVE_TM, d), row),
                  pl.BlockSpec((None, 8, d), lambda i: ((i * MOVE_TM) // rows_per_mod, 0, 0)),
                  pl.BlockSpec((1, d), lambda i: (0, 0)),
                  pl.BlockSpec(memory_space=pl.ANY)],
        out_specs=pl.BlockSpec((MOVE_TM, d), row),
        out_shape=jax.ShapeDtypeStruct((m, d), F32),
        scratch_shapes=[pltpu.VMEM((MOVE_TM, d), F32), pltpu.VMEM((MOVE_TM, d), F32), pltpu.SemaphoreType.DMA],
        compiler_params=_cparams("arbitrary"), name="combine",
    )(pos, route, x1, mod, nw_final, ys)


def _layer_weights(P, l):
    d = P['w_in'].shape[1]
    cw = P['pool_scale'].shape[1]
    hw = P['hyena_conv'].shape[2]
    o1, o2, o3 = cw, cw + hw, cw + hw + 3 * cw
    na = (P['w_in'].shape[2] - o3) // 3
    cols = ((o1, hw), (o2, 3 * cw), (o3, na), (o3 + na, na), (o3 + 2 * na, na), (0, cw))
    gw = cw // len(POOL_WINDOWS)
    pool_bd = jnp.zeros((cw, cw), F32)
    for g in range(len(POOL_WINDOWS)):
        pool_bd = pool_bd.at[g * gw:(g + 1) * gw, g * gw:(g + 1) * gw].set(P['pool_w'][l, g])
    wr = jnp.concatenate([P['w_route_group'][l],
                          jnp.transpose(P['w_route_exp'][l], (1, 0, 2)).reshape(d, N_EXPERTS)], axis=1)
    wr = jnp.zeros((d, ROUTE_LANES), F32).at[:, :wr.shape[1]].set(wr)
    wr_hi = wr.astype(BF16)
    wr_lo = (wr - wr_hi.astype(F32)).astype(BF16)
    br = jnp.concatenate([P['b_route_group'][l], P['b_route_exp'][l].reshape(-1)])
    br = jnp.zeros((1, ROUTE_LANES), F32).at[0, :br.shape[0]].set(br)
    return dict(
        cols=cols, w_in=P['w_in'][l].astype(BF16), norm_mix=P['norm_mix'][l][None], norm_ffn=P['norm_ffn'][l][None],
        pool_bd=pool_bd.astype(BF16), pool_scale=P['pool_scale'][l][None], sconv_w=P['sconv_w'][l],
        hyena_conv=P['hyena_conv'][l], hyena_bias=P['hyena_bias'][l],
        merge=(P['w_gate'][l].astype(BF16), P['b_gate'][l][None], P['w_br_a'][l].astype(BF16),
               P['w_br_b'][l].astype(BF16), P['w_br_c'][l].astype(BF16), P['w_br_d'][l].astype(BF16),
               P['w_out'][l].astype(BF16), jnp.stack([wr_hi, wr_lo]), br),
    )


def _run_stream(x3, mods, LW, EW, nw_final, hy, attend, depth):
    b, seq_len, d = x3.shape
    m = b * seq_len
    x = x3.reshape(m, d)
    rows_per_mod = m // mods.shape[1]
    tm = min(512, seq_len)
    tc = min(256, seq_len)
    kvs = []
    for l in range(depth):
        W = LW[l]
        mod = mods[l]
        cw = W['pool_scale'].shape[1]
        h, ph, ps, q, k, v, pp = _inproj(x, mod, W['norm_mix'], W['w_in'], W['cols'],
                                         min(INPROJ_TM, rows_per_mod), rows_per_mod)
        kvs.append((k, v))
        ya, yc, u_t, x0_t = _local_mixers(pp, ps, ph, W['pool_bd'], W['pool_scale'], W['sconv_w'],
                                          W['hyena_conv'], seq_len, tc)
        yb_t = _hyena_conv(u_t, x0_t, hy['fwd_bf'], hy['inv_c'], hy['inv_s'], hy['spec'][l],
                           W['hyena_bias'], cw)
        yd = attend(l, q, k, v)
        x1, h2, route, counts = _merge(x, h, ya, yb_t, yc, yd, mod, W['norm_ffn'], W['merge'], tm,
                                       rows_per_mod, seq_len)
        n_tiles = 2 * m // EXPERT_TR + N_EXPERTS
        seg, tile_tab = _plan(counts, n_tiles)
        pos = _positions(route, seg)
        xs = _dispatch(h2, pos, seg, n_tiles)
        ys = _experts(xs, tile_tab[:, 0], seg, EW[0], EW[1], EW[2], l)
        x = _combine(ys, pos, route, x1, mod, nw_final, rows_per_mod, final_norm=(l == depth - 1))
    return x, kvs


def _hyena_setup(seq_len, P, depth):
    fwd32, inv_c, inv_s = _dft_matrices(seq_len)
    spec = []
    for l in range(depth):
        filt = _hyena_filter(seq_len, P['hyena_f1'][l], P['hyena_fb1'][l], P['hyena_f2'][l],
                             P['hyena_fb2'][l], P['hyena_f3'][l], P['hyena_freq'][l], P['hyena_decay'][l])
        spec.append(_filter_spectrum(fwd32, filt))
    return dict(fwd_bf=fwd32.astype(BF16), inv_c=inv_c, inv_s=inv_s, spec=spec)


def kernel(x_prompt, x_sample, cache_k, cache_v, c, c_ctx, w_ada, b_ada, norm_mix, w_in, w_gate, b_gate, pool_w, pool_scale, hyena_conv, hyena_f1, hyena_fb1, hyena_f2, hyena_fb2, hyena_f3, hyena_freq, hyena_decay, hyena_bias, sconv_w, na_rpb, w_br_a, w_br_b, w_br_c, w_br_d, w_out, norm_ffn, w_route_group, b_route_group, w_route_exp, b_route_exp, w_e_gate, w_e_up, w_e_down, norm_final):
    P = dict(w_in=w_in, w_gate=w_gate, b_gate=b_gate, pool_w=pool_w, pool_scale=pool_scale,
             hyena_conv=hyena_conv, hyena_f1=hyena_f1, hyena_fb1=hyena_fb1, hyena_f2=hyena_f2,
             hyena_fb2=hyena_fb2, hyena_f3=hyena_f3, hyena_freq=hyena_freq, hyena_decay=hyena_decay,
             hyena_bias=hyena_bias, sconv_w=sconv_w, w_br_a=w_br_a, w_br_b=w_br_b, w_br_c=w_br_c,
             w_br_d=w_br_d, w_out=w_out, norm_mix=norm_mix, norm_ffn=norm_ffn,
             w_route_group=w_route_group, b_route_group=b_route_group, w_route_exp=w_route_exp,
             b_route_exp=b_route_exp, w_e_gate=w_e_gate, w_e_up=w_e_up, w_e_down=w_e_down)
    depth, d, _ = w_ada.shape
    bp, lp, _ = x_prompt.shape
    bs, ls, _ = x_sample.shape
    assert (ls // GRID_W) % LAT_QROWS == 0 and ls // GRID_W >= LAT_KROWS

    n_c = 1 + bs
    n_pad = -(-n_c // 8) * 8
    cvecs = jnp.zeros((n_pad, d), F32).at[0].set(c_ctx).at[1:n_c].set(c)
    ada = _ada(cvecs, w_ada, b_ada).reshape(depth, n_pad, 6, d)
    ada = jnp.concatenate([ada, jnp.zeros((depth, n_pad, 2, d), F32)], axis=2)

    LW = [_layer_weights(P, l) for l in range(depth)]
    nw_final = norm_final[None]
    f = w_e_gate.shape[-1]
    EW = (w_e_gate.reshape(depth * N_EXPERTS, d, f), w_e_up.reshape(depth * N_EXPERTS, d, f),
          w_e_down.reshape(depth * N_EXPERTS, f, d))

    hy_p = _hyena_setup(lp, P, depth)
    xp, kv_p = _run_stream(x_prompt, ada[:, 0:1], LW, EW, nw_final, hy_p,
                           lambda l, q, k, v: _context_attention(q, k, v, lp), depth)
    y_prompt = xp.reshape(bp, lp, d)
    new_k = jnp.stack([k.reshape(bp, lp, N_HEADS, HEAD_DIM) for k, _ in kv_p], axis=1)
    new_v = jnp.stack([v.reshape(bp, lp, N_HEADS, HEAD_DIM) for _, v in kv_p], axis=1)

    hy_s = _hyena_setup(ls, P, depth)
    past = cache_k.shape[2]
    ck = cache_k.reshape(bs, depth, past, N_HEADS * HEAD_DIM)
    cv = cache_v.reshape(bs, depth, past, N_HEADS * HEAD_DIM)
    biases = [_latent_bias(na_rpb[l], ls // GRID_W) for l in range(depth)]
    xs, _ = _run_stream(x_sample, ada[:, 1:n_c], LW, EW, nw_final, hy_s,
                        lambda l, q, k, v: _latent_attention(q, k, v, ck, cv, l, biases[l], ls),
                        depth)
    y_sample = xs.reshape(bs, ls, d)
    return (y_prompt, y_sample, new_k, new_v)
```

```python
import functools
import math

import numpy as np
import jax
import jax.numpy as jnp
from jax import lax
from jax.experimental import pallas as pl
from jax.experimental.pallas import tpu as pltpu

F32 = jnp.float32
BF16 = jnp.bfloat16
HIGHEST = lax.Precision.HIGHEST

EPS = 1e-6
GRID_W = 64
NA_ROWS = 8
NA_COLS = 16
N_HEADS = 8
HEAD_DIM = 64
POOL_WINDOWS = (2, 4, 8, 16)
HYENA_BANDS = 16
MOE_GROUPS = 4
MOE_EXPERTS = 8
N_EXPERTS = MOE_GROUPS * MOE_EXPERTS
ROUTE_LANES = 128
EXPERT_LANE0 = MOE_GROUPS
HALO = 8
NEG_BIG = -1e30
VMEM_LIMIT_BYTES = 48 * 1024 * 1024


def _cparams(*sem):
    return pltpu.CompilerParams(dimension_semantics=sem, vmem_limit_bytes=VMEM_LIMIT_BYTES)


def _resident(shape):
    nd = len(shape)
    return pl.BlockSpec(shape, lambda *_: (0,) * nd, pipeline_mode=pl.Buffered(1))


def _dot(a, b):
    return jnp.dot(a, b, preferred_element_type=F32)


def _silu(x):
    return x * jax.nn.sigmoid(x)


def _ada_kernel(cv_ref, w_ref, b_ref, o_ref):
    o_ref[...] = jnp.dot(_silu(cv_ref[...]), w_ref[...], precision=HIGHEST,
                         preferred_element_type=F32) + b_ref[...]


def _ada(cvecs, w_ada, b_ada):
    depth, d, d6 = w_ada.shape
    r = cvecs.shape[0]
    return pl.pallas_call(
        _ada_kernel,
        grid=(depth, d6 // d),
        in_specs=[pl.BlockSpec((r, d), lambda l, j: (0, 0)),
                  pl.BlockSpec((None, d, d), lambda l, j: (l, 0, j)),
                  pl.BlockSpec((None, 1, d), lambda l, j: (l, 0, j))],
        out_specs=pl.BlockSpec((None, r, d), lambda l, j: (l, 0, j)),
        out_shape=jax.ShapeDtypeStruct((depth, r, d6), F32),
        compiler_params=_cparams("parallel", "parallel"),
        name="ada",
    )(cvecs, w_ada, b_ada.reshape(depth, 1, d6))


def _rms_mod(x, nw, shift, scale):
    y = x * lax.rsqrt(jnp.mean(x * x, axis=-1, keepdims=True) + EPS) * nw
    return y * (1.0 + scale) + shift


def _inproj_kernel(x_ref, mod_ref, nw_ref, w_ref, h_ref, ph_ref, ps_ref, q_ref, k_ref, v_ref, pp_ref,
                   *, cols):
    mod = mod_ref[...]
    h = _rms_mod(x_ref[...], nw_ref[...], mod[0:1, :], mod[1:2, :]).astype(BF16)
    h_ref[...] = h
    for ref, (off, wd) in zip((ph_ref, ps_ref, q_ref, k_ref, v_ref, pp_ref), cols):
        r = _dot(h, w_ref[:, off:off + wd])
        if ref is q_ref:
            r = r * (HEAD_DIM ** -0.5)
        ref[...] = r.astype(ref.dtype)


def _inproj(x, mod, nw, w_in, cols, tm, rows_per_mod):
    m, d = x.shape
    n = w_in.shape[1]
    widths = [wd for _, wd in cols]
    dts = (F32, F32, BF16, F32, F32, F32)
    row = lambda i: (i, 0)
    return pl.pallas_call(
        functools.partial(_inproj_kernel, cols=cols),
        grid=(m // tm,),
        in_specs=[pl.BlockSpec((tm, d), row),
                  pl.BlockSpec((None, 8, d), lambda i: ((i * tm) // rows_per_mod, 0, 0)),
                  pl.BlockSpec((1, d), lambda i: (0, 0)),
                  _resident((d, n))],
        out_specs=[pl.BlockSpec((tm, d), row)] + [pl.BlockSpec((tm, wd), row) for wd in widths],
        out_shape=[jax.ShapeDtypeStruct((m, d), BF16)]
        + [jax.ShapeDtypeStruct((m, wd), dt) for wd, dt in zip(widths, dts)],
        compiler_params=_cparams("parallel"),
        name="inproj",
    )(x, mod, nw, w_in)


def _fill_padded(pad_ref, prev_ref, cur_ref, next_ref, first, last, tc):
    zero = jnp.zeros((HALO, cur_ref.shape[1]), F32)
    pad_ref[0:HALO, :] = jnp.where(first, zero, prev_ref[...])
    pad_ref[HALO:HALO + tc, :] = cur_ref[...]
    pad_ref[HALO + tc:2 * HALO + tc, :] = jnp.where(last, zero, next_ref[...])


def _local_kernel(pp_ref, pp_prev, pp_next, ps_ref, ps_prev, ps_next, ph_ref, ph_prev, ph_next,
                  pw_ref, pscale_ref, sw_ref, hw_ref,
                  ya_ref, yc_ref, u_ref, x0_ref,
                  pad_p, pad_s, pad_h, *, seq_len, tc):
    nchunk = seq_len // tc
    j = pl.program_id(0) % nchunk
    first = j == 0
    last = j == nchunk - 1
    cw = pp_ref.shape[1]
    sw = cw

    _fill_padded(pad_p, pp_prev, pp_ref, pp_next, first, last, tc)
    sh = lambda k: pad_p[HALO + k:HALO + k + tc, :]
    u = pp_ref[...]
    sums = {}
    acc = u
    lo_done, hi_done = 0, 0
    for win in POOL_WINDOWS:
        lo, hi = -(win // 2), win // 2 - 1
        for k in range(lo, lo_done):
            acc = acc + sh(k)
        for k in range(hi_done + 1, hi + 1):
            acc = acc + sh(k)
        lo_done, hi_done = lo, hi
        sums[win] = acc
    t = j * tc + lax.broadcasted_iota(jnp.int32, (tc, 1), 0)
    lane = lax.broadcasted_iota(jnp.int32, (1, cw), 1)
    gw = cw // len(POOL_WINDOWS)
    pooled = None
    for g, win in reversed(list(enumerate(POOL_WINDOWS))):
        cnt = jnp.minimum(t - win // 2 + win, seq_len) - jnp.maximum(t - win // 2, 0)
        val = sums[win] * (1.0 / cnt.astype(F32))
        pooled = val if pooled is None else jnp.where(lane < (g + 1) * gw, val, pooled)
    pooled = pooled - u
    ya = _dot(pooled.astype(BF16), pw_ref[...]) * pscale_ref[...]
    ya_ref[...] = ya.astype(BF16)

    _fill_padded(pad_s, ps_prev, ps_ref, ps_next, first, last, tc)
    w3 = sw_ref[...]
    z = lambda k: (pad_s[HALO + k:HALO + k + tc, 2 * sw:3 * sw] * pad_s[HALO + k:HALO + k + tc, 0:sw])
    conv = w3[0:1, :] * z(-1) + w3[1:2, :] * z(0) + w3[2:3, :] * z(1)
    yc_ref[...] = (ps_ref[:, sw:2 * sw] * conv).astype(BF16)

    _fill_padded(pad_h, ph_prev, ph_ref, ph_next, first, last, tc)
    hw = hw_ref[...]
    c3 = (hw[0:1, :] * pad_h[HALO - 1:HALO - 1 + tc, :] + hw[1:2, :] * pad_h[HALO:HALO + tc, :]
          + hw[2:3, :] * pad_h[HALO + 1:HALO + 1 + tc, :])
    x0_ref[...] = c3[:, 0:sw]
    u_ref[...] = c3[:, sw:2 * sw] * c3[:, 2 * sw:3 * sw]


def _local_mixers(pp, ps, ph, pool_bd, pool_scale, sconv_w, hyena_conv, seq_len, tc):
    m, cw = pp.shape
    nb = m // seq_len
    nchunk = seq_len // tc
    hpc = tc // HALO
    nhb = m // HALO
    row = lambda i: (i, 0)
    prev = lambda i: (jnp.maximum(i * hpc - 1, 0), 0)
    nxt = lambda i: (jnp.minimum((i + 1) * hpc, nhb - 1), 0)
    tl = lambda i: (i % nchunk, i // nchunk)
    const = lambda i: (0, 0)

    def trio(width):
        return [pl.BlockSpec((tc, width), row), pl.BlockSpec((HALO, width), prev),
                pl.BlockSpec((HALO, width), nxt)]

    return pl.pallas_call(
        functools.partial(_local_kernel, seq_len=seq_len, tc=tc),
        grid=(m // tc,),
        in_specs=trio(cw) + trio(3 * cw) + trio(3 * cw)
        + [pl.BlockSpec((cw, cw), const), pl.BlockSpec((1, cw), const),
           pl.BlockSpec((3, cw), const), pl.BlockSpec((3, 3 * cw), const)],
        out_specs=[pl.BlockSpec((tc, cw), row), pl.BlockSpec((tc, cw), row),
                   pl.BlockSpec((tc, cw), tl), pl.BlockSpec((tc, cw), tl)],
        out_shape=[jax.ShapeDtypeStruct((m, cw), BF16), jax.ShapeDtypeStruct((m, cw), BF16),
                   jax.ShapeDtypeStruct((seq_len, nb * cw), F32),
                   jax.ShapeDtypeStruct((seq_len, nb * cw), F32)],
        scratch_shapes=[pltpu.VMEM((tc + 2 * HALO, cw), F32), pltpu.VMEM((tc + 2 * HALO, 3 * cw), F32),
                        pltpu.VMEM((tc + 2 * HALO, 3 * cw), F32)],
        compiler_params=_cparams("parallel"),
        name="local_mixers",
    )(pp, pp, pp, ps, ps, ps, ph, ph, ph, pool_bd, pool_scale, sconv_w, hyena_conv)


def _trig(rows, cols, n):
    split = 64
    r = np.asarray(rows, np.int64)[:, None]
    c = np.asarray(cols, np.int64)
    assert c[0] % split == 0 and len(c) % split == 0 and np.all(np.diff(c) == 1)
    c0 = np.arange(split)[None, :]
    c1 = c[::split][None, :]
    ang0 = ((r * c0) % n) * (2.0 * math.pi / n)
    ang1 = ((r * c1) % n) * (2.0 * math.pi / n)
    tab = lambda a: jnp.asarray(a, F32)
    ca, sa = tab(np.cos(ang0))[:, None, :], tab(np.sin(ang0))[:, None, :]
    cb, sb = tab(np.cos(ang1))[:, :, None], tab(np.sin(ang1))[:, :, None]
    shape = (len(rows), len(c))
    return (ca * cb - sa * sb).reshape(shape), (sa * cb + ca * sb).reshape(shape)


def _dft_matrices(seq_len):
    L = seq_len
    n = 2 * L
    k = np.arange(L)
    first = jnp.asarray(k == 0)
    alt = jnp.asarray(np.where(k % 2 == 0, 1.0, -1.0), F32)
    cos2, sin2 = _trig(np.concatenate([k, k]), k, n)
    re_rows = jnp.asarray(np.arange(n) < L)[:, None]
    nyq_row = jnp.asarray(np.arange(n) == L)[:, None]
    fwd = jnp.where(re_rows, cos2, jnp.where(nyq_row, alt[None, :], -sin2))
    tp = np.arange(L // 2, L // 2 + L)
    cos_i, sin_i = _trig(tp, k, n)
    alt_t = jnp.asarray(np.where(tp % 2 == 0, 1.0, -1.0) / n, F32)
    inv_c = jnp.where(first[None, :], 1.0 / n, cos_i * (2.0 / n))
    inv_s = jnp.where(first[None, :], alt_t[:, None], sin_i * (-2.0 / n))
    return fwd, inv_c.astype(BF16), inv_s.astype(BF16)


def _hyena_embedding(seq_len, width):
    t = np.arange(seq_len, dtype=np.float64)
    w = (2.0 * math.pi / seq_len) * t
    bands = np.linspace(1e-4, HYENA_BANDS - 1, HYENA_BANDS)
    z = np.concatenate([(t / (seq_len - 1))[:, None], np.cos(w[:, None] * bands),
                        -np.sin(w[:, None] * bands)], axis=-1)
    out = np.zeros((seq_len, width), np.float32)
    out[:, :z.shape[1]] = z
    return jnp.asarray(out)


def _filter_kernel(z_ref, f1_ref, fb1_ref, f2_ref, fb2_ref, f3_ref, freq_ref, decay_ref, o_ref, *, seq_len):
    hdot = lambda a, b: jnp.dot(a, b, precision=HIGHEST, preferred_element_type=F32)
    fr = freq_ref[...]
    hdn = jnp.sin(fr * (hdot(z_ref[...], f1_ref[...]) + fb1_ref[...]))
    hdn = jnp.sin(fr * (hdot(hdn, f2_ref[...]) + fb2_ref[...]))
    filt = hdot(hdn, f3_ref[...])
    t = lax.broadcasted_iota(jnp.int32, (seq_len, 1), 0)
    dist = jnp.abs(t - seq_len // 2).astype(F32) / (seq_len / 2)
    filt = filt * jnp.exp(-dist * decay_ref[...])
    o_ref[...] = filt / jnp.sum(jnp.abs(filt), axis=0, keepdims=True)


def _pad2(a, rows, cols):
    return jnp.zeros((rows, cols), F32).at[:a.shape[0], :a.shape[1]].set(a)


def _hyena_filter(seq_len, f1, fb1, f2, fb2, f3, freq, decay):
    p = 128
    c = f3.shape[1]
    args = (_hyena_embedding(seq_len, p), _pad2(f1, p, p), _pad2(fb1[None], 1, p), _pad2(f2, p, p),
            _pad2(fb2[None], 1, p), _pad2(f3, p, c), _pad2(freq[None], 1, p), decay[None])
    return pl.pallas_call(
        functools.partial(_filter_kernel, seq_len=seq_len),
        out_shape=jax.ShapeDtypeStruct((seq_len, c), F32),
        compiler_params=pltpu.CompilerParams(vmem_limit_bytes=VMEM_LIMIT_BYTES),
        name="hyena_filter",
    )(*args)


def _spectrum_kernel(f_ref, h_ref, o_ref):
    o_ref[...] = jnp.dot(f_ref[...], h_ref[...], precision=HIGHEST, preferred_element_type=F32)


def _filter_spectrum(fwd32, filt):
    n, L = fwd32.shape
    c = filt.shape[1]
    tf = min(n, 512)
    return pl.pallas_call(
        _spectrum_kernel,
        grid=(n // tf,),
        in_specs=[pl.BlockSpec((tf, L), lambda i: (i, 0)), pl.BlockSpec((L, c), lambda i: (0, 0))],
        out_specs=pl.BlockSpec((tf, c), lambda i: (i, 0)),
        out_shape=jax.ShapeDtypeStruct((n, c), F32),
        compiler_params=_cparams("parallel"),
        name="filter_spectrum",
    )(fwd32, filt)


def _hy_fwd_kernel(fc_ref, fs_ref, u_ref, hr_ref, hi_ref, yr_ref, yi_ref, *, cw):
    ub = u_ref[...].astype(BF16)
    ur = _dot(fc_ref[...], ub)
    ui = _dot(fs_ref[...], ub)
    tf = ur.shape[0]
    row0 = (pl.program_id(0) * tf + lax.broadcasted_iota(jnp.int32, (tf, 1), 0)) == 0
    hr, hi = hr_ref[...], hi_ref[...]
    for s in range(ur.shape[1] // cw):
        a, b = ur[:, s * cw:(s + 1) * cw], ui[:, s * cw:(s + 1) * cw]
        yr = jnp.where(row0, a * hr, a * hr - b * hi)
        yi = jnp.where(row0, b * hi, a * hi + b * hr)
        yr_ref[:, s * cw:(s + 1) * cw] = yr.astype(BF16)
        yi_ref[:, s * cw:(s + 1) * cw] = yi.astype(BF16)


def _hy_inv_kernel(ic_ref, is_ref, yr_ref, yi_ref, u_ref, x0_ref, bias_ref, o_ref):
    conv = _dot(ic_ref[...], yr_ref[...]) + _dot(is_ref[...], yi_ref[...])
    o_ref[...] = ((conv + bias_ref[...] * u_ref[...]) * x0_ref[...]).astype(BF16)


def _hyena_conv(u_t, x0_t, fwd_bf, inv_c, inv_s, spec, bias, cw):
    L, ncol = u_t.shape
    tf = min(L, 1024)
    tn = min(ncol, 512)
    nf = L // tf
    yr, yi = pl.pallas_call(
        functools.partial(_hy_fwd_kernel, cw=cw),
        grid=(nf, ncol // tn),
        in_specs=[pl.BlockSpec((tf, L), lambda i, j: (i, 0)),
                  pl.BlockSpec((tf, L), lambda i, j: (nf + i, 0)),
                  pl.BlockSpec((L, tn), lambda i, j: (0, j)),
                  pl.BlockSpec((tf, cw), lambda i, j: (i, 0)),
                  pl.BlockSpec((tf, cw), lambda i, j: (nf + i, 0))],
        out_specs=[pl.BlockSpec((tf, tn), lambda i, j: (i, j))] * 2,
        out_shape=[jax.ShapeDtypeStruct((L, ncol), BF16)] * 2,
        compiler_params=_cparams("parallel", "parallel"),
        name="hyena_dft",
    )(fwd_bf, fwd_bf, u_t, spec, spec)
    bias_t = jnp.tile(bias[None, :], (1, tn // cw))
    return pl.pallas_call(
        _hy_inv_kernel,
        grid=(nf, ncol // tn),
        in_specs=[pl.BlockSpec((tf, L), lambda i, j: (i, 0)),
                  pl.BlockSpec((tf, L), lambda i, j: (i, 0)),
                  pl.BlockSpec((L, tn), lambda i, j: (0, j)),
                  pl.BlockSpec((L, tn), lambda i, j: (0, j)),
                  pl.BlockSpec((tf, tn), lambda i, j: (i, j)),
                  pl.BlockSpec((tf, tn), lambda i, j: (i, j)),
                  pl.BlockSpec((1, tn), lambda i, j: (0, 0))],
        out_specs=pl.BlockSpec((tf, tn), lambda i, j: (i, j)),
        out_shape=jax.ShapeDtypeStruct((L, ncol), BF16),
        compiler_params=_cparams("parallel", "parallel"),
        name="hyena_idft",
    )(inv_c, inv_s, yr, yi, u_t, x0_t, bias_t)


def _qkt(q, k):
    return lax.dot_general(q, k, (((1,), (1,)), ((), ())), preferred_element_type=F32)


PAIR = 2 * HEAD_DIM


def _attend_pair(q_pair, parts, bias_of=None):
    low = lax.broadcasted_iota(jnp.int32, (1, PAIR), 1) < HEAD_DIM
    one = jnp.ones((), BF16)
    accs = []
    for half in (0, 1):
        mine = low if half == 0 else jnp.logical_not(low)
        qh = jnp.where(mine, q_pair, jnp.zeros((), BF16))
        scores = []
        for i, (k, _) in enumerate(parts):
            s = _qkt(qh, k)
            b = None if bias_of is None else bias_of(half, i)
            scores.append(s if b is None else s + b)
        m = None
        for s in scores:
            mi = jnp.max(s, axis=-1, keepdims=True)
            m = mi if m is None else jnp.maximum(m, mi)
        acc = None
        for s, (_, v) in zip(scores, parts):
            p = jnp.exp((s - m).astype(BF16))
            o = _dot(p, jnp.where(mine, v, one))
            acc = o if acc is None else acc + o
        accs.append(acc)
    num = jnp.where(low, accs[0], accs[1])
    den = pltpu.roll(jnp.where(low, accs[1], accs[0]), HEAD_DIM, axis=1)
    return num * (1.0 / den)


def _ctx_attn_kernel(q_ref, k_ref, v_ref, o_ref):
    for j in range(N_HEADS // 2):
        sl = slice(j * PAIR, (j + 1) * PAIR)
        parts = [(k_ref[:, sl].astype(BF16), v_ref[:, sl].astype(BF16))]
        o_ref[:, sl] = _attend_pair(q_ref[:, sl], parts).astype(BF16)


def _context_attention(q, k, v, seq_len):
    m, w = q.shape
    spec = pl.BlockSpec((seq_len, w), lambda b: (b, 0))
    return pl.pallas_call(
        _ctx_attn_kernel,
        grid=(m // seq_len,),
        in_specs=[spec, spec, spec],
        out_specs=spec,
        out_shape=jax.ShapeDtypeStruct((m, w), BF16),
        compiler_params=_cparams("parallel"),
        name="context_attention",
    )(q, k, v)


LAT_QROWS = 4
LAT_KROWS = NA_ROWS + LAT_QROWS
LAT_TQ = LAT_QROWS * GRID_W


def _lat_attn_kernel(q_ref, k0_ref, k1_ref, k2_ref, v0_ref, v1_ref, v2_ref, ck_ref, cv_ref, g_ref, o_ref,
                     *, rows):
    nblk = rows // LAT_QROWS
    rblk = pl.program_id(0)
    kblk = jnp.clip(rblk - 1, 0, nblk - 3)
    slot = {}
    for ri in range(LAT_QROWS):
        r = rblk * LAT_QROWS + ri
        rs = jnp.clip(r - NA_ROWS // 2, 0, rows - NA_ROWS)
        for kj in range(LAT_KROWS):
            krow = kblk * LAT_QROWS + kj
            in_window = (krow >= rs) & (krow < rs + NA_ROWS)
            slot[ri, kj] = jnp.where(in_window, krow - r + NA_ROWS - 1, 2 * NA_ROWS - 1)
    low_half = lax.broadcasted_iota(jnp.int32, (GRID_W, 2 * GRID_W), 1) < GRID_W

    def bias_block(h, i):
        row_blocks = []
        for ri in range(LAT_QROWS):
            tiles = []
            for kj in range(LAT_QROWS * i, LAT_QROWS * (i + 1), 2):
                tiles.append(jnp.where(low_half, g_ref[h, slot[ri, kj]], g_ref[h, slot[ri, kj + 1]]))
            row_blocks.append(jnp.concatenate(tiles, axis=1))
        return jnp.concatenate(row_blocks, axis=0)

    n_local = 3
    for j in range(N_HEADS // 2):
        sl = slice(j * PAIR, (j + 1) * PAIR)
        parts = [(kr[:, sl].astype(BF16), vr[:, sl].astype(BF16))
                 for kr, vr in ((k0_ref, v0_ref), (k1_ref, v1_ref), (k2_ref, v2_ref), (ck_ref, cv_ref))]
        bias_of = lambda half, i, j=j: bias_block(2 * j + half, i) if i < n_local else None
        o_ref[:, sl] = _attend_pair(q_ref[:, sl], parts, bias_of).astype(BF16)


def _latent_bias(rpb, rows):
    del rows
    nh, ndr, ndc = rpb.shape
    qc = np.arange(GRID_W)[:, None]
    kc = (np.arange(2 * GRID_W) % GRID_W)[None, :]
    ws = np.clip(qc - NA_COLS // 2, 0, GRID_W - NA_COLS)
    col_ok = ((kc >= ws) & (kc < ws + NA_COLS)).reshape(-1)
    dc = np.clip(kc - qc + NA_COLS - 1, 0, ndc - 1).reshape(-1)
    onehot = jnp.asarray((dc[None, :] == np.arange(ndc)[:, None]) & col_ok[None, :], F32)
    rpb_ext = jnp.concatenate([rpb, jnp.zeros((nh, 2 * NA_ROWS - ndr, ndc), F32)], axis=1)
    g = jnp.einsum('hrd,dx->hrx', rpb_ext, onehot, precision=HIGHEST)
    mask = np.where(col_ok[None, :] & (np.arange(2 * NA_ROWS) < ndr)[:, None], 0.0, NEG_BIG)
    g = g + jnp.asarray(mask, F32)
    return g.reshape(nh, 2 * NA_ROWS, GRID_W, 2 * GRID_W)


def _latent_attention(q, k, v, ctx_k, ctx_v, layer, bias, seq_len):
    m, w = q.shape
    rows = seq_len // GRID_W
    nblk = rows // LAT_QROWS
    nb = m // seq_len
    tq = LAT_TQ
    nctx = ctx_k.shape[2]

    def kspec(i):
        return pl.BlockSpec((tq, w), lambda r, b: (b * nblk + jnp.clip(r - 1, 0, nblk - 3) + i, 0))

    qspec = pl.BlockSpec((tq, w), lambda r, b: (b * nblk + r, 0))
    cspec = pl.BlockSpec((None, None, nctx, w), lambda r, b: (b, layer, 0, 0))
    return pl.pallas_call(
        functools.partial(_lat_attn_kernel, rows=rows),
        grid=(nblk, nb),
        in_specs=[qspec, kspec(0), kspec(1), kspec(2), kspec(0), kspec(1), kspec(2), cspec, cspec,
                  _resident(bias.shape)],
        out_specs=qspec,
        out_shape=jax.ShapeDtypeStruct((m, w), BF16),
        compiler_params=_cparams("parallel", "parallel"),
        name="latent_attention",
    )(q, k, k, k, v, v, v, ctx_k, ctx_v, bias)


def _merge_kernel(x_ref, h_ref, ya_ref, yb_ref, yc_ref, yd_ref, mod_ref, nw_ref,
                  wg_ref, bg_ref, wa_ref, wb_ref, wc_ref, wd_ref, wo_ref, wr_ref, br_ref,
                  x1_ref, h2_ref, route_ref, cnt_ref, carry_ref):
    @pl.when(pl.program_id(0) == 0)
    def _():
        carry_ref[...] = jnp.zeros_like(carry_ref)

    d = x_ref.shape[1]
    h = h_ref[...]
    merged = None
    for i, (y_ref, w_ref) in enumerate(((ya_ref, wa_ref), (yb_ref, wb_ref), (yc_ref, wc_ref), (yd_ref, wd_ref))):
        gate = jax.nn.sigmoid(_dot(h, wg_ref[:, i * d:(i + 1) * d]) + bg_ref[:, i * d:(i + 1) * d])
        term = gate * _dot(y_ref[...], w_ref[...])
        merged = term if merged is None else merged + term
    mod = mod_ref[...]
    x1 = x_ref[...] + mod[2:3, :] * _dot(merged.astype(BF16), wo_ref[...])
    x1_ref[...] = x1
    h2 = _rms_mod(x1, nw_ref[...], mod[3:4, :], mod[4:5, :])
    h2_ref[...] = h2
    h2_hi = h2.astype(BF16)
    h2_lo = (h2 - h2_hi.astype(F32)).astype(BF16)
    logits = (_dot(h2_hi, wr_ref[0]) + _dot(h2_lo, wr_ref[0]) + _dot(h2_hi, wr_ref[1])) + br_ref[...]
    route_ref[...] = _route_tile(logits, carry_ref)
    cnt_ref[...] = carry_ref[...]


def _merge(x, h, ya, yb_t, yc, yd, mod, nw, wts, tm, rows_per_mod, seq_len):
    m, d = x.shape
    cw = ya.shape[1]
    nchunk = seq_len // tm
    row = lambda i: (i, 0)
    const2 = lambda i: (0, 0)
    w_specs = [_resident(w.shape) for w in wts]
    return pl.pallas_call(
        _merge_kernel,
        grid=(m // tm,),
        in_specs=[pl.BlockSpec((tm, d), row), pl.BlockSpec((tm, d), row),
                  pl.BlockSpec((tm, cw), row),
                  pl.BlockSpec((tm, cw), lambda i: (i % nchunk, i // nchunk)),
                  pl.BlockSpec((tm, cw), row),
                  pl.BlockSpec((tm, yd.shape[1]), row),
                  pl.BlockSpec((None, 8, d), lambda i: ((i * tm) // rows_per_mod, 0, 0)),
                  pl.BlockSpec((1, d), const2)] + w_specs,
        out_specs=[pl.BlockSpec((tm, d), row), pl.BlockSpec((tm, d), row),
                   pl.BlockSpec((tm, ROUTE_LANES), row), pl.BlockSpec((8, ROUTE_LANES), const2)],
        out_shape=[jax.ShapeDtypeStruct((m, d), F32), jax.ShapeDtypeStruct((m, d), F32),
                   jax.ShapeDtypeStruct((m, ROUTE_LANES), F32), jax.ShapeDtypeStruct((8, ROUTE_LANES), F32)],
        scratch_shapes=[pltpu.VMEM((8, ROUTE_LANES), F32)],
        compiler_params=_cparams("arbitrary"),
        name="merge",
    )(x, h, ya, yb_t, yc, yd, mod, nw, *wts)


ROUTE_TM = 512
EXPERT_TR = 512
MOVE_TM = 1024
INPROJ_TM = 1024
R_E1, R_E2, R_RANK1, R_RANK2, R_W1, R_W2 = range(6)


def _route_tile(lg, carry_ref):
    lane = lax.broadcasted_iota(jnp.int32, lg.shape, 1).astype(F32)
    neg = jnp.float32(-jnp.inf)
    big = jnp.float32(ROUTE_LANES)
    gl = jnp.where(lane < MOE_GROUPS, lg, neg)
    gm = jnp.max(gl, axis=-1, keepdims=True)
    g_prob = 1.0 / jnp.sum(jnp.exp(gl - gm), axis=-1, keepdims=True)
    gidx = jnp.min(jnp.where(gl == gm, lane, big), axis=-1, keepdims=True)
    e0 = EXPERT_LANE0 + gidx * MOE_EXPERTS
    el = jnp.where((lane >= e0) & (lane < e0 + MOE_EXPERTS), lg, neg)
    m1 = jnp.max(el, axis=-1, keepdims=True)
    i1 = jnp.min(jnp.where(el == m1, lane, big), axis=-1, keepdims=True)
    el2 = jnp.where(lane == i1, neg, el)
    m2 = jnp.max(el2, axis=-1, keepdims=True)
    i2 = jnp.min(jnp.where(el2 == m2, lane, big), axis=-1, keepdims=True)
    r = jnp.exp(m2 - m1)
    w1 = 1.0 / (1.0 + r)
    w2 = r * w1
    two_hot = jnp.where((lane == i1) | (lane == i2), 1.0, 0.0)
    tm = lg.shape[0]
    ltri = (lax.broadcasted_iota(jnp.int32, (tm, tm), 1) < lax.broadcasted_iota(jnp.int32, (tm, tm), 0))
    rank = carry_ref[0:1, :] + _dot(ltri.astype(BF16), two_hot.astype(BF16))
    pick = lambda idx: jnp.sum(jnp.where(lane == idx, rank, 0.0), axis=-1, keepdims=True)
    rec = jnp.zeros_like(lg)
    for col, val in ((R_E1, i1), (R_E2, i2), (R_RANK1, pick(i1)), (R_RANK2, pick(i2)),
                     (R_W1, w1 * g_prob), (R_W2, w2 * g_prob)):
        rec = jnp.where(lane == col, val, rec)
    carry_ref[...] = carry_ref[...] + jnp.sum(two_hot, axis=0, keepdims=True)
    return rec


def _plan_kernel(cnt_ref, seg_ref, tile_ref):
    cnt = cnt_ref[...]
    lane = lax.broadcasted_iota(jnp.int32, cnt.shape, 1)
    is_e = (lane >= EXPERT_LANE0) & (lane < EXPERT_LANE0 + N_EXPERTS)
    size = jnp.where(is_e, jnp.floor((cnt + (EXPERT_TR - 1)) * (1.0 / EXPERT_TR)) * EXPERT_TR, 0.0)
    upper = (lax.broadcasted_iota(jnp.int32, (ROUTE_LANES, ROUTE_LANES), 0)
             < lax.broadcasted_iota(jnp.int32, (ROUTE_LANES, ROUTE_LANES), 1)).astype(F32)
    start = jnp.dot(size, upper, precision=HIGHEST, preferred_element_type=F32)
    end = start + size
    total = jnp.max(end, axis=-1, keepdims=True)
    row = lax.broadcasted_iota(jnp.int32, cnt.shape, 0)
    seg = jnp.where(row == SEG_START, start, jnp.where(row == SEG_SIZE, size, jnp.where(
        row == SEG_END, end, jnp.where(row == SEG_TILES, total * (1.0 / EXPERT_TR), cnt))))
    seg_ref[...] = seg.astype(jnp.int32)
    nt = tile_ref.shape[0]
    t0 = (lax.broadcasted_iota(jnp.int32, (nt, ROUTE_LANES), 0) * EXPERT_TR).astype(F32)
    lane_t = lax.broadcasted_iota(jnp.int32, (nt, ROUTE_LANES), 1)
    done = jnp.where((lane_t >= EXPERT_LANE0) & (lane_t < EXPERT_LANE0 + N_EXPERTS) & (end[0:1, :] <= t0), 1.0, 0.0)
    te = jnp.minimum(jnp.sum(done, axis=-1, keepdims=True), N_EXPERTS - 1.0)
    tile_ref[...] = jnp.broadcast_to(te, (nt, ROUTE_LANES)).astype(jnp.int32)


def _plan(counts, n_tiles):
    nt = -(-n_tiles // 8) * 8
    return pl.pallas_call(
        _plan_kernel,
        out_shape=[jax.ShapeDtypeStruct((8, ROUTE_LANES), jnp.int32),
                   jax.ShapeDtypeStruct((nt, ROUTE_LANES), jnp.int32)],
        name="plan",
    )(counts)


def _positions_kernel(route_ref, seg_ref, pos_ref):
    rec = route_ref[...]
    start = seg_ref[0:1, :].astype(F32)
    lane = lax.broadcasted_iota(jnp.int32, rec.shape, 1).astype(F32)
    seg_start = lambda col: jnp.sum(jnp.where(lane == rec[:, col:col + 1], start, 0.0), axis=-1, keepdims=True)
    p1 = seg_start(R_E1) + rec[:, R_RANK1:R_RANK1 + 1]
    p2 = seg_start(R_E2) + rec[:, R_RANK2:R_RANK2 + 1]
    both = jnp.where(lane == 0.0, p1, jnp.where(lane == 1.0, p2, 0.0))
    pos_ref[...] = jnp.transpose(both)[0:8, :].astype(jnp.int32)


def _positions(route, seg):
    m = route.shape[0]
    return pl.pallas_call(
        _positions_kernel, grid=(m // ROUTE_TM,),
        in_specs=[pl.BlockSpec((ROUTE_TM, ROUTE_LANES), lambda i: (i, 0)),
                  pl.BlockSpec((8, ROUTE_LANES), lambda i: (0, 0))],
        out_specs=pl.BlockSpec((8, ROUTE_TM), lambda i: (0, i)),
        out_shape=jax.ShapeDtypeStruct((8, m), jnp.int32),
        compiler_params=_cparams("parallel"), name="positions",
    )(route, seg)


SEG_START, SEG_SIZE, SEG_END, SEG_TILES, SEG_COUNT = range(5)


def _row_copies(src_of, dst_of, sem):
    copies = [pltpu.make_async_copy(src_of(j, k), dst_of(j, k), sem) for j in range(MOVE_TM) for k in (0, 1)]
    for n, c in enumerate(copies):
        c.start(priority=n % 2)
    for c in copies:
        c.wait()


def _dispatch_kernel(seg_s, pos_s, h2_ref, xs_ref, zbuf, zsem, sem, *, n_tiles):
    @pl.when(pl.program_id(0) == 0)
    def _():
        zbuf[...] = jnp.zeros_like(zbuf)

        def zero_tiles(go):
            for e in range(N_EXPERTS):
                lane = EXPERT_LANE0 + e

                @pl.when(seg_s[SEG_SIZE, lane] > 0)
                def _():
                    start = pl.multiple_of(seg_s[SEG_END, lane] - EXPERT_TR, EXPERT_TR)
                    go(pltpu.make_async_copy(zbuf, xs_ref.at[pl.ds(start, EXPERT_TR)], zsem.at[e]))

                tile = n_tiles - N_EXPERTS + e

                @pl.when(tile >= seg_s[SEG_TILES, 0])
                def _():
                    dst = xs_ref.at[pl.ds(tile * EXPERT_TR, EXPERT_TR)]
                    go(pltpu.make_async_copy(zbuf, dst, zsem.at[N_EXPERTS + e]))

        zero_tiles(lambda c: c.start())
        zero_tiles(lambda c: c.wait())

    _row_copies(lambda j, k: h2_ref.at[pl.ds(j, 1)], lambda j, k: xs_ref.at[pl.ds(pos_s[k, j], 1)], sem)


def _dispatch(h2, pos, seg, n_tiles):
    m, d = h2.shape
    return pl.pallas_call(
        functools.partial(_dispatch_kernel, n_tiles=n_tiles),
        grid_spec=pltpu.PrefetchScalarGridSpec(
            num_scalar_prefetch=1, grid=(m // MOVE_TM,),
            in_specs=[pl.BlockSpec((8, MOVE_TM), lambda i, seg: (0, i), memory_space=pltpu.SMEM),
                      pl.BlockSpec((MOVE_TM, d), lambda i, seg: (i, 0))],
            out_specs=pl.BlockSpec(memory_space=pl.ANY),
            scratch_shapes=[pltpu.VMEM((EXPERT_TR, d), F32), pltpu.SemaphoreType.DMA((2 * N_EXPERTS,)),
                            pltpu.SemaphoreType.DMA]),
        out_shape=jax.ShapeDtypeStruct((n_tiles * EXPERT_TR, d), F32),
        compiler_params=_cparams("arbitrary"), name="dispatch",
    )(seg, pos, h2)


def _experts_kernel(te_s, seg_s, xs_ref, wg_ref, wu_ref, wd_ref, ys_ref):
    used = pl.program_id(0) < seg_s[SEG_TILES, 0]

    @pl.when(used)
    def _():
        x = xs_ref[...].astype(BF16)
        a = _dot(x, wg_ref[...].astype(BF16))
        b = _dot(x, wu_ref[...].astype(BF16))
        ys_ref[...] = _dot((_silu(a) * b).astype(BF16), wd_ref[...].astype(BF16))

    @pl.when(jnp.logical_not(used))
    def _():
        ys_ref[...] = jnp.zeros_like(ys_ref)


def _experts(xs, tile_expert, seg, w_gate, w_up, w_down, layer):
    p, d = xs.shape
    f = w_gate.shape[-1]
    last = lambda j, seg: jnp.minimum(j, seg[SEG_TILES, 0] - 1)
    wmap = lambda j, te, seg: (layer * N_EXPERTS + te[last(j, seg)], 0, 0)
    wspec = lambda shape: pl.BlockSpec(shape, wmap)
    return pl.pallas_call(
        _experts_kernel,
        grid_spec=pltpu.PrefetchScalarGridSpec(
            num_scalar_prefetch=2, grid=(p // EXPERT_TR,),
            in_specs=[pl.BlockSpec((EXPERT_TR, d), lambda j, te, seg: (last(j, seg), 0)),
                      wspec((None, d, f)), wspec((None, d, f)), wspec((None, f, d))],
            out_specs=pl.BlockSpec((EXPERT_TR, d), lambda j, te, seg: (j, 0))),
        out_shape=jax.ShapeDtypeStruct((p, d), F32),
        compiler_params=_cparams("arbitrary"), name="experts",
    )(tile_expert, seg, xs, w_gate, w_up, w_down)


def _combine_kernel(pos_s, route_ref, x1_ref, mod_ref, nw_ref, ys_ref, o_ref, y1buf, y2buf, sem, *, final_norm):
    bufs = (y1buf, y2buf)
    _row_copies(lambda j, k: ys_ref.at[pl.ds(pos_s[k, j], 1)], lambda j, k: bufs[k].at[pl.ds(j, 1)], sem)
    rec = route_ref[...]
    moe = rec[:, R_W1:R_W1 + 1] * y1buf[...] + rec[:, R_W2:R_W2 + 1] * y2buf[...]
    x = x1_ref[...] + mod_ref[5:6, :] * moe
    if final_norm:
        x = x * lax.rsqrt(jnp.mean(x * x, axis=-1, keepdims=True) + EPS) * nw_ref[...]
    o_ref[...] = x


def _combine(ys, pos, route, x1, mod, nw_final, rows_per_mod, final_norm):
    m, d = x1.shape
    row = lambda i: (i, 0)
    return pl.pallas_call(
        functools.partial(_combine_kernel, final_norm=final_norm),
        grid=(m // MOVE_TM,),
        in_specs=[pl.BlockSpec((8, MOVE_TM), lambda i: (0, i), memory_space=pltpu.SMEM),
                  pl.BlockSpec((MOVE_TM, ROUTE_LANES), row), pl.BlockSpec((MOVE_TM, d), row),
                  pl.BlockSpec((None, 8, d), lambda i: ((i * MOVE_TM) // rows_per_mod, 0, 0)),
                  pl.BlockSpec((1, d), lambda i: (0, 0)),
                  pl.BlockSpec(memory_space=pl.ANY)],
        out_specs=pl.BlockSpec((MOVE_TM, d), row),
        out_shape=jax.ShapeDtypeStruct((m, d), F32),
        scratch_shapes=[pltpu.VMEM((MOVE_TM, d), F32), pltpu.VMEM((MOVE_TM, d), F32), pltpu.SemaphoreType.DMA],
        compiler_params=_cparams("arbitrary"), name="combine",
    )(pos, route, x1, mod, nw_final, ys)


def _layer_weights(P, l):
    d = P['w_in'].shape[1]
    cw = P['pool_scale'].shape[1]
    hw = P['hyena_conv'].shape[2]
    o1, o2, o3 = cw, cw + hw, cw + hw + 3 * cw
    na = (P['w_in'].shape[2] - o3) // 3
    cols = ((o1, hw), (o2, 3 * cw), (o3, na), (o3 + na, na), (o3 + 2 * na, na), (0, cw))
    gw = cw // len(POOL_WINDOWS)
    pool_bd = jnp.zeros((cw, cw), F32)
    for g in range(len(POOL_WINDOWS)):
        pool_bd = pool_bd.at[g * gw:(g + 1) * gw, g * gw:(g + 1) * gw].set(P['pool_w'][l, g])
    wr = jnp.concatenate([P['w_route_group'][l],
                          jnp.transpose(P['w_route_exp'][l], (1, 0, 2)).reshape(d, N_EXPERTS)], axis=1)
    wr = jnp.zeros((d, ROUTE_LANES), F32).at[:, :wr.shape[1]].set(wr)
    wr_hi = wr.astype(BF16)
    wr_lo = (wr - wr_hi.astype(F32)).astype(BF16)
    br = jnp.concatenate([P['b_route_group'][l], P['b_route_exp'][l].reshape(-1)])
    br = jnp.zeros((1, ROUTE_LANES), F32).at[0, :br.shape[0]].set(br)
    return dict(
        cols=cols, w_in=P['w_in'][l].astype(BF16), norm_mix=P['norm_mix'][l][None], norm_ffn=P['norm_ffn'][l][None],
        pool_bd=pool_bd.astype(BF16), pool_scale=P['pool_scale'][l][None], sconv_w=P['sconv_w'][l],
        hyena_conv=P['hyena_conv'][l], hyena_bias=P['hyena_bias'][l],
        merge=(P['w_gate'][l].astype(BF16), P['b_gate'][l][None], P['w_br_a'][l].astype(BF16),
               P['w_br_b'][l].astype(BF16), P['w_br_c'][l].astype(BF16), P['w_br_d'][l].astype(BF16),
               P['w_out'][l].astype(BF16), jnp.stack([wr_hi, wr_lo]), br),
    )


def _run_stream(x3, mods, LW, EW, nw_final, hy, attend, depth):
    b, seq_len, d = x3.shape
    m = b * seq_len
    x = x3.reshape(m, d)
    rows_per_mod = m // mods.shape[1]
    tm = min(512, seq_len)
    tc = min(256, seq_len)
    kvs = []
    for l in range(depth):
        W = LW[l]
        mod = mods[l]
        cw = W['pool_scale'].shape[1]
        h, ph, ps, q, k, v, pp = _inproj(x, mod, W['norm_mix'], W['w_in'], W['cols'],
                                         min(INPROJ_TM, rows_per_mod), rows_per_mod)
        kvs.append((k, v))
        ya, yc, u_t, x0_t = _local_mixers(pp, ps, ph, W['pool_bd'], W['pool_scale'], W['sconv_w'],
                                          W['hyena_conv'], seq_len, tc)
        yb_t = _hyena_conv(u_t, x0_t, hy['fwd_bf'], hy['inv_c'], hy['inv_s'], hy['spec'][l],
                           W['hyena_bias'], cw)
        yd = attend(l, q, k, v)
        x1, h2, route, counts = _merge(x, h, ya, yb_t, yc, yd, mod, W['norm_ffn'], W['merge'], tm,
                                       rows_per_mod, seq_len)
        n_tiles = 2 * m // EXPERT_TR + N_EXPERTS
        seg, tile_tab = _plan(counts, n_tiles)
        pos = _positions(route, seg)
        xs = _dispatch(h2, pos, seg, n_tiles)
        ys = _experts(xs, tile_tab[:, 0], seg, EW[0], EW[1], EW[2], l)
        x = _combine(ys, pos, route, x1, mod, nw_final, rows_per_mod, final_norm=(l == depth - 1))
    return x, kvs


def _hyena_setup(seq_len, P, depth):
    fwd32, inv_c, inv_s = _dft_matrices(seq_len)
    spec = []
    for l in range(depth):
        filt = _hyena_filter(seq_len, P['hyena_f1'][l], P['hyena_fb1'][l], P['hyena_f2'][l],
                             P['hyena_fb2'][l], P['hyena_f3'][l], P['hyena_freq'][l], P['hyena_decay'][l])
        spec.append(_filter_spectrum(fwd32, filt))
    return dict(fwd_bf=fwd32.astype(BF16), inv_c=inv_c, inv_s=inv_s, spec=spec)


def kernel(x_prompt, x_sample, cache_k, cache_v, c, c_ctx, w_ada, b_ada, norm_mix, w_in, w_gate, b_gate, pool_w, pool_scale, hyena_conv, hyena_f1, hyena_fb1, hyena_f2, hyena_fb2, hyena_f3, hyena_freq, hyena_decay, hyena_bias, sconv_w, na_rpb, w_br_a, w_br_b, w_br_c, w_br_d, w_out, norm_ffn, w_route_group, b_route_group, w_route_exp, b_route_exp, w_e_gate, w_e_up, w_e_down, norm_final):
    P = dict(w_in=w_in, w_gate=w_gate, b_gate=b_gate, pool_w=pool_w, pool_scale=pool_scale,
             hyena_conv=hyena_conv, hyena_f1=hyena_f1, hyena_fb1=hyena_fb1, hyena_f2=hyena_f2,
             hyena_fb2=hyena_fb2, hyena_f3=hyena_f3, hyena_freq=hyena_freq, hyena_decay=hyena_decay,
             hyena_bias=hyena_bias, sconv_w=sconv_w, w_br_a=w_br_a, w_br_b=w_br_b, w_br_c=w_br_c,
             w_br_d=w_br_d, w_out=w_out, norm_mix=norm_mix, norm_ffn=norm_ffn,
             w_route_group=w_route_group, b_route_group=b_route_group, w_route_exp=w_route_exp,
             b_route_exp=b_route_exp, w_e_gate=w_e_gate, w_e_up=w_e_up, w_e_down=w_e_down)
    depth, d, _ = w_ada.shape
    bp, lp, _ = x_prompt.shape
    bs, ls, _ = x_sample.shape
    assert (ls // GRID_W) % LAT_QROWS == 0 and ls // GRID_W >= LAT_KROWS

    n_c = 1 + bs
    n_pad = -(-n_c // 8) * 8
    cvecs = jnp.zeros((n_pad, d), F32).at[0].set(c_ctx).at[1:n_c].set(c)
    ada = _ada(cvecs, w_ada, b_ada).reshape(depth, n_pad, 6, d)
    ada = jnp.concatenate([ada, jnp.zeros((depth, n_pad, 2, d), F32)], axis=2)

    LW = [_layer_weights(P, l) for l in range(depth)]
    nw_final = norm_final[None]
    f = w_e_gate.shape[-1]
    EW = (w_e_gate.reshape(depth * N_EXPERTS, d, f), w_e_up.reshape(depth * N_EXPERTS, d, f),
          w_e_down.reshape(depth * N_EXPERTS, f, d))

    hy_p = _hyena_setup(lp, P, depth)
    xp, kv_p = _run_stream(x_prompt, ada[:, 0:1], LW, EW, nw_final, hy_p,
                           lambda l, q, k, v: _context_attention(q, k, v, lp), depth)
    y_prompt = xp.reshape(bp, lp, d)
    new_k = jnp.stack([k.reshape(bp, lp, N_HEADS, HEAD_DIM) for k, _ in kv_p], axis=1)
    new_v = jnp.stack([v.reshape(bp, lp, N_HEADS, HEAD_DIM) for _, v in kv_p], axis=1)

    hy_s = _hyena_setup(ls, P, depth)
    past = cache_k.shape[2]
    ck = cache_k.reshape(bs, depth, past, N_HEADS * HEAD_DIM)
    cv = cache_v.reshape(bs, depth, past, N_HEADS * HEAD_DIM)
    biases = [_latent_bias(na_rpb[l], ls // GRID_W) for l in range(depth)]
    xs, _ = _run_stream(x_sample, ada[:, 1:n_c], LW, EW, nw_final, hy_s,
                        lambda l, q, k, v: _latent_attention(q, k, v, ck, cv, l, biases[l], ls),
                        depth)
    y_sample = xs.reshape(bs, ls, d)
    return (y_prompt, y_sample, new_k, new_v)
```

```python
import functools
import math

import numpy as np
import jax
import jax.numpy as jnp
from jax import lax
from jax.experimental import pallas as pl
from jax.experimental.pallas import tpu as pltpu

F32 = jnp.float32
BF16 = jnp.bfloat16
HIGHEST = lax.Precision.HIGHEST

EPS = 1e-6
GRID_W = 64
NA_ROWS = 8
NA_COLS = 16
N_HEADS = 8
HEAD_DIM = 64
POOL_WINDOWS = (2, 4, 8, 16)
HYENA_BANDS = 16
MOE_GROUPS = 4
MOE_EXPERTS = 8
N_EXPERTS = MOE_GROUPS * MOE_EXPERTS
ROUTE_LANES = 128
EXPERT_LANE0 = MOE_GROUPS
HALO = 8
NEG_BIG = -1e30
VMEM_LIMIT_BYTES = 48 * 1024 * 1024


def _cparams(*sem):
    return pltpu.CompilerParams(dimension_semantics=sem, vmem_limit_bytes=VMEM_LIMIT_BYTES)


def _resident(shape):
    nd = len(shape)
    return pl.BlockSpec(shape, lambda *_: (0,) * nd, pipeline_mode=pl.Buffered(1))


def _dot(a, b):
    return jnp.dot(a, b, preferred_element_type=F32)


def _silu(x):
    return x * jax.nn.sigmoid(x)


def _ada_kernel(cv_ref, w_ref, b_ref, o_ref):
    o_ref[...] = jnp.dot(_silu(cv_ref[...]), w_ref[...], precision=HIGHEST,
                         preferred_element_type=F32) + b_ref[...]


def _ada(cvecs, w_ada, b_ada):
    depth, d, d6 = w_ada.shape
    r = cvecs.shape[0]
    return pl.pallas_call(
        _ada_kernel,
        grid=(depth, d6 // d),
        in_specs=[pl.BlockSpec((r, d), lambda l, j: (0, 0)),
                  pl.BlockSpec((None, d, d), lambda l, j: (l, 0, j)),
                  pl.BlockSpec((None, 1, d), lambda l, j: (l, 0, j))],
        out_specs=pl.BlockSpec((None, r, d), lambda l, j: (l, 0, j)),
        out_shape=jax.ShapeDtypeStruct((depth, r, d6), F32),
        compiler_params=_cparams("parallel", "parallel"),
        name="ada",
    )(cvecs, w_ada, b_ada.reshape(depth, 1, d6))


def _rms_mod(x, nw, shift, scale):
    y = x * lax.rsqrt(jnp.mean(x * x, axis=-1, keepdims=True) + EPS) * nw
    return y * (1.0 + scale) + shift


def _inproj_kernel(x_ref, mod_ref, nw_ref, w_ref, h_ref, ph_ref, ps_ref, q_ref, k_ref, v_ref, pp_ref,
                   *, cols):
    mod = mod_ref[...]
    h = _rms_mod(x_ref[...], nw_ref[...], mod[0:1, :], mod[1:2, :]).astype(BF16)
    h_ref[...] = h
    for ref, (off, wd) in zip((ph_ref, ps_ref, q_ref, k_ref, v_ref, pp_ref), cols):
        r = _dot(h, w_ref[:, off:off + wd])
        if ref is q_ref:
            r = r * (HEAD_DIM ** -0.5)
        ref[...] = r.astype(ref.dtype)


def _inproj(x, mod, nw, w_in, cols, tm, rows_per_mod):
    m, d = x.shape
    n = w_in.shape[1]
    widths = [wd for _, wd in cols]
    dts = (F32, F32, BF16, F32, F32, F32)
    row = lambda i: (i, 0)
    return pl.pallas_call(
        functools.partial(_inproj_kernel, cols=cols),
        grid=(m // tm,),
        in_specs=[pl.BlockSpec((tm, d), row),
                  pl.BlockSpec((None, 8, d), lambda i: ((i * tm) // rows_per_mod, 0, 0)),
                  pl.BlockSpec((1, d), lambda i: (0, 0)),
                  _resident((d, n))],
        out_specs=[pl.BlockSpec((tm, d), row)] + [pl.BlockSpec((tm, wd), row) for wd in widths],
        out_shape=[jax.ShapeDtypeStruct((m, d), BF16)]
        + [jax.ShapeDtypeStruct((m, wd), dt) for wd, dt in zip(widths, dts)],
        compiler_params=_cparams("parallel"),
        name="inproj",
    )(x, mod, nw, w_in)


def _fill_padded(pad_ref, prev_ref, cur_ref, next_ref, first, last, tc):
    zero = jnp.zeros((HALO, cur_ref.shape[1]), F32)
    pad_ref[0:HALO, :] = jnp.where(first, zero, prev_ref[...])
    pad_ref[HALO:HALO + tc, :] = cur_ref[...]
    pad_ref[HALO + tc:2 * HALO + tc, :] = jnp.where(last, zero, next_ref[...])


def _local_kernel(pp_ref, pp_prev, pp_next, ps_ref, ps_prev, ps_next, ph_ref, ph_prev, ph_next,
                  pw_ref, pscale_ref, sw_ref, hw_ref, band_ref,
                  ya_ref, yc_ref, u_ref, x0_ref,
                  pad_p, pad_s, pad_h, *, seq_len, tc):
    nchunk = seq_len // tc
    j = pl.program_id(0) % nchunk
    first = j == 0
    last = j == nchunk - 1
    cw = pp_ref.shape[1]
    sw = cw

    _fill_padded(pad_p, pp_prev, pp_ref, pp_next, first, last, tc)
    u = pp_ref[...]
    padded = pad_p[...].astype(BF16)
    sums = {win: _dot(band_ref[g], padded) for g, win in enumerate(POOL_WINDOWS)}
    t = j * tc + lax.broadcasted_iota(jnp.int32, (tc, 1), 0)
    lane = lax.broadcasted_iota(jnp.int32, (1, cw), 1)
    gw = cw // len(POOL_WINDOWS)
    pooled = None
    for g, win in reversed(list(enumerate(POOL_WINDOWS))):
        cnt = jnp.minimum(t - win // 2 + win, seq_len) - jnp.maximum(t - win // 2, 0)
        val = sums[win] * (1.0 / cnt.astype(F32))
        pooled = val if pooled is None else jnp.where(lane < (g + 1) * gw, val, pooled)
    pooled = pooled - u
    ya = _dot(pooled.astype(BF16), pw_ref[...]) * pscale_ref[...]
    ya_ref[...] = ya.astype(BF16)

    _fill_padded(pad_s, ps_prev, ps_ref, ps_next, first, last, tc)
    w3 = sw_ref[...]
    z = lambda k: (pad_s[HALO + k:HALO + k + tc, 2 * sw:3 * sw] * pad_s[HALO + k:HALO + k + tc, 0:sw])
    conv = w3[0:1, :] * z(-1) + w3[1:2, :] * z(0) + w3[2:3, :] * z(1)
    yc_ref[...] = (ps_ref[:, sw:2 * sw] * conv).astype(BF16)

    _fill_padded(pad_h, ph_prev, ph_ref, ph_next, first, last, tc)
    hw = hw_ref[...]
    c3 = (hw[0:1, :] * pad_h[HALO - 1:HALO - 1 + tc, :] + hw[1:2, :] * pad_h[HALO:HALO + tc, :]
          + hw[2:3, :] * pad_h[HALO + 1:HALO + 1 + tc, :])
    x0_ref[...] = c3[:, 0:sw]
    u_ref[...] = c3[:, sw:2 * sw] * c3[:, 2 * sw:3 * sw]


def _local_mixers(pp, ps, ph, pool_bd, pool_scale, sconv_w, hyena_conv, seq_len, tc):
    m, cw = pp.shape
    nb = m // seq_len
    nchunk = seq_len // tc
    hpc = tc // HALO
    nhb = m // HALO
    row = lambda i: (i, 0)
    prev = lambda i: (jnp.maximum(i * hpc - 1, 0), 0)
    nxt = lambda i: (jnp.minimum((i + 1) * hpc, nhb - 1), 0)
    tl = lambda i: (i % nchunk, i // nchunk)
    const = lambda i: (0, 0)

    def trio(width):
        return [pl.BlockSpec((tc, width), row), pl.BlockSpec((HALO, width), prev),
                pl.BlockSpec((HALO, width), nxt)]

    band = np.zeros((len(POOL_WINDOWS), tc, tc + 2 * HALO), np.float32)
    for g, win in enumerate(POOL_WINDOWS):
        for k in range(-(win // 2), win // 2):
            band[g, np.arange(tc), HALO + np.arange(tc) + k] = 1.0
    band = jnp.asarray(band, BF16)

    return pl.pallas_call(
        functools.partial(_local_kernel, seq_len=seq_len, tc=tc),
        grid=(m // tc,),
        in_specs=trio(cw) + trio(3 * cw) + trio(3 * cw)
        + [pl.BlockSpec((cw, cw), const), pl.BlockSpec((1, cw), const),
           pl.BlockSpec((3, cw), const), pl.BlockSpec((3, 3 * cw), const), _resident(band.shape)],
        out_specs=[pl.BlockSpec((tc, cw), row), pl.BlockSpec((tc, cw), row),
                   pl.BlockSpec((tc, cw), tl), pl.BlockSpec((tc, cw), tl)],
        out_shape=[jax.ShapeDtypeStruct((m, cw), BF16), jax.ShapeDtypeStruct((m, cw), BF16),
                   jax.ShapeDtypeStruct((seq_len, nb * cw), F32),
                   jax.ShapeDtypeStruct((seq_len, nb * cw), F32)],
        scratch_shapes=[pltpu.VMEM((tc + 2 * HALO, cw), F32), pltpu.VMEM((tc + 2 * HALO, 3 * cw), F32),
                        pltpu.VMEM((tc + 2 * HALO, 3 * cw), F32)],
        compiler_params=_cparams("parallel"),
        name="local_mixers",
    )(pp, pp, pp, ps, ps, ps, ph, ph, ph, pool_bd, pool_scale, sconv_w, hyena_conv, band)


def _trig(rows, cols, n):
    split = 64
    r = np.asarray(rows, np.int64)[:, None]
    c = np.asarray(cols, np.int64)
    assert c[0] % split == 0 and len(c) % split == 0 and np.all(np.diff(c) == 1)
    c0 = np.arange(split)[None, :]
    c1 = c[::split][None, :]
    ang0 = ((r * c0) % n) * (2.0 * math.pi / n)
    ang1 = ((r * c1) % n) * (2.0 * math.pi / n)
    tab = lambda a: jnp.asarray(a, F32)
    ca, sa = tab(np.cos(ang0))[:, None, :], tab(np.sin(ang0))[:, None, :]
    cb, sb = tab(np.cos(ang1))[:, :, None], tab(np.sin(ang1))[:, :, None]
    shape = (len(rows), len(c))
    return (ca * cb - sa * sb).reshape(shape), (sa * cb + ca * sb).reshape(shape)


def _dft_matrices(seq_len):
    L = seq_len
    n = 2 * L
    k = np.arange(L)
    first = jnp.asarray(k == 0)
    alt = jnp.asarray(np.where(k % 2 == 0, 1.0, -1.0), F32)
    cos2, sin2 = _trig(np.concatenate([k, k]), k, n)
    re_rows = jnp.asarray(np.arange(n) < L)[:, None]
    nyq_row = jnp.asarray(np.arange(n) == L)[:, None]
    fwd = jnp.where(re_rows, cos2, jnp.where(nyq_row, alt[None, :], -sin2))
    tp = np.arange(L // 2, L // 2 + L)
    cos_i, sin_i = _trig(tp, k, n)
    alt_t = jnp.asarray(np.where(tp % 2 == 0, 1.0, -1.0) / n, F32)
    inv_c = jnp.where(first[None, :], 1.0 / n, cos_i * (2.0 / n))
    inv_s = jnp.where(first[None, :], alt_t[:, None], sin_i * (-2.0 / n))
    return fwd, inv_c.astype(BF16), inv_s.astype(BF16)


def _hyena_embedding(seq_len, width):
    t = np.arange(seq_len, dtype=np.float64)
    w = (2.0 * math.pi / seq_len) * t
    bands = np.linspace(1e-4, HYENA_BANDS - 1, HYENA_BANDS)
    z = np.concatenate([(t / (seq_len - 1))[:, None], np.cos(w[:, None] * bands),
                        -np.sin(w[:, None] * bands)], axis=-1)
    out = np.zeros((seq_len, width), np.float32)
    out[:, :z.shape[1]] = z
    return jnp.asarray(out)


def _filter_kernel(z_ref, f1_ref, fb1_ref, f2_ref, fb2_ref, f3_ref, freq_ref, decay_ref, o_ref, *, seq_len):
    hdot = lambda a, b: jnp.dot(a, b, precision=HIGHEST, preferred_element_type=F32)
    fr = freq_ref[...]
    hdn = jnp.sin(fr * (hdot(z_ref[...], f1_ref[...]) + fb1_ref[...]))
    hdn = jnp.sin(fr * (hdot(hdn, f2_ref[...]) + fb2_ref[...]))
    filt = hdot(hdn, f3_ref[...])
    t = lax.broadcasted_iota(jnp.int32, (seq_len, 1), 0)
    dist = jnp.abs(t - seq_len // 2).astype(F32) / (seq_len / 2)
    filt = filt * jnp.exp(-dist * decay_ref[...])
    o_ref[...] = filt / jnp.sum(jnp.abs(filt), axis=0, keepdims=True)


def _pad2(a, rows, cols):
    return jnp.zeros((rows, cols), F32).at[:a.shape[0], :a.shape[1]].set(a)


def _hyena_filter(seq_len, f1, fb1, f2, fb2, f3, freq, decay):
    p = 128
    c = f3.shape[1]
    args = (_hyena_embedding(seq_len, p), _pad2(f1, p, p), _pad2(fb1[None], 1, p), _pad2(f2, p, p),
            _pad2(fb2[None], 1, p), _pad2(f3, p, c), _pad2(freq[None], 1, p), decay[None])
    return pl.pallas_call(
        functools.partial(_filter_kernel, seq_len=seq_len),
        out_shape=jax.ShapeDtypeStruct((seq_len, c), F32),
        compiler_params=pltpu.CompilerParams(vmem_limit_bytes=VMEM_LIMIT_BYTES),
        name="hyena_filter",
    )(*args)


def _spectrum_kernel(f_ref, h_ref, o_ref):
    o_ref[...] = jnp.dot(f_ref[...], h_ref[...], precision=HIGHEST, preferred_element_type=F32)


def _filter_spectrum(fwd32, filt):
    n, L = fwd32.shape
    c = filt.shape[1]
    tf = min(n, 512)
    return pl.pallas_call(
        _spectrum_kernel,
        grid=(n // tf,),
        in_specs=[pl.BlockSpec((tf, L), lambda i: (i, 0)), pl.BlockSpec((L, c), lambda i: (0, 0))],
        out_specs=pl.BlockSpec((tf, c), lambda i: (i, 0)),
        out_shape=jax.ShapeDtypeStruct((n, c), F32),
        compiler_params=_cparams("parallel"),
        name="filter_spectrum",
    )(fwd32, filt)


def _hy_fwd_kernel(fc_ref, fs_ref, u_ref, hr_ref, hi_ref, yr_ref, yi_ref, *, cw):
    ub = u_ref[...].astype(BF16)
    ur = _dot(fc_ref[...], ub)
    ui = _dot(fs_ref[...], ub)
    tf = ur.shape[0]
    row0 = (pl.program_id(0) * tf + lax.broadcasted_iota(jnp.int32, (tf, 1), 0)) == 0
    hr, hi = hr_ref[...], hi_ref[...]
    for s in range(ur.shape[1] // cw):
        a, b = ur[:, s * cw:(s + 1) * cw], ui[:, s * cw:(s + 1) * cw]
        yr = jnp.where(row0, a * hr, a * hr - b * hi)
        yi = jnp.where(row0, b * hi, a * hi + b * hr)
        yr_ref[:, s * cw:(s + 1) * cw] = yr.astype(BF16)
        yi_ref[:, s * cw:(s + 1) * cw] = yi.astype(BF16)


def _hy_inv_kernel(ic_ref, is_ref, yr_ref, yi_ref, u_ref, x0_ref, bias_ref, o_ref):
    conv = _dot(ic_ref[...], yr_ref[...]) + _dot(is_ref[...], yi_ref[...])
    o_ref[...] = ((conv + bias_ref[...] * u_ref[...]) * x0_ref[...]).astype(BF16)


def _hyena_conv(u_t, x0_t, fwd_bf, inv_c, inv_s, spec, bias, cw):
    L, ncol = u_t.shape
    tf = min(L, 1024)
    tn = min(ncol, 512)
    nf = L // tf
    yr, yi = pl.pallas_call(
        functools.partial(_hy_fwd_kernel, cw=cw),
        grid=(nf, ncol // tn),
        in_specs=[pl.BlockSpec((tf, L), lambda i, j: (i, 0)),
                  pl.BlockSpec((tf, L), lambda i, j: (nf + i, 0)),
                  pl.BlockSpec((L, tn), lambda i, j: (0, j)),
                  pl.BlockSpec((tf, cw), lambda i, j: (i, 0)),
                  pl.BlockSpec((tf, cw), lambda i, j: (nf + i, 0))],
        out_specs=[pl.BlockSpec((tf, tn), lambda i, j: (i, j))] * 2,
        out_shape=[jax.ShapeDtypeStruct((L, ncol), BF16)] * 2,
        compiler_params=_cparams("parallel", "parallel"),
        name="hyena_dft",
    )(fwd_bf, fwd_bf, u_t, spec, spec)
    bias_t = jnp.tile(bias[None, :], (1, tn // cw))
    return pl.pallas_call(
        _hy_inv_kernel,
        grid=(nf, ncol // tn),
        in_specs=[pl.BlockSpec((tf, L), lambda i, j: (i, 0)),
                  pl.BlockSpec((tf, L), lambda i, j: (i, 0)),
                  pl.BlockSpec((L, tn), lambda i, j: (0, j)),
                  pl.BlockSpec((L, tn), lambda i, j: (0, j)),
                  pl.BlockSpec((tf, tn), lambda i, j: (i, j)),
                  pl.BlockSpec((tf, tn), lambda i, j: (i, j)),
                  pl.BlockSpec((1, tn), lambda i, j: (0, 0))],
        out_specs=pl.BlockSpec((tf, tn), lambda i, j: (i, j)),
        out_shape=jax.ShapeDtypeStruct((L, ncol), BF16),
        compiler_params=_cparams("parallel", "parallel"),
        name="hyena_idft",
    )(inv_c, inv_s, yr, yi, u_t, x0_t, bias_t)


def _qkt(q, k):
    return lax.dot_general(q, k, (((1,), (1,)), ((), ())), preferred_element_type=F32)


PAIR = 2 * HEAD_DIM


def _attend_pair(q_pair, parts, bias_of=None):
    low = lax.broadcasted_iota(jnp.int32, (1, PAIR), 1) < HEAD_DIM
    one = jnp.ones((), BF16)
    accs = []
    for half in (0, 1):
        mine = low if half == 0 else jnp.logical_not(low)
        qh = jnp.where(mine, q_pair, jnp.zeros((), BF16))
        scores = []
        for i, (k, _) in enumerate(parts):
            s = _qkt(qh, k)
            b = None if bias_of is None else bias_of(half, i)
            scores.append(s if b is None else s + b)
        m = None
        for s in scores:
            mi = jnp.max(s, axis=-1, keepdims=True)
            m = mi if m is None else jnp.maximum(m, mi)
        acc = None
        for s, (_, v) in zip(scores, parts):
            p = jnp.exp((s - m).astype(BF16))
            o = _dot(p, jnp.where(mine, v, one))
            acc = o if acc is None else acc + o
        accs.append(acc)
    num = jnp.where(low, accs[0], accs[1])
    den = pltpu.roll(jnp.where(low, accs[1], accs[0]), HEAD_DIM, axis=1)
    return num * (1.0 / den)


def _ctx_attn_kernel(q_ref, k_ref, v_ref, o_ref):
    for j in range(N_HEADS // 2):
        sl = slice(j * PAIR, (j + 1) * PAIR)
        parts = [(k_ref[:, sl].astype(BF16), v_ref[:, sl].astype(BF16))]
        o_ref[:, sl] = _attend_pair(q_ref[:, sl], parts).astype(BF16)


def _context_attention(q, k, v, seq_len):
    m, w = q.shape
    spec = pl.BlockSpec((seq_len, w), lambda b: (b, 0))
    return pl.pallas_call(
        _ctx_attn_kernel,
        grid=(m // seq_len,),
        in_specs=[spec, spec, spec],
        out_specs=spec,
        out_shape=jax.ShapeDtypeStruct((m, w), BF16),
        compiler_params=_cparams("parallel"),
        name="context_attention",
    )(q, k, v)


LAT_QROWS = 4
LAT_KROWS = NA_ROWS + LAT_QROWS
LAT_TQ = LAT_QROWS * GRID_W


def _lat_attn_kernel(q_ref, k0_ref, k1_ref, k2_ref, v0_ref, v1_ref, v2_ref, ck_ref, cv_ref, g_ref, o_ref,
                     *, rows):
    nblk = rows // LAT_QROWS
    rblk = pl.program_id(0)
    kblk = jnp.clip(rblk - 1, 0, nblk - 3)
    slot = {}
    for ri in range(LAT_QROWS):
        r = rblk * LAT_QROWS + ri
        rs = jnp.clip(r - NA_ROWS // 2, 0, rows - NA_ROWS)
        for kj in range(LAT_KROWS):
            krow = kblk * LAT_QROWS + kj
            in_window = (krow >= rs) & (krow < rs + NA_ROWS)
            slot[ri, kj] = jnp.where(in_window, krow - r + NA_ROWS - 1, 2 * NA_ROWS - 1)
    low_half = lax.broadcasted_iota(jnp.int32, (GRID_W, 2 * GRID_W), 1) < GRID_W

    def bias_block(h, i):
        row_blocks = []
        for ri in range(LAT_QROWS):
            tiles = []
            for kj in range(LAT_QROWS * i, LAT_QROWS * (i + 1), 2):
                tiles.append(jnp.where(low_half, g_ref[h, slot[ri, kj]], g_ref[h, slot[ri, kj + 1]]))
            row_blocks.append(jnp.concatenate(tiles, axis=1))
        return jnp.concatenate(row_blocks, axis=0)

    n_local = 3
    for j in range(N_HEADS // 2):
        sl = slice(j * PAIR, (j + 1) * PAIR)
        parts = [(kr[:, sl].astype(BF16), vr[:, sl].astype(BF16))
                 for kr, vr in ((k0_ref, v0_ref), (k1_ref, v1_ref), (k2_ref, v2_ref), (ck_ref, cv_ref))]
        bias_of = lambda half, i, j=j: bias_block(2 * j + half, i) if i < n_local else None
        o_ref[:, sl] = _attend_pair(q_ref[:, sl], parts, bias_of).astype(BF16)


def _latent_bias(rpb, rows):
    del rows
    nh, ndr, ndc = rpb.shape
    qc = np.arange(GRID_W)[:, None]
    kc = (np.arange(2 * GRID_W) % GRID_W)[None, :]
    ws = np.clip(qc - NA_COLS // 2, 0, GRID_W - NA_COLS)
    col_ok = ((kc >= ws) & (kc < ws + NA_COLS)).reshape(-1)
    dc = np.clip(kc - qc + NA_COLS - 1, 0, ndc - 1).reshape(-1)
    onehot = jnp.asarray((dc[None, :] == np.arange(ndc)[:, None]) & col_ok[None, :], F32)
    rpb_ext = jnp.concatenate([rpb, jnp.zeros((nh, 2 * NA_ROWS - ndr, ndc), F32)], axis=1)
    g = jnp.einsum('hrd,dx->hrx', rpb_ext, onehot, precision=HIGHEST)
    mask = np.where(col_ok[None, :] & (np.arange(2 * NA_ROWS) < ndr)[:, None], 0.0, NEG_BIG)
    g = g + jnp.asarray(mask, F32)
    return g.reshape(nh, 2 * NA_ROWS, GRID_W, 2 * GRID_W)


def _latent_attention(q, k, v, ctx_k, ctx_v, layer, bias, seq_len):
    m, w = q.shape
    rows = seq_len // GRID_W
    nblk = rows // LAT_QROWS
    nb = m // seq_len
    tq = LAT_TQ
    nctx = ctx_k.shape[2]

    def kspec(i):
        return pl.BlockSpec((tq, w), lambda r, b: (b * nblk + jnp.clip(r - 1, 0, nblk - 3) + i, 0))

    qspec = pl.BlockSpec((tq, w), lambda r, b: (b * nblk + r, 0))
    cspec = pl.BlockSpec((None, None, nctx, w), lambda r, b: (b, layer, 0, 0))
    return pl.pallas_call(
        functools.partial(_lat_attn_kernel, rows=rows),
        grid=(nblk, nb),
        in_specs=[qspec, kspec(0), kspec(1), kspec(2), kspec(0), kspec(1), kspec(2), cspec, cspec,
                  _resident(bias.shape)],
        out_specs=qspec,
        out_shape=jax.ShapeDtypeStruct((m, w), BF16),
        compiler_params=_cparams("parallel", "parallel"),
        name="latent_attention",
    )(q, k, k, k, v, v, v, ctx_k, ctx_v, bias)


def _merge_kernel(x_ref, h_ref, ya_ref, yb_ref, yc_ref, yd_ref, mod_ref, nw_ref,
                  wg_ref, bg_ref, wa_ref, wb_ref, wc_ref, wd_ref, wo_ref, wr_ref, br_ref,
                  x1_ref, h2_ref, route_ref, cnt_ref, carry_ref):
    @pl.when(pl.program_id(0) == 0)
    def _():
        carry_ref[...] = jnp.zeros_like(carry_ref)

    d = x_ref.shape[1]
    h = h_ref[...]
    merged = None
    for i, (y_ref, w_ref) in enumerate(((ya_ref, wa_ref), (yb_ref, wb_ref), (yc_ref, wc_ref), (yd_ref, wd_ref))):
        gate = jax.nn.sigmoid(_dot(h, wg_ref[:, i * d:(i + 1) * d]) + bg_ref[:, i * d:(i + 1) * d])
        term = gate * _dot(y_ref[...], w_ref[...])
        merged = term if merged is None else merged + term
    mod = mod_ref[...]
    x1 = x_ref[...] + mod[2:3, :] * _dot(merged.astype(BF16), wo_ref[...])
    x1_ref[...] = x1
    h2 = _rms_mod(x1, nw_ref[...], mod[3:4, :], mod[4:5, :])
    h2_ref[...] = h2
    h2_hi = h2.astype(BF16)
    h2_lo = (h2 - h2_hi.astype(F32)).astype(BF16)
    logits = (_dot(h2_hi, wr_ref[0]) + _dot(h2_lo, wr_ref[0]) + _dot(h2_hi, wr_ref[1])) + br_ref[...]
    route_ref[...] = _route_tile(logits, carry_ref)
    cnt_ref[...] = carry_ref[...]


def _merge(x, h, ya, yb_t, yc, yd, mod, nw, wts, tm, rows_per_mod, seq_len):
    m, d = x.shape
    cw = ya.shape[1]
    nchunk = seq_len // tm
    row = lambda i: (i, 0)
    const2 = lambda i: (0, 0)
    w_specs = [_resident(w.shape) for w in wts]
    return pl.pallas_call(
        _merge_kernel,
        grid=(m // tm,),
        in_specs=[pl.BlockSpec((tm, d), row), pl.BlockSpec((tm, d), row),
                  pl.BlockSpec((tm, cw), row),
                  pl.BlockSpec((tm, cw), lambda i: (i % nchunk, i // nchunk)),
                  pl.BlockSpec((tm, cw), row),
                  pl.BlockSpec((tm, yd.shape[1]), row),
                  pl.BlockSpec((None, 8, d), lambda i: ((i * tm) // rows_per_mod, 0, 0)),
                  pl.BlockSpec((1, d), const2)] + w_specs,
        out_specs=[pl.BlockSpec((tm, d), row), pl.BlockSpec((tm, d), row),
                   pl.BlockSpec((tm, ROUTE_LANES), row), pl.BlockSpec((8, ROUTE_LANES), const2)],
        out_shape=[jax.ShapeDtypeStruct((m, d), F32), jax.ShapeDtypeStruct((m, d), F32),
                   jax.ShapeDtypeStruct((m, ROUTE_LANES), F32), jax.ShapeDtypeStruct((8, ROUTE_LANES), F32)],
        scratch_shapes=[pltpu.VMEM((8, ROUTE_LANES), F32)],
        compiler_params=_cparams("arbitrary"),
        name="merge",
    )(x, h, ya, yb_t, yc, yd, mod, nw, *wts)


ROUTE_TM = 512
EXPERT_TR = 512
MOVE_TM = 512
INPROJ_TM = 1024
R_E1, R_E2, R_RANK1, R_RANK2, R_W1, R_W2 = range(6)


def _route_tile(lg, carry_ref):
    lane = lax.broadcasted_iota(jnp.int32, lg.shape, 1).astype(F32)
    neg = jnp.float32(-jnp.inf)
    big = jnp.float32(ROUTE_LANES)
    gl = jnp.where(lane < MOE_GROUPS, lg, neg)
    gm = jnp.max(gl, axis=-1, keepdims=True)
    g_prob = 1.0 / jnp.sum(jnp.exp(gl - gm), axis=-1, keepdims=True)
    gidx = jnp.min(jnp.where(gl == gm, lane, big), axis=-1, keepdims=True)
    e0 = EXPERT_LANE0 + gidx * MOE_EXPERTS
    el = jnp.where((lane >= e0) & (lane < e0 + MOE_EXPERTS), lg, neg)
    m1 = jnp.max(el, axis=-1, keepdims=True)
    i1 = jnp.min(jnp.where(el == m1, lane, big), axis=-1, keepdims=True)
    el2 = jnp.where(lane == i1, neg, el)
    m2 = jnp.max(el2, axis=-1, keepdims=True)
    i2 = jnp.min(jnp.where(el2 == m2, lane, big), axis=-1, keepdims=True)
    r = jnp.exp(m2 - m1)
    w1 = 1.0 / (1.0 + r)
    w2 = r * w1
    two_hot = jnp.where((lane == i1) | (lane == i2), 1.0, 0.0)
    tm = lg.shape[0]
    ltri = (lax.broadcasted_iota(jnp.int32, (tm, tm), 1) < lax.broadcasted_iota(jnp.int32, (tm, tm), 0))
    rank = carry_ref[0:1, :] + _dot(ltri.astype(BF16), two_hot.astype(BF16))
    pick = lambda idx: jnp.sum(jnp.where(lane == idx, rank, 0.0), axis=-1, keepdims=True)
    rec = jnp.zeros_like(lg)
    for col, val in ((R_E1, i1), (R_E2, i2), (R_RANK1, pick(i1)), (R_RANK2, pick(i2)),
                     (R_W1, w1 * g_prob), (R_W2, w2 * g_prob)):
        rec = jnp.where(lane == col, val, rec)
    carry_ref[...] = carry_ref[...] + jnp.sum(two_hot, axis=0, keepdims=True)
    return rec


def _plan_kernel(cnt_ref, seg_ref, tile_ref):
    cnt = cnt_ref[...]
    lane = lax.broadcasted_iota(jnp.int32, cnt.shape, 1)
    is_e = (lane >= EXPERT_LANE0) & (lane < EXPERT_LANE0 + N_EXPERTS)
    size = jnp.where(is_e, jnp.floor((cnt + (EXPERT_TR - 1)) * (1.0 / EXPERT_TR)) * EXPERT_TR, 0.0)
    upper = (lax.broadcasted_iota(jnp.int32, (ROUTE_LANES, ROUTE_LANES), 0)
             < lax.broadcasted_iota(jnp.int32, (ROUTE_LANES, ROUTE_LANES), 1)).astype(F32)
    start = jnp.dot(size, upper, precision=HIGHEST, preferred_element_type=F32)
    end = start + size
    total = jnp.max(end, axis=-1, keepdims=True)
    row = lax.broadcasted_iota(jnp.int32, cnt.shape, 0)
    seg = jnp.where(row == SEG_START, start, jnp.where(row == SEG_SIZE, size, jnp.where(
        row == SEG_END, end, jnp.where(row == SEG_TILES, total * (1.0 / EXPERT_TR), cnt))))
    seg_ref[...] = seg.astype(jnp.int32)
    nt = tile_ref.shape[0]
    t0 = (lax.broadcasted_iota(jnp.int32, (nt, ROUTE_LANES), 0) * EXPERT_TR).astype(F32)
    lane_t = lax.broadcasted_iota(jnp.int32, (nt, ROUTE_LANES), 1)
    done = jnp.where((lane_t >= EXPERT_LANE0) & (lane_t < EXPERT_LANE0 + N_EXPERTS) & (end[0:1, :] <= t0), 1.0, 0.0)
    te = jnp.minimum(jnp.sum(done, axis=-1, keepdims=True), N_EXPERTS - 1.0)
    tile_ref[...] = jnp.broadcast_to(te, (nt, ROUTE_LANES)).astype(jnp.int32)


def _plan(counts, n_tiles):
    nt = -(-n_tiles // 8) * 8
    return pl.pallas_call(
        _plan_kernel,
        out_shape=[jax.ShapeDtypeStruct((8, ROUTE_LANES), jnp.int32),
                   jax.ShapeDtypeStruct((nt, ROUTE_LANES), jnp.int32)],
        name="plan",
    )(counts)


def _positions_kernel(route_ref, seg_ref, pos_ref):
    rec = route_ref[...]
    start = seg_ref[0:1, :].astype(F32)
    lane = lax.broadcasted_iota(jnp.int32, rec.shape, 1).astype(F32)
    seg_start = lambda col: jnp.sum(jnp.where(lane == rec[:, col:col + 1], start, 0.0), axis=-1, keepdims=True)
    p1 = seg_start(R_E1) + rec[:, R_RANK1:R_RANK1 + 1]
    p2 = seg_start(R_E2) + rec[:, R_RANK2:R_RANK2 + 1]
    both = jnp.where(lane == 0.0, p1, jnp.where(lane == 1.0, p2, 0.0))
    pos_ref[...] = jnp.transpose(both)[0:8, :].astype(jnp.int32)


def _positions(route, seg):
    m = route.shape[0]
    return pl.pallas_call(
        _positions_kernel, grid=(m // ROUTE_TM,),
        in_specs=[pl.BlockSpec((ROUTE_TM, ROUTE_LANES), lambda i: (i, 0)),
                  pl.BlockSpec((8, ROUTE_LANES), lambda i: (0, 0))],
        out_specs=pl.BlockSpec((8, ROUTE_TM), lambda i: (0, i)),
        out_shape=jax.ShapeDtypeStruct((8, m), jnp.int32),
        compiler_params=_cparams("parallel"), name="positions",
    )(route, seg)


SEG_START, SEG_SIZE, SEG_END, SEG_TILES, SEG_COUNT = range(5)


def _row_copies(src_of, dst_of, sem):
    copies = [pltpu.make_async_copy(src_of(j, k), dst_of(j, k), sem) for j in range(MOVE_TM) for k in (0, 1)]
    for n, c in enumerate(copies):
        c.start(priority=n % 2)
    for c in copies:
        c.wait()


def _dispatch_kernel(seg_s, pos_s, h2_ref, xs_ref, zbuf, zsem, sem, *, n_tiles):
    @pl.when(pl.program_id(0) == 0)
    def _():
        zbuf[...] = jnp.zeros_like(zbuf)

        def zero_tiles(go):
            for e in range(N_EXPERTS):
                lane = EXPERT_LANE0 + e

                @pl.when(seg_s[SEG_SIZE, lane] > 0)
                def _():
                    start = pl.multiple_of(seg_s[SEG_END, lane] - EXPERT_TR, EXPERT_TR)
                    go(pltpu.make_async_copy(zbuf, xs_ref.at[pl.ds(start, EXPERT_TR)], zsem.at[e]))

                tile = n_tiles - N_EXPERTS + e

                @pl.when(tile >= seg_s[SEG_TILES, 0])
                def _():
                    dst = xs_ref.at[pl.ds(tile * EXPERT_TR, EXPERT_TR)]
                    go(pltpu.make_async_copy(zbuf, dst, zsem.at[N_EXPERTS + e]))

        zero_tiles(lambda c: c.start())
        zero_tiles(lambda c: c.wait())

    _row_copies(lambda j, k: h2_ref.at[pl.ds(j, 1)], lambda j, k: xs_ref.at[pl.ds(pos_s[k, j], 1)], sem)


def _dispatch(h2, pos, seg, n_tiles):
    m, d = h2.shape
    return pl.pallas_call(
        functools.partial(_dispatch_kernel, n_tiles=n_tiles),
        grid_spec=pltpu.PrefetchScalarGridSpec(
            num_scalar_prefetch=1, grid=(m // MOVE_TM,),
            in_specs=[pl.BlockSpec((8, MOVE_TM), lambda i, seg: (0, i), memory_space=pltpu.SMEM),
                      pl.BlockSpec((MOVE_TM, d), lambda i, seg: (i, 0))],
            out_specs=pl.BlockSpec(memory_space=pl.ANY),
            scratch_shapes=[pltpu.VMEM((EXPERT_TR, d), F32), pltpu.SemaphoreType.DMA((2 * N_EXPERTS,)),
                            pltpu.SemaphoreType.DMA]),
        out_shape=jax.ShapeDtypeStruct((n_tiles * EXPERT_TR, d), F32),
        compiler_params=_cparams("arbitrary"), name="dispatch",
    )(seg, pos, h2)


def _experts_kernel(te_s, seg_s, xs_ref, wg_ref, wu_ref, wd_ref, ys_ref):
    used = pl.program_id(0) < seg_s[SEG_TILES, 0]

    @pl.when(used)
    def _():
        x = xs_ref[...].astype(BF16)
        a = _dot(x, wg_ref[...].astype(BF16))
        b = _dot(x, wu_ref[...].astype(BF16))
        ys_ref[...] = _dot((_silu(a) * b).astype(BF16), wd_ref[...].astype(BF16))

    @pl.when(jnp.logical_not(used))
    def _():
        ys_ref[...] = jnp.zeros_like(ys_ref)


def _experts(xs, tile_expert, seg, w_gate, w_up, w_down, layer):
    p, d = xs.shape
    f = w_gate.shape[-1]
    last = lambda j, seg: jnp.minimum(j, seg[SEG_TILES, 0] - 1)
    wmap = lambda j, te, seg: (layer * N_EXPERTS + te[last(j, seg)], 0, 0)
    wspec = lambda shape: pl.BlockSpec(shape, wmap)
    return pl.pallas_call(
        _experts_kernel,
        grid_spec=pltpu.PrefetchScalarGridSpec(
            num_scalar_prefetch=2, grid=(p // EXPERT_TR,),
            in_specs=[pl.BlockSpec((EXPERT_TR, d), lambda j, te, seg: (last(j, seg), 0)),
                      wspec((None, d, f)), wspec((None, d, f)), wspec((None, f, d))],
            out_specs=pl.BlockSpec((EXPERT_TR, d), lambda j, te, seg: (j, 0))),
        out_shape=jax.ShapeDtypeStruct((p, d), F32),
        compiler_params=_cparams("arbitrary"), name="experts",
    )(tile_expert, seg, xs, w_gate, w_up, w_down)


def _combine_kernel(pos_s, route_ref, x1_ref, mod_ref, nw_ref, ys_ref, o_ref, y1buf, y2buf, sem, *, final_norm):
    bufs = (y1buf, y2buf)
    _row_copies(lambda j, k: ys_ref.at[pl.ds(pos_s[k, j], 1)], lambda j, k: bufs[k].at[pl.ds(j, 1)], sem)
    rec = route_ref[...]
    moe = rec[:, R_W1:R_W1 + 1] * y1buf[...] + rec[:, R_W2:R_W2 + 1] * y2buf[...]
    x = x1_ref[...] + mod_ref[5:6, :] * moe
    if final_norm:
        x = x * lax.rsqrt(jnp.mean(x * x, axis=-1, keepdims=True) + EPS) * nw_ref[...]
    o_ref[...] = x


def _combine(ys, pos, route, x1, mod, nw_final, rows_per_mod, final_norm):
    m, d = x1.shape
    row = lambda i: (i, 0)
    return pl.pallas_call(
        functools.partial(_combine_kernel, final_norm=final_norm),
        grid=(m // MOVE_TM,),
        in_specs=[pl.BlockSpec((8, MOVE_TM), lambda i: (0, i), memory_space=pltpu.SMEM),
                  pl.BlockSpec((MOVE_TM, ROUTE_LANES), row), pl.BlockSpec((MOVE_TM, d), row),
                  pl.BlockSpec((None, 8, d), lambda i: ((i * MOVE_TM) // rows_per_mod, 0, 0)),
                  pl.BlockSpec((1, d), lambda i: (0, 0)),
                  pl.BlockSpec(memory_space=pl.ANY)],
        out_specs=pl.BlockSpec((MOVE_TM, d), row),
        out_shape=jax.ShapeDtypeStruct((m, d), F32),
        scratch_shapes=[pltpu.VMEM((MOVE_TM, d), F32), pltpu.VMEM((MOVE_TM, d), F32), pltpu.SemaphoreType.DMA],
        compiler_params=_cparams("arbitrary"), name="combine",
    )(pos, route, x1, mod, nw_final, ys)


def _layer_weights(P, l):
    d = P['w_in'].shape[1]
    cw = P['pool_scale'].shape[1]
    hw = P['hyena_conv'].shape[2]
    o1, o2, o3 = cw, cw + hw, cw + hw + 3 * cw
    na = (P['w_in'].shape[2] - o3) // 3
    cols = ((o1, hw), (o2, 3 * cw), (o3, na), (o3 + na, na), (o3 + 2 * na, na), (0, cw))
    gw = cw // len(POOL_WINDOWS)
    pool_bd = jnp.zeros((cw, cw), F32)
    for g in range(len(POOL_WINDOWS)):
        pool_bd = pool_bd.at[g * gw:(g + 1) * gw, g * gw:(g + 1) * gw].set(P['pool_w'][l, g])
    wr = jnp.concatenate([P['w_route_group'][l],
                          jnp.transpose(P['w_route_exp'][l], (1, 0, 2)).reshape(d, N_EXPERTS)], axis=1)
    wr = jnp.zeros((d, ROUTE_LANES), F32).at[:, :wr.shape[1]].set(wr)
    wr_hi = wr.astype(BF16)
    wr_lo = (wr - wr_hi.astype(F32)).astype(BF16)
    br = jnp.concatenate([P['b_route_group'][l], P['b_route_exp'][l].reshape(-1)])
    br = jnp.zeros((1, ROUTE_LANES), F32).at[0, :br.shape[0]].set(br)
    return dict(
        cols=cols, w_in=P['w_in'][l].astype(BF16), norm_mix=P['norm_mix'][l][None], norm_ffn=P['norm_ffn'][l][None],
        pool_bd=pool_bd.astype(BF16), pool_scale=P['pool_scale'][l][None], sconv_w=P['sconv_w'][l],
        hyena_conv=P['hyena_conv'][l], hyena_bias=P['hyena_bias'][l],
        merge=(P['w_gate'][l].astype(BF16), P['b_gate'][l][None], P['w_br_a'][l].astype(BF16),
               P['w_br_b'][l].astype(BF16), P['w_br_c'][l].astype(BF16), P['w_br_d'][l].astype(BF16),
               P['w_out'][l].astype(BF16), jnp.stack([wr_hi, wr_lo]), br),
    )


def _run_stream(x3, mods, LW, EW, nw_final, hy, attend, depth):
    b, seq_len, d = x3.shape
    m = b * seq_len
    x = x3.reshape(m, d)
    rows_per_mod = m // mods.shape[1]
    tm = min(512, seq_len)
    tc = min(256, seq_len)
    kvs = []
    for l in range(depth):
        W = LW[l]
        mod = mods[l]
        cw = W['pool_scale'].shape[1]
        h, ph, ps, q, k, v, pp = _inproj(x, mod, W['norm_mix'], W['w_in'], W['cols'],
                                         min(INPROJ_TM, rows_per_mod), rows_per_mod)
        kvs.append((k, v))
        ya, yc, u_t, x0_t = _local_mixers(pp, ps, ph, W['pool_bd'], W['pool_scale'], W['sconv_w'],
                                          W['hyena_conv'], seq_len, tc)
        yb_t = _hyena_conv(u_t, x0_t, hy['fwd_bf'], hy['inv_c'], hy['inv_s'], hy['spec'][l],
                           W['hyena_bias'], cw)
        yd = attend(l, q, k, v)
        x1, h2, route, counts = _merge(x, h, ya, yb_t, yc, yd, mod, W['norm_ffn'], W['merge'], tm,
                                       rows_per_mod, seq_len)
        n_tiles = 2 * m // EXPERT_TR + N_EXPERTS
        seg, tile_tab = _plan(counts, n_tiles)
        pos = _positions(route, seg)
        xs = _dispatch(h2, pos, seg, n_tiles)
        ys = _experts(xs, tile_tab[:, 0], seg, EW[0], EW[1], EW[2], l)
        x = _combine(ys, pos, route, x1, mod, nw_final, rows_per_mod, final_norm=(l == depth - 1))
    return x, kvs


def _hyena_setup(seq_len, P, depth):
    fwd32, inv_c, inv_s = _dft_matrices(seq_len)
    spec = []
    for l in range(depth):
        filt = _hyena_filter(seq_len, P['hyena_f1'][l], P['hyena_fb1'][l], P['hyena_f2'][l],
                             P['hyena_fb2'][l], P['hyena_f3'][l], P['hyena_freq'][l], P['hyena_decay'][l])
        spec.append(_filter_spectrum(fwd32, filt))
    return dict(fwd_bf=fwd32.astype(BF16), inv_c=inv_c, inv_s=inv_s, spec=spec)


def kernel(x_prompt, x_sample, cache_k, cache_v, c, c_ctx, w_ada, b_ada, norm_mix, w_in, w_gate, b_gate, pool_w, pool_scale, hyena_conv, hyena_f1, hyena_fb1, hyena_f2, hyena_fb2, hyena_f3, hyena_freq, hyena_decay, hyena_bias, sconv_w, na_rpb, w_br_a, w_br_b, w_br_c, w_br_d, w_out, norm_ffn, w_route_group, b_route_group, w_route_exp, b_route_exp, w_e_gate, w_e_up, w_e_down, norm_final):
    P = dict(w_in=w_in, w_gate=w_gate, b_gate=b_gate, pool_w=pool_w, pool_scale=pool_scale,
             hyena_conv=hyena_conv, hyena_f1=hyena_f1, hyena_fb1=hyena_fb1, hyena_f2=hyena_f2,
             hyena_fb2=hyena_fb2, hyena_f3=hyena_f3, hyena_freq=hyena_freq, hyena_decay=hyena_decay,
             hyena_bias=hyena_bias, sconv_w=sconv_w, w_br_a=w_br_a, w_br_b=w_br_b, w_br_c=w_br_c,
             w_br_d=w_br_d, w_out=w_out, norm_mix=norm_mix, norm_ffn=norm_ffn,
             w_route_group=w_route_group, b_route_group=b_route_group, w_route_exp=w_route_exp,
             b_route_exp=b_route_exp, w_e_gate=w_e_gate, w_e_up=w_e_up, w_e_down=w_e_down)
    depth, d, _ = w_ada.shape
    bp, lp, _ = x_prompt.shape
    bs, ls, _ = x_sample.shape
    assert (ls // GRID_W) % LAT_QROWS == 0 and ls // GRID_W >= LAT_KROWS

    n_c = 1 + bs
    n_pad = -(-n_c // 8) * 8
    cvecs = jnp.zeros((n_pad, d), F32).at[0].set(c_ctx).at[1:n_c].set(c)
    ada = _ada(cvecs, w_ada, b_ada).reshape(depth, n_pad, 6, d)
    ada = jnp.concatenate([ada, jnp.zeros((depth, n_pad, 2, d), F32)], axis=2)

    LW = [_layer_weights(P, l) for l in range(depth)]
    nw_final = norm_final[None]
    f = w_e_gate.shape[-1]
    EW = (w_e_gate.reshape(depth * N_EXPERTS, d, f), w_e_up.reshape(depth * N_EXPERTS, d, f),
          w_e_down.reshape(depth * N_EXPERTS, f, d))

    hy_p = _hyena_setup(lp, P, depth)
    xp, kv_p = _run_stream(x_prompt, ada[:, 0:1], LW, EW, nw_final, hy_p,
                           lambda l, q, k, v: _context_attention(q, k, v, lp), depth)
    y_prompt = xp.reshape(bp, lp, d)
    new_k = jnp.stack([k.reshape(bp, lp, N_HEADS, HEAD_DIM) for k, _ in kv_p], axis=1)
    new_v = jnp.stack([v.reshape(bp, lp, N_HEADS, HEAD_DIM) for _, v in kv_p], axis=1)

    hy_s = _hyena_setup(ls, P, depth)
    past = cache_k.shape[2]
    ck = cache_k.reshape(bs, depth, past, N_HEADS * HEAD_DIM)
    cv = cache_v.reshape(bs, depth, past, N_HEADS * HEAD_DIM)
    biases = [_latent_bias(na_rpb[l], ls // GRID_W) for l in range(depth)]
    xs, _ = _run_stream(x_sample, ada[:, 1:n_c], LW, EW, nw_final, hy_s,
                        lambda l, q, k, v: _latent_attention(q, k, v, ck, cv, l, biases[l], ls),
                        depth)
    y_sample = xs.reshape(bs, ls, d)
    return (y_prompt, y_sample, new_k, new_v)
```
